```python
import math
import jax, jax.numpy as jnp
from jax import lax
import numpy as np

D_MODEL = 4096
BATCH = 16
SEQ = 256
DEPTH = 1
DEC_BATCH = 4
DEC_SEQ = 1024
PAST_LEN = 256

GRID_W = 64
MIX_WIDTH = D_MODEL
CONV_WIDTH = MIX_WIDTH // 2
CONV_K = 31
RET_WIDTH = MIX_WIDTH - CONV_WIDTH
RET_HEADS = 8
RET_DK = RET_WIDTH // RET_HEADS
RET_DV = RET_WIDTH // RET_HEADS
CHUNK = 128
ROPE_BASE = 10000.0
N_GROUPS = 4
EXPERTS_PER_GROUP = 8
N_EXPERTS = N_GROUPS * EXPERTS_PER_GROUP
TOP_K = 2
D_EXPERT = D_MODEL // 4
MOE_BLOCK = 128
IN_COLS = 2 * CONV_WIDTH + 4 * RET_WIDTH
DEEPNORM_ALPHA = (2.0 * DEPTH) ** 0.25
DEEPNORM_BETA = (8.0 * DEPTH) ** -0.25
LN_EPS = 1e-5

kernel_name = 'hybrid_conv_retention_hmoe_diffusion_step'


def layer_norm(x, g=None, b=None):
    xf = x.astype(jnp.float32)
    mu = xf.mean(-1, keepdims=True)
    var = jnp.square(xf - mu).mean(-1, keepdims=True)
    y = (xf - mu) * lax.rsqrt(var + LN_EPS)
    if g is not None:
        y = y * g.astype(jnp.float32) + b.astype(jnp.float32)
    return y.astype(x.dtype)


def modulation(cond, w_mod, b_mod):
    m = jax.nn.silu(cond) @ w_mod + b_mod
    return jnp.split(m[:, None, :], 6, axis=-1)


def rotate(x, ang):
    half = x.shape[-1] // 2
    x1, x2 = x[..., :half], x[..., half:]
    cos = jnp.cos(ang).astype(x.dtype)
    sin = jnp.sin(ang).astype(x.dtype)
    return jnp.concatenate([x1 * cos - x2 * sin, x1 * sin + x2 * cos], axis=-1)


def axial_rope(x):
    n_tok = x.shape[2]
    rows = n_tok // GRID_W
    r_idx = jnp.repeat(jnp.arange(rows), GRID_W).astype(jnp.float32)
    c_idx = jnp.tile(jnp.arange(GRID_W), rows).astype(jnp.float32)
    dq = x.shape[-1] // 2
    inv = ROPE_BASE ** (-jnp.arange(dq // 2, dtype=jnp.float32) / (dq // 2))
    xr = rotate(x[..., :dq], r_idx[:, None] * inv)
    xc = rotate(x[..., dq:], c_idx[:, None] * inv)
    return jnp.concatenate([xr, xc], axis=-1)


def retention_scan(q, k, v, log_g, s0):
    B, H, N, _ = q.shape
    nc = N // CHUNK
    dt = q.dtype
    idx = jnp.arange(CHUNK, dtype=jnp.float32)
    diff = idx[:, None] - idx[None, :]
    decay_in = jnp.where(diff >= 0, jnp.exp(jnp.maximum(diff, 0.0) * log_g[:, None, None]), 0.0).astype(dt)
    xi = jnp.exp((idx + 1.0)[None, :] * log_g[:, None]).astype(dt)[None, :, :, None]
    zeta = jnp.exp((CHUNK - 1.0 - idx)[None, :] * log_g[:, None]).astype(dt)[None, :, :, None]
    g_chunk = jnp.exp(CHUNK * log_g).astype(dt)[None, :, None, None]

    def to_chunks(t):
        return t.reshape(B, H, nc, CHUNK, t.shape[-1]).transpose(2, 0, 1, 3, 4)

    def step(s, qkv):
        qc, kc, vc = qkv
        scores = jnp.einsum('bhid,bhjd->bhij', qc, kc) * decay_in
        o = jnp.einsum('bhij,bhjv->bhiv', scores, vc) + jnp.einsum('bhid,bhdv->bhiv', qc, s) * xi
        s = g_chunk * s + jnp.einsum('bhjd,bhjv->bhdv', kc * zeta, vc)
        return s, o

    s_fin, o = lax.scan(step, s0.astype(dt), (to_chunks(q), to_chunks(k), to_chunks(v)))
    o = o.transpose(1, 2, 0, 3, 4).reshape(B, H, N, v.shape[-1])
    return o, s_fin


def bidir_retention(q, k, v, log_g_fwd, log_g_bwd, s0_fwd, s0_bwd):
    o_f, s_f = retention_scan(q, k, v, log_g_fwd, s0_fwd)
    flip = lambda t: jnp.flip(t, axis=2)
    o_b, s_b = retention_scan(flip(q), flip(k), flip(v), log_g_bwd, s0_bwd)
    return o_f + flip(o_b), s_f, s_b


def conv_module(c_val, c_gate, conv_w, conv_b, ln_g, ln_b):
    u = c_val * jax.nn.sigmoid(c_gate)
    u = lax.conv_general_dilated(
        u, conv_w.astype(u.dtype)[:, None, :], window_strides=(1,),
        padding=[(CONV_K // 2, CONV_K // 2)], dimension_numbers=('NWC', 'WIO', 'NWC'),
        feature_group_count=CONV_WIDTH)
    return jax.nn.silu(layer_norm(u + conv_b, ln_g, ln_b))


def hier_moe(h, w_grp, b_grp, w_exp, b_exp, w_gate, w_up, w_down):
    T, D = h.shape
    f32 = jnp.float32
    p_grp = jax.nn.softmax((h @ w_grp).astype(f32) + b_grp.astype(f32), axis=-1)
    grp = jnp.argmax(p_grp, axis=-1)
    grp_prob = jnp.take_along_axis(p_grp, grp[:, None], axis=-1)
    le = ((h @ w_exp).astype(f32) + b_exp.astype(f32)).reshape(T, N_GROUPS, EXPERTS_PER_GROUP)
    le = jnp.take_along_axis(le, grp[:, None, None], axis=1)[:, 0]
    top_p, top_i = lax.top_k(jax.nn.softmax(le, axis=-1), TOP_K)
    gates = grp_prob * top_p / top_p.sum(-1, keepdims=True)
    eid = (grp[:, None] * EXPERTS_PER_GROUP + top_i).reshape(-1).astype(jnp.int32)
    tok = jnp.repeat(jnp.arange(T, dtype=jnp.int32), TOP_K)
    gw = gates.reshape(-1).astype(h.dtype)
    order = jnp.argsort(eid, stable=True)
    eid_s, tok_s, gw_s = eid[order], tok[order], gw[order]
    counts = jnp.bincount(eid, length=N_EXPERTS)
    start = jnp.cumsum(counts) - counts
    padded = (counts + MOE_BLOCK - 1) // MOE_BLOCK * MOE_BLOCK
    pend = jnp.cumsum(padded)
    pstart = pend - padded
    n_assign = T * TOP_K
    pos = pstart[eid_s] + jnp.arange(n_assign) - start[eid_s]
    n_rows = -(-n_assign // MOE_BLOCK) * MOE_BLOCK + N_EXPERTS * MOE_BLOCK
    n_blocks = n_rows // MOE_BLOCK
    buf_tok = jnp.full((n_rows,), T, jnp.int32).at[pos].set(tok_s)
    buf_w = jnp.zeros((n_rows,), h.dtype).at[pos].set(gw_s)
    blk_e = jnp.minimum(jnp.searchsorted(pend, jnp.arange(n_blocks) * MOE_BLOCK, side='right'), N_EXPERTS - 1)
    h_pad = jnp.concatenate([h, jnp.zeros((1, D), h.dtype)], axis=0)
    xb = h_pad[buf_tok].reshape(n_blocks, MOE_BLOCK, D)

    def expert_block(args):
        xblk, e = args
        return (jax.nn.silu(xblk @ w_gate[e]) * (xblk @ w_up[e])) @ w_down[e]

    yb = lax.map(expert_block, (xb, blk_e)).reshape(n_rows, D)
    return jax.ops.segment_sum(yb * buf_w[:, None], buf_tok, num_segments=T + 1)[:T]


def trunk_layer(x, cond, s0_fwd, s0_bwd, latent,
                w_mod, b_mod, w_in, conv_w, conv_b, conv_ln_g, conv_ln_b,
                dec_fwd, dec_bwd, ret_gn_g, w_out, ln1_g, ln1_b,
                w_grp, b_grp, w_exp, b_exp, w_gate, w_up, w_down, ln2_g, ln2_b):
    B, N, D = x.shape
    sh1, sc1, gt1, sh2, sc2, gt2 = modulation(cond, w_mod, b_mod)
    h = layer_norm(x) * (1 + sc1) + sh1
    z = h @ w_in
    o1 = CONV_WIDTH
    o2 = 2 * CONV_WIDTH
    o3 = o2 + RET_WIDTH
    o4 = o3 + RET_WIDTH
    o5 = o4 + RET_WIDTH
    c_val, c_gate = z[..., :o1], z[..., o1:o2]
    q, k, v, g = z[..., o2:o3], z[..., o3:o4], z[..., o4:o5], z[..., o5:]
    u = conv_module(c_val, c_gate, conv_w, conv_b, conv_ln_g, conv_ln_b)
    heads = lambda t: t.reshape(B, N, RET_HEADS, -1).transpose(0, 2, 1, 3)
    q = heads(q)
    k = heads(k) * (RET_DK ** -0.5)
    v = heads(v)
    if latent:
        q = axial_rope(q)
        k = axial_rope(k)
    if s0_fwd is None:
        s0_fwd = jnp.zeros((B, RET_HEADS, RET_DK, RET_DV), x.dtype)
        s0_bwd = jnp.zeros((B, RET_HEADS, RET_DK, RET_DV), x.dtype)
    o, s_fwd, s_bwd = bidir_retention(
        q, k, v, jax.nn.log_sigmoid(dec_fwd.astype(jnp.float32)),
        jax.nn.log_sigmoid(dec_bwd.astype(jnp.float32)), s0_fwd, s0_bwd)
    r = jax.nn.silu(g) * (layer_norm(o).transpose(0, 2, 1, 3).reshape(B, N, RET_WIDTH) * ret_gn_g)
    y = jnp.concatenate([u, r], axis=-1) @ w_out
    x = layer_norm(DEEPNORM_ALPHA * x + gt1 * y, ln1_g, ln1_b)
    h = layer_norm(x) * (1 + sc2) + sh2
    f = hier_moe(h.reshape(B * N, D), w_grp, b_grp, w_exp, b_exp, w_gate, w_up, w_down).reshape(B, N, D)
    x = layer_norm(DEEPNORM_ALPHA * x + gt2 * f, ln2_g, ln2_b)
    return x, s_fwd, s_bwd


def setup_inputs(seed: int = 0) -> dict:
    key = jax.random.key(seed)
    ks = jax.random.split(key, 28)
    nrm = lambda k, shape, s: jax.random.normal(k, shape, jnp.float32) * s
    g0 = 1.0 - 2.0 ** (-5.0 - np.arange(RET_HEADS, dtype=np.float32))
    logit0 = jnp.asarray(np.log(g0) - np.log1p(-g0), jnp.float32)
    st_shape = (DEC_BATCH, DEPTH, RET_HEADS, RET_DK, RET_DV)
    return {
        'x_prompt': nrm(ks[0], (BATCH, SEQ, D_MODEL), 1.0),
        'x_sample': nrm(ks[1], (DEC_BATCH, DEC_SEQ, D_MODEL), 1.0),
        'state_ret_fwd': nrm(ks[2], st_shape, 0.5),
        'state_ret_bwd': nrm(ks[3], st_shape, 0.5),
        'c': nrm(ks[4], (DEC_BATCH, D_MODEL), 1.0),
        'c_ctx': nrm(ks[5], (D_MODEL,), 1.0),
        'w_mod': nrm(ks[6], (DEPTH, D_MODEL, 6 * D_MODEL), 0.5 * D_MODEL ** -0.5),
        'b_mod': nrm(ks[7], (DEPTH, 6 * D_MODEL), 0.02),
        'w_in': nrm(ks[8], (DEPTH, D_MODEL, IN_COLS), D_MODEL ** -0.5),
        'conv_w': nrm(ks[9], (DEPTH, CONV_K, CONV_WIDTH), CONV_K ** -0.5),
        'conv_b': nrm(ks[10], (DEPTH, CONV_WIDTH), 0.02),
        'conv_ln_g': 1.0 + nrm(ks[11], (DEPTH, CONV_WIDTH), 0.02),
        'conv_ln_b': nrm(ks[12], (DEPTH, CONV_WIDTH), 0.02),
        'ret_decay_fwd': logit0[None, :] + nrm(ks[13], (DEPTH, RET_HEADS), 0.1),
        'ret_decay_bwd': logit0[None, :] + nrm(ks[14], (DEPTH, RET_HEADS), 0.1),
        'ret_gn_g': 1.0 + nrm(ks[15], (DEPTH, RET_WIDTH), 0.02),
        'w_out': nrm(ks[16], (DEPTH, MIX_WIDTH, D_MODEL), MIX_WIDTH ** -0.5 * DEEPNORM_BETA),
        'ln1_g': 1.0 + nrm(ks[17], (DEPTH, D_MODEL), 0.02),
        'ln1_b': nrm(ks[18], (DEPTH, D_MODEL), 0.02),
        'w_grp': nrm(ks[19], (DEPTH, D_MODEL, N_GROUPS), D_MODEL ** -0.5),
        'b_grp': nrm(ks[20], (DEPTH, N_GROUPS), 0.01),
        'w_exp': nrm(ks[21], (DEPTH, D_MODEL, N_EXPERTS), D_MODEL ** -0.5),
        'b_exp': nrm(ks[22], (DEPTH, N_EXPERTS), 0.01),
        'w_gate': nrm(ks[23], (DEPTH, N_EXPERTS, D_MODEL, D_EXPERT), D_MODEL ** -0.5),
        'w_up': nrm(ks[24], (DEPTH, N_EXPERTS, D_MODEL, D_EXPERT), D_MODEL ** -0.5),
        'w_down': nrm(ks[25], (DEPTH, N_EXPERTS, D_EXPERT, D_MODEL), D_EXPERT ** -0.5 * DEEPNORM_BETA),
        'ln2_g': 1.0 + nrm(ks[26], (DEPTH, D_MODEL), 0.02),
        'ln2_b': nrm(ks[27], (DEPTH, D_MODEL), 0.02),
    }


def reference(x_prompt, x_sample, state_ret_fwd, state_ret_bwd, c, c_ctx,
              w_mod, b_mod, w_in, conv_w, conv_b, conv_ln_g, conv_ln_b,
              ret_decay_fwd, ret_decay_bwd, ret_gn_g, w_out, ln1_g, ln1_b,
              w_grp, b_grp, w_exp, b_exp, w_gate, w_up, w_down, ln2_g, ln2_b):
    y_prompt = x_prompt
    y_sample = x_sample
    new_fwd = []
    new_bwd = []
    for l in range(DEPTH):
        wl = (w_mod[l], b_mod[l], w_in[l], conv_w[l], conv_b[l], conv_ln_g[l], conv_ln_b[l],
              ret_decay_fwd[l], ret_decay_bwd[l], ret_gn_g[l], w_out[l], ln1_g[l], ln1_b[l],
              w_grp[l], b_grp[l], w_exp[l], b_exp[l], w_gate[l], w_up[l], w_down[l], ln2_g[l], ln2_b[l])
        y_prompt, s_f, s_b = trunk_layer(y_prompt, c_ctx[None, :], None, None, False, *wl)
        new_fwd.append(s_f)
        new_bwd.append(s_b)
        y_sample, _, _ = trunk_layer(y_sample, c, state_ret_fwd[:, l], state_ret_bwd[:, l], True, *wl)
    new_state_ret_fwd = jnp.stack(new_fwd, axis=1)
    new_state_ret_bwd = jnp.stack(new_bwd, axis=1)
    return (y_prompt, y_sample, new_state_ret_fwd, new_state_ret_bwd)
```

```python
import functools

import numpy as np
import jax
import jax.numpy as jnp
from jax import lax
from jax.experimental import pallas as pl
from jax.experimental.pallas import tpu as pltpu

F32 = jnp.float32
BF16 = jnp.bfloat16

LN_EPS = 1e-5
CONV_K = 31
CONV_HALO = 16
RET_HEADS = 8
RET_D = 256
CHUNK = 128
GRID_W = 64
ROPE_BASE = 10000.0
N_GROUPS = 4
EXPERTS_PER_GROUP = 8
N_EXPERTS = N_GROUPS * EXPERTS_PER_GROUP
TOP_K = 2
ROUTER_LANES = 128

VMEM_LIMIT = 56 * 1024 * 1024

TM_PROJ = 1024
TM_ROW = 256
TM_MOE = 256
TF_MOE = 256
TN_MOE = 1024


def _params(*sem):
    return pltpu.CompilerParams(dimension_semantics=tuple(sem), vmem_limit_bytes=VMEM_LIMIT)


def _sigmoid(x):
    return 1.0 / (1.0 + jnp.exp(-x))


def _ln_rows(x):
    mu = jnp.mean(x, axis=-1, keepdims=True)
    xc = x - mu
    var = jnp.mean(xc * xc, axis=-1, keepdims=True)
    return xc * lax.rsqrt(var + LN_EPS)


def _mod_kernel(c_ref, w_ref, b_ref, o_ref):
    c = c_ref[...]
    s = c * _sigmoid(c)
    o_ref[...] = jnp.dot(s.astype(BF16), w_ref[...].astype(BF16), preferred_element_type=F32) + b_ref[...]


def _modulation(cond8, w_mod, b_mod):
    d, n = w_mod.shape
    tn = 512
    return pl.pallas_call(
        _mod_kernel,
        grid=(n // tn,),
        in_specs=[pl.BlockSpec((8, d), lambda j: (0, 0)),
                  pl.BlockSpec((d, tn), lambda j: (0, j)),
                  pl.BlockSpec((1, tn), lambda j: (0, j))],
        out_specs=pl.BlockSpec((8, tn), lambda j: (0, j)),
        out_shape=jax.ShapeDtypeStruct((8, n), F32),
        compiler_params=_params("arbitrary"),
        name="modulation",
    )(cond8, w_mod, b_mod.reshape(1, n))


def _ln_mod_kernel(npb, xp_ref, xs_ref, sc_ref, sh_ref, o_ref):
    def body(x_ref):
        y = _ln_rows(x_ref[...])
        o_ref[...] = (y * (1.0 + sc_ref[...]) + sh_ref[...]).astype(o_ref.dtype)

    i = pl.program_id(0)
    pl.when(i < npb)(lambda: body(xp_ref))
    pl.when(i >= npb)(lambda: body(xs_ref))


def _ln_modulate(xp, xs, sc, sh, dec_seq):
    (tp, d), ts = xp.shape, xs.shape[0]
    tm = 512
    npb, nsb = tp // tm, ts // tm
    cidx = lambda i: jnp.where(i < npb, 0, 1 + (jnp.maximum(i - npb, 0) * tm) // dec_seq)
    return pl.pallas_call(
        functools.partial(_ln_mod_kernel, npb),
        grid=(npb + nsb,),
        in_specs=[pl.BlockSpec((tm, d), lambda i: (jnp.minimum(i, npb - 1), 0)),
                  pl.BlockSpec((tm, d), lambda i: (jnp.maximum(i - npb, 0), 0)),
                  pl.BlockSpec((None, 1, d), lambda i: (cidx(i), 0, 0)),
                  pl.BlockSpec((None, 1, d), lambda i: (cidx(i), 0, 0))],
        out_specs=pl.BlockSpec((tm, d), lambda i: (i, 0)),
        out_shape=jax.ShapeDtypeStruct((tp + ts, d), BF16),
        compiler_params=_params("arbitrary"),
        name="ln_modulate",
    )(xp, xs, sc, sh)


def _glu_proj_kernel(h_ref, wv_ref, wg_ref, o_ref):
    h = h_ref[...]
    a = jnp.dot(h, wv_ref[...].astype(BF16), preferred_element_type=F32)
    g = jnp.dot(h, wg_ref[...].astype(BF16), preferred_element_type=F32)
    o_ref[...] = a * _sigmoid(g)


def _glu_proj(h, w_in, conv_width):
    t, d = h.shape
    tm, tn = TM_PROJ, 256
    goff = conv_width // tn
    return pl.pallas_call(
        _glu_proj_kernel,
        grid=(t // tm, conv_width // tn),
        in_specs=[pl.BlockSpec((tm, d), lambda i, j: (i, 0)),
                  pl.BlockSpec((d, tn), lambda i, j: (0, j)),
                  pl.BlockSpec((d, tn), lambda i, j: (0, j + goff))],
        out_specs=pl.BlockSpec((tm, tn), lambda i, j: (i, j)),
        out_shape=jax.ShapeDtypeStruct((t, conv_width), F32),
        compiler_params=_params("arbitrary", "arbitrary"),
        name="glu_proj",
    )(h, w_in, w_in)


def _qkvg_proj_kernel(npb, tiles_per_part, k_scale, h_ref, w_ref, cos_ref, sin_ref, o_ref):
    i, j = pl.program_id(0), pl.program_id(1)
    z = jnp.dot(h_ref[...], w_ref[...].astype(BF16), preferred_element_type=F32)
    tpp = tiles_per_part

    @pl.when(j >= 3 * tpp)
    def _():
        o_ref[...] = (z * _sigmoid(z)).astype(o_ref.dtype)

    @pl.when((j >= 2 * tpp) & (j < 3 * tpp))
    def _():
        o_ref[...] = z.astype(o_ref.dtype)

    @pl.when(j < 2 * tpp)
    def _():
        zz = z * jnp.where(j >= tpp, k_scale, 1.0).astype(F32)

        @pl.when(i < npb)
        def _():
            o_ref[...] = zz.astype(o_ref.dtype)

        @pl.when(i >= npb)
        def _():
            for s in range(zz.shape[1] // 128):
                cs = slice(s * 128, (s + 1) * 128)
                ts_ = slice((s * 128) % RET_D, (s * 128) % RET_D + 128)
                zs = zz[:, cs]
                o_ref[:, cs] = (zs * cos_ref[:, ts_] + pltpu.roll(zs, 64, axis=1) * sin_ref[:, ts_]).astype(o_ref.dtype)


def _qkvg_proj(h, w_in, cos_t, sin_t, col0, n_cols, tp):
    t, d = h.shape
    tm, tn = TM_PROJ, 512
    assert cos_t.shape == (tm, RET_D), "a projection row tile is one latent sequence"
    npb = tp // tm
    tpp = (n_cols // 4) // tn
    return pl.pallas_call(
        functools.partial(_qkvg_proj_kernel, npb, tpp, RET_D ** -0.5),
        grid=(t // tm, n_cols // tn),
        in_specs=[pl.BlockSpec((tm, d), lambda i, j: (i, 0)),
                  pl.BlockSpec((d, tn), lambda i, j: (0, j + col0 // tn)),
                  pl.BlockSpec((tm, RET_D), lambda i, j: (0, 0)),
                  pl.BlockSpec((tm, RET_D), lambda i, j: (0, 0))],
        out_specs=pl.BlockSpec((tm, tn), lambda i, j: (i, j)),
        out_shape=jax.ShapeDtypeStruct((t, n_cols), BF16),
        compiler_params=_params("arbitrary", "arbitrary"),
        name="qkvg_proj",
    )(h, w_in, cos_t, sin_t)


def _rope_tables(n_tok, width):
    rows = n_tok // GRID_W
    r_idx = np.repeat(np.arange(rows), GRID_W).astype(np.float64)
    c_idx = np.tile(np.arange(GRID_W), rows).astype(np.float64)
    dq = RET_D // 2
    inv = ROPE_BASE ** (-np.arange(dq // 2, dtype=np.float64) / (dq // 2))
    sign = np.concatenate([-np.ones(dq // 2), np.ones(dq // 2)])
    cos_h, sin_h = [], []
    for idx in (r_idx, c_idx):
        ang = idx[:, None] * inv
        cos_h.append(np.concatenate([np.cos(ang), np.cos(ang)], axis=1))
        sin_h.append(np.concatenate([np.sin(ang), np.sin(ang)], axis=1) * sign)
    cos_h, sin_h = np.concatenate(cos_h, axis=1), np.concatenate(sin_h, axis=1)
    reps = width // RET_D
    return (jnp.asarray(np.tile(cos_h, (1, reps)), F32), jnp.asarray(np.tile(sin_h, (1, reps)), F32))


def _conv_kernel(npb, tiles_per_seq, uc_ref, up_ref, un_ref, w_ref, b_ref, g_ref, bt_ref, o_ref, buf_ref, y_ref):
    r = pl.program_id(0)
    tm = uc_ref.shape[0]
    ncg = buf_ref.shape[0]
    t = lax.rem(jnp.maximum(r - npb, 0), tiles_per_seq)
    is_s = r >= npb
    has_prev = is_s & (t != 0)
    has_next = is_s & (t != tiles_per_seq - 1)
    for cg in range(ncg):
        cs = slice(cg * 128, (cg + 1) * 128)
        buf_ref[cg, 0:CONV_HALO, :] = jnp.where(has_prev, up_ref[:, cs], 0.0)
        buf_ref[cg, CONV_HALO:CONV_HALO + tm, :] = uc_ref[:, cs]
        buf_ref[cg, CONV_HALO + tm:, :] = jnp.where(has_next, un_ref[:, cs], 0.0)

    off = CONV_HALO - CONV_K // 2

    def body(cg, carry):
        acc = jnp.zeros((tm, 128), F32)
        for tap in range(CONV_K):
            acc = acc + buf_ref[cg, off + tap:off + tap + tm, :] * w_ref[cg, tap:tap + 1, :]
        y_ref[cg] = acc + b_ref[cg]
        return carry

    lax.fori_loop(0, ncg, body, 0)

    n_ch = ncg * 128
    tot = y_ref[0]
    for cg in range(1, ncg):
        tot = tot + y_ref[cg]
    mu = jnp.sum(tot, axis=1, keepdims=True) * (1.0 / n_ch)
    sq = jnp.zeros((tm, 128), F32)
    for cg in range(ncg):
        dv = y_ref[cg] - mu
        sq = sq + dv * dv
    var = jnp.sum(sq, axis=1, keepdims=True) * (1.0 / n_ch)
    rstd = lax.rsqrt(var + LN_EPS)
    for cg in range(ncg):
        cs = slice(cg * 128, (cg + 1) * 128)
        v = (y_ref[cg] - mu) * rstd * g_ref[cg] + bt_ref[cg]
        o_ref[:, cs] = (v * _sigmoid(v)).astype(o_ref.dtype)


def _conv_module(u, conv_w, conv_b, ln_g, ln_b, tp, dec_seq):
    t, c = u.shape
    tm = TM_ROW
    ncg = c // 128
    npb = tp // tm
    hb = tm // CONV_HALO
    n_halo_blocks = t // CONV_HALO
    w3 = jnp.zeros((32, c), F32).at[:CONV_K].set(conv_w).reshape(32, ncg, 128).transpose(1, 0, 2)
    vec = lambda a: a.reshape(ncg, 1, 128)
    return pl.pallas_call(
        functools.partial(_conv_kernel, npb, dec_seq // tm),
        grid=(t // tm,),
        in_specs=[pl.BlockSpec((tm, c), lambda r: (r, 0)),
                  pl.BlockSpec((CONV_HALO, c), lambda r: (jnp.maximum(r * hb - 1, 0), 0)),
                  pl.BlockSpec((CONV_HALO, c), lambda r: (jnp.minimum((r + 1) * hb, n_halo_blocks - 1), 0)),
                  pl.BlockSpec((ncg, 32, 128), lambda r: (0, 0, 0)),
                  pl.BlockSpec((ncg, 1, 128), lambda r: (0, 0, 0)),
                  pl.BlockSpec((ncg, 1, 128), lambda r: (0, 0, 0)),
                  pl.BlockSpec((ncg, 1, 128), lambda r: (0, 0, 0))],
        out_specs=pl.BlockSpec((tm, c), lambda r: (r, 0)),
        out_shape=jax.ShapeDtypeStruct((t, c), BF16),
        scratch_shapes=[pltpu.VMEM((ncg, tm + 2 * CONV_HALO, 128), F32),
                        pltpu.VMEM((ncg, tm, 128), F32)],
        compiler_params=_params("arbitrary"),
        name="conv_ln_swish",
    )(u, u, u, w3, vec(conv_b), vec(ln_g), vec(ln_b))


def _retention_kernel(nc, has_init, lg_ref, q_ref, k_ref, v_ref, g_ref, gn_ref, *rest):
    if has_init:
        s0f_ref, s0b_ref, r_ref, o_ref, sf_ref, sb_ref = rest
    else:
        r_ref, sf_ref, sb_ref, o_ref = rest
    hd = pl.program_id(1)
    lgf, lgb = lg_ref[0, hd], lg_ref[1, hd]
    c = CHUNK
    row = lax.broadcasted_iota(jnp.int32, (c, c), 0)
    col = lax.broadcasted_iota(jnp.int32, (c, c), 1)
    diff = (row - col).astype(F32)
    dec = (jnp.where(diff >= 0, jnp.exp(jnp.maximum(diff, 0.0) * lgf), 0.0)
           + jnp.where(diff <= 0, jnp.exp(jnp.maximum(-diff, 0.0) * lgb), 0.0))
    pos = lax.broadcasted_iota(jnp.int32, (c, 1), 0).astype(F32)
    xi_f = jnp.exp((pos + 1.0) * lgf)
    zeta_f = jnp.exp((c - 1.0 - pos) * lgf)
    xi_b = jnp.exp((c - pos) * lgb)
    zeta_b = jnp.exp(pos * lgb)
    gch_f = jnp.exp(jnp.full((1, RET_D), c, F32) * lgf)
    gch_b = jnp.exp(jnp.full((1, RET_D), c, F32) * lgb)

    if has_init:
        sf_ref[...] = s0f_ref[...]
        sb_ref[...] = s0b_ref[...]
    else:
        sf_ref[...] = jnp.zeros_like(sf_ref)
        sb_ref[...] = jnp.zeros_like(sb_ref)

    def chunk(ci):
        sl = slice(ci * c, (ci + 1) * c)
        return q_ref[sl, :], k_ref[sl, :], v_ref[sl, :], sl

    def state_update(s_ref, kc, vc, zeta, gch):
        kz = (kc.astype(F32) * zeta).T.astype(BF16)
        s_ref[...] = gch * s_ref[...] + jnp.dot(kz, vc, preferred_element_type=F32)

    for ci in range(nc):
        qc, kc, vc, sl = chunk(ci)
        s = lax.dot_general(qc, kc, (((1,), (1,)), ((), ())), preferred_element_type=F32)
        p = (s * dec).astype(BF16)
        o = jnp.dot(p, vc, preferred_element_type=F32)
        o = o + jnp.dot(qc, sf_ref[...].astype(BF16), preferred_element_type=F32) * xi_f
        o_ref[sl, :] = o
        state_update(sf_ref, kc, vc, zeta_f, gch_f)

    for ci in reversed(range(nc)):
        qc, kc, vc, sl = chunk(ci)
        o_ref[sl, :] = o_ref[sl, :] + jnp.dot(qc, sb_ref[...].astype(BF16), preferred_element_type=F32) * xi_b
        state_update(sb_ref, kc, vc, zeta_b, gch_b)

    y = _ln_rows(o_ref[...])
    r_ref[...] = (g_ref[...].astype(F32) * (y * gn_ref[...])).astype(r_ref.dtype)


def _retention(qkvg, lg, gn, n, row_blk0, n_seq, s0f=None, s0b=None):
    has_init = s0f is not None
    nh, dd = RET_HEADS, RET_D
    col = lambda part: (lambda b, h: (b + row_blk0, part * nh + h))
    st_spec = pl.BlockSpec((None, None, None, dd, dd), lambda b, h: (b, 0, h, 0, 0))
    in_specs = [pl.BlockSpec(memory_space=pltpu.SMEM),
                pl.BlockSpec((n, dd), col(0)), pl.BlockSpec((n, dd), col(1)),
                pl.BlockSpec((n, dd), col(2)), pl.BlockSpec((n, dd), col(3)),
                pl.BlockSpec((1, dd), lambda b, h: (0, h))]
    args = [lg, qkvg, qkvg, qkvg, qkvg, gn]
    r_shape = jax.ShapeDtypeStruct((n_seq * n, nh * dd), BF16)
    r_spec = pl.BlockSpec((n, dd), lambda b, h: (b, h))
    st_scratch = pltpu.VMEM((dd, dd), F32)
    if has_init:
        in_specs += [st_spec, st_spec]
        args += [s0f, s0b]
        out_specs, out_shape = r_spec, r_shape
        scratch = [pltpu.VMEM((n, dd), F32), st_scratch, st_scratch]
    else:
        st_shape = jax.ShapeDtypeStruct((n_seq, 1, nh, dd, dd), F32)
        out_specs, out_shape = [r_spec, st_spec, st_spec], [r_shape, st_shape, st_shape]
        scratch = [pltpu.VMEM((n, dd), F32)]
    return pl.pallas_call(
        functools.partial(_retention_kernel, n // CHUNK, has_init),
        grid=(n_seq, nh),
        in_specs=in_specs, out_specs=out_specs, out_shape=out_shape,
        scratch_shapes=scratch,
        compiler_params=_params("arbitrary", "arbitrary"),
        name="retention_latent" if has_init else "retention_context",
    )(*args)


def _out_proj_kernel(npb, alpha, u_ref, r_ref, w1_ref, w2_ref, xp_ref, xs_ref, gt_ref, o_ref):
    y = (jnp.dot(u_ref[...], w1_ref[...].astype(BF16), preferred_element_type=F32)
         + jnp.dot(r_ref[...], w2_ref[...].astype(BF16), preferred_element_type=F32))
    i = pl.program_id(0)

    @pl.when(i < npb)
    def _():
        o_ref[...] = alpha * xp_ref[...] + gt_ref[...] * y

    @pl.when(i >= npb)
    def _():
        o_ref[...] = alpha * xs_ref[...] + gt_ref[...] * y


def _out_proj(u, r, w_out, xp, xs, gt, alpha, dec_seq):
    t, kh = u.shape
    d = w_out.shape[1]
    tm, tn = TM_PROJ, 512
    npb = xp.shape[0] // tm
    cidx = lambda i: jnp.where(i < npb, 0, 1 + (jnp.maximum(i - npb, 0) * tm) // dec_seq)
    return pl.pallas_call(
        functools.partial(_out_proj_kernel, npb, alpha),
        grid=(t // tm, d // tn),
        in_specs=[pl.BlockSpec((tm, kh), lambda i, j: (i, 0)),
                  pl.BlockSpec((tm, kh), lambda i, j: (i, 0)),
                  pl.BlockSpec((kh, tn), lambda i, j: (0, j)),
                  pl.BlockSpec((kh, tn), lambda i, j: (1, j)),
                  pl.BlockSpec((tm, tn), lambda i, j: (jnp.minimum(i, npb - 1), j)),
                  pl.BlockSpec((tm, tn), lambda i, j: (jnp.maximum(i - npb, 0), j)),
                  pl.BlockSpec((None, 1, tn), lambda i, j: (cidx(i), 0, j))],
        out_specs=pl.BlockSpec((tm, tn), lambda i, j: (i, j)),
        out_shape=jax.ShapeDtypeStruct((t, d), F32),
        compiler_params=_params("arbitrary", "arbitrary"),
        name="out_proj_residual",
    )(u, r, w_out, w_out, xp, xs, gt)


def _post_mix_kernel(v_ref, g1_ref, b1_ref, sc_ref, sh_ref, wr_ref, br_ref, x1_ref, h2_ref, rt_ref):
    x1 = _ln_rows(v_ref[...]) * g1_ref[...] + b1_ref[...]
    x1_ref[...] = x1
    h2 = _ln_rows(x1) * (1.0 + sc_ref[...]) + sh_ref[...]
    h2_ref[...] = h2
    logits = jnp.dot(h2.astype(BF16), wr_ref[...], preferred_element_type=F32) + br_ref[...]
    lane = lax.broadcasted_iota(jnp.int32, logits.shape, 1)
    big = jnp.int32(ROUTER_LANES)
    neg = jnp.float32(-jnp.inf)

    def first_lane_of_max(vals):
        m = jnp.max(vals, axis=1, keepdims=True)
        return m, jnp.min(jnp.where(vals == m, lane, big), axis=1, keepdims=True)

    lgt = jnp.where(lane < N_GROUPS, logits, neg)
    eg = jnp.exp(lgt - jnp.max(lgt, axis=1, keepdims=True))
    pg = eg / jnp.sum(eg, axis=1, keepdims=True)
    grp_prob, grp = first_lane_of_max(jnp.where(lane < N_GROUPS, pg, -1.0))
    lo = N_GROUPS + grp * EXPERTS_PER_GROUP
    in_grp = (lane >= lo) & (lane < lo + EXPERTS_PER_GROUP)
    let = jnp.where(in_grp, logits, neg)
    ee = jnp.exp(let - jnp.max(let, axis=1, keepdims=True))
    pe = jnp.where(in_grp, ee / jnp.sum(ee, axis=1, keepdims=True), -1.0)
    p1, l1 = first_lane_of_max(pe)
    p2, l2 = first_lane_of_max(jnp.where(lane == l1, -1.0, pe))
    den = p1 + p2
    gate1, gate2 = grp_prob * p1 / den, grp_prob * p2 / den
    e1 = (l1 - N_GROUPS).astype(F32)
    e2 = (l2 - N_GROUPS).astype(F32)
    rt_ref[...] = jnp.where(lane == 0, e1, jnp.where(lane == 1, e2, jnp.where(lane == 2, gate1, jnp.where(lane == 3, gate2, 0.0))))


def _post_mix(v, ln_g, ln_b, sc, sh, w_router, b_router, tp, dec_seq):
    t, d = v.shape
    tm = TM_ROW
    npb = tp // tm
    cidx = lambda i: jnp.where(i < npb, 0, 1 + (jnp.maximum(i - npb, 0) * tm) // dec_seq)
    row = pl.BlockSpec((tm, d), lambda i: (i, 0))
    vec = pl.BlockSpec((1, d), lambda i: (0, 0))
    cvec = pl.BlockSpec((None, 1, d), lambda i: (cidx(i), 0, 0))
    return pl.pallas_call(
        _post_mix_kernel,
        grid=(t // tm,),
        in_specs=[row, vec, vec, cvec, cvec,
                  pl.BlockSpec((d, ROUTER_LANES), lambda i: (0, 0)),
                  pl.BlockSpec((1, ROUTER_LANES), lambda i: (0, 0))],
        out_specs=[row, row, pl.BlockSpec((tm, ROUTER_LANES), lambda i: (i, 0))],
        out_shape=[jax.ShapeDtypeStruct((t, d), F32), jax.ShapeDtypeStruct((t, d), F32),
                   jax.ShapeDtypeStruct((t, ROUTER_LANES), F32)],
        compiler_params=_params("arbitrary"),
        name="ln_ln_router",
    )(v, ln_g.reshape(1, d), ln_b.reshape(1, d), sc, sh, w_router, b_router)


def _row_copy(src_hbm, dst_vmem, src_row, dst_row, sem):
    return pltpu.make_async_copy(src_hbm.at[pl.ds(src_row, 1), :], dst_vmem.at[pl.ds(dst_row, 1), :], sem)


def _gather_kernel(rowtok_ref, nact_ref, h_hbm, o_ref, buf_ref, sem):
    b = pl.program_id(0)
    tm = buf_ref.shape[0]

    @pl.when(b < nact_ref[0])
    def _():
        def issue(r, carry):
            _row_copy(h_hbm, buf_ref, rowtok_ref[b * tm + r], r, sem).start()
            return carry

        def drain(r, carry):
            _row_copy(h_hbm, buf_ref, 0, r, sem).wait()
            return carry

        lax.fori_loop(0, tm, issue, 0)
        lax.fori_loop(0, tm, drain, 0)
        o_ref[...] = buf_ref[...].astype(o_ref.dtype)

    @pl.when(b >= nact_ref[0])
    def _():
        o_ref[...] = jnp.zeros_like(o_ref)


def _gather_rows(h2, row_tok, n_act, n_blocks):
    t, d = h2.shape
    tm = TM_MOE
    return pl.pallas_call(
        _gather_kernel,
        grid_spec=pltpu.PrefetchScalarGridSpec(
            num_scalar_prefetch=2,
            grid=(n_blocks,),
            in_specs=[pl.BlockSpec(memory_space=pl.ANY)],
            out_specs=pl.BlockSpec((tm, d), lambda b, rt, na: (b, 0)),
            scratch_shapes=[pltpu.VMEM((tm, d), F32), pltpu.SemaphoreType.DMA(())]),
        out_shape=jax.ShapeDtypeStruct((n_blocks * tm, d), BF16),
        compiler_params=_params("arbitrary"),
        name="moe_gather",
    )(row_tok, n_act, h2)


def _expert_up_kernel(se_ref, sj_ref, sb_ref, oj_ref, ob_ref, ns_ref, x_ref, wg_ref, wu_ref, o_ref):
    live = pl.program_id(0) < ns_ref[0]

    @pl.when(live)
    def _():
        x = x_ref[...]
        a = jnp.dot(x, wg_ref[...].astype(BF16), preferred_element_type=F32)
        u = jnp.dot(x, wu_ref[...].astype(BF16), preferred_element_type=F32)
        o_ref[...] = ((a * _sigmoid(a)) * u).astype(o_ref.dtype)

    @pl.when(jnp.logical_not(live))
    def _():
        o_ref[...] = jnp.zeros_like(o_ref)


def _expert_up(xs, w_gate, w_up, sched, n_steps):
    r, d = xs.shape
    de = w_gate.shape[2]
    tm, tf = TM_MOE, TF_MOE
    w_spec = pl.BlockSpec((None, d, tf), lambda s, se, sj, sb, oj, ob, ns: (se[s], 0, sj[s]))
    return pl.pallas_call(
        _expert_up_kernel,
        grid_spec=pltpu.PrefetchScalarGridSpec(
            num_scalar_prefetch=6,
            grid=(n_steps,),
            in_specs=[pl.BlockSpec((tm, d), lambda s, se, sj, sb, oj, ob, ns: (sb[s], 0)), w_spec, w_spec],
            out_specs=pl.BlockSpec((tm, tf), lambda s, se, sj, sb, oj, ob, ns: (ob[s], oj[s]))),
        out_shape=jax.ShapeDtypeStruct((r, de), BF16),
        compiler_params=_params("arbitrary"),
        name="moe_gate_up",
    )(*sched, xs, w_gate, w_up)


def _expert_down_kernel(se_ref, sj_ref, sb_ref, oj_ref, ob_ref, ns_ref, h_ref, wd_ref, o_ref):
    live = pl.program_id(0) < ns_ref[0]

    @pl.when(live)
    def _():
        o_ref[...] = jnp.dot(h_ref[...], wd_ref[...].astype(BF16), preferred_element_type=F32)

    @pl.when(jnp.logical_not(live))
    def _():
        o_ref[...] = jnp.zeros_like(o_ref)


def _expert_down(hid, w_down, sched, n_steps):
    r, de = hid.shape
    d = w_down.shape[2]
    tm, tn = TM_MOE, TN_MOE
    return pl.pallas_call(
        _expert_down_kernel,
        grid_spec=pltpu.PrefetchScalarGridSpec(
            num_scalar_prefetch=6,
            grid=(n_steps,),
            in_specs=[pl.BlockSpec((tm, de), lambda s, se, sj, sb, oj, ob, ns: (sb[s], 0)),
                      pl.BlockSpec((None, de, tn), lambda s, se, sj, sb, oj, ob, ns: (se[s], 0, sj[s]))],
            out_specs=pl.BlockSpec((tm, tn), lambda s, se, sj, sb, oj, ob, ns: (ob[s], oj[s]))),
        out_shape=jax.ShapeDtypeStruct((r, d), F32),
        compiler_params=_params("arbitrary"),
        name="moe_down",
    )(*sched, hid, w_down)


def _combine_kernel(npb, alpha, pos_ref, y_hbm, x1_ref, rt_ref, gt_ref, g_ref, b_ref, op_ref, os_ref, buf_ref, sem):
    i = pl.program_id(0)
    tm = x1_ref.shape[0]

    def issue(r, carry):
        for k in range(TOP_K):
            _row_copy(y_hbm, buf_ref.at[k], pos_ref[(i * tm + r) * TOP_K + k], r, sem).start()
        return carry

    def drain(r, carry):
        for k in range(TOP_K):
            _row_copy(y_hbm, buf_ref.at[k], 0, r, sem).wait()
        return carry

    lax.fori_loop(0, tm, issue, 0)
    lax.fori_loop(0, tm, drain, 0)
    rt = rt_ref[...]
    f = rt[:, 2:3] * buf_ref[0] + rt[:, 3:4] * buf_ref[1]
    out = _ln_rows(alpha * x1_ref[...] + gt_ref[...] * f) * g_ref[...] + b_ref[...]

    @pl.when(i < npb)
    def _():
        op_ref[...] = out

    @pl.when(i >= npb)
    def _():
        os_ref[...] = out


def _combine(yb, pos, x1, route, gt, ln_g, ln_b, alpha, tp, dec_seq):
    t, d = x1.shape
    tm = TM_ROW
    npb = tp // tm
    cidx = lambda i: jnp.where(i < npb, 0, 1 + (jnp.maximum(i - npb, 0) * tm) // dec_seq)
    vec = pl.BlockSpec((1, d), lambda i, p: (0, 0))
    return pl.pallas_call(
        functools.partial(_combine_kernel, npb, alpha),
        grid_spec=pltpu.PrefetchScalarGridSpec(
            num_scalar_prefetch=1,
            grid=(t // tm,),
            in_specs=[pl.BlockSpec(memory_space=pl.ANY),
                      pl.BlockSpec((tm, d), lambda i, p: (i, 0)),
                      pl.BlockSpec((tm, ROUTER_LANES), lambda i, p: (i, 0)),
                      pl.BlockSpec((None, 1, d), lambda i, p: (cidx(i), 0, 0)),
                      vec, vec],
            out_specs=[pl.BlockSpec((tm, d), lambda i, p: (jnp.minimum(i, npb - 1), 0)),
                       pl.BlockSpec((tm, d), lambda i, p: (jnp.maximum(i - npb, 0), 0))],
            scratch_shapes=[pltpu.VMEM((TOP_K, tm, d), F32), pltpu.SemaphoreType.DMA(())]),
        out_shape=[jax.ShapeDtypeStruct((tp, d), F32), jax.ShapeDtypeStruct((t - tp, d), F32)],
        compiler_params=_params("arbitrary"),
        name="moe_combine_ln",
    )(pos, yb, x1, route, gt, ln_g.reshape(1, d), ln_b.reshape(1, d))


def _dispatch_plan(eid, n_tok):
    tm = TM_MOE
    n_assign = n_tok * TOP_K
    n_blocks = n_assign // tm + N_EXPERTS
    flat = eid.reshape(-1)
    onehot = (flat[:, None] == jnp.arange(N_EXPERTS, dtype=jnp.int32)[None, :]).astype(jnp.int32)
    csum = jnp.cumsum(onehot, axis=0)
    rank = jnp.take_along_axis(csum, flat[:, None], axis=1)[:, 0] - 1
    counts = csum[-1]
    nb = (counts + tm - 1) // tm
    nb_end = jnp.cumsum(nb)
    bs = nb_end - nb
    n_act = nb_end[-1]
    pos = (bs[flat] * tm + rank).astype(jnp.int32)
    tok = jnp.arange(n_assign, dtype=jnp.int32) // TOP_K
    row_tok = jnp.zeros((n_blocks * tm,), jnp.int32).at[pos].set(tok)

    def schedule(n_inner):
        n_steps = n_inner * n_blocks
        n_live = n_inner * n_act
        step = jnp.arange(n_steps, dtype=jnp.int32)
        s = jnp.minimum(step, n_live - 1)
        e = jnp.minimum(jnp.searchsorted(n_inner * nb_end, s, side="right"), N_EXPERTS - 1).astype(jnp.int32)
        loc = s - n_inner * bs[e]
        nbe = jnp.maximum(nb[e], 1)
        sj = (loc // nbe).astype(jnp.int32)
        sb = (bs[e] + loc % nbe).astype(jnp.int32)
        spare = jnp.maximum(step - n_live, 0)
        oj = jnp.where(step < n_live, sj, spare % n_inner).astype(jnp.int32)
        ob = jnp.where(step < n_live, sb, n_act + spare // n_inner).astype(jnp.int32)
        return (e, sj, sb, oj, ob, n_live.reshape(1).astype(jnp.int32)), n_steps

    return pos, row_tok, n_act.reshape(1).astype(jnp.int32), n_blocks, schedule


def kernel(x_prompt, x_sample, state_ret_fwd, state_ret_bwd, c, c_ctx, w_mod, b_mod, w_in, conv_w, conv_b, conv_ln_g, conv_ln_b, ret_decay_fwd, ret_decay_bwd, ret_gn_g, w_out, ln1_g, ln1_b, w_grp, b_grp, w_exp, b_exp, w_gate, w_up, w_down, ln2_g, ln2_b):
    depth = w_mod.shape[0]
    assert depth == 1, "single-layer step"
    bp, sp, d = x_prompt.shape
    bs_, ss, _ = x_sample.shape
    tp, ts = bp * sp, bs_ * ss
    t = tp + ts
    conv_width = conv_w.shape[2]
    ret_width = ret_gn_g.shape[1]
    assert ret_width == RET_HEADS * RET_D and sp % CHUNK == 0 and ss % CHUNK == 0
    alpha = (2.0 * depth) ** 0.25

    xp = x_prompt.reshape(tp, d)
    xs = x_sample.reshape(ts, d)
    cond8 = jnp.zeros((8, d), F32).at[0].set(c_ctx).at[1:1 + bs_].set(c)
    m = _modulation(cond8, w_mod[0], b_mod[0])
    sh1, sc1, gt1, sh2, sc2, gt2 = [m[:, k * d:(k + 1) * d].reshape(8, 1, d) for k in range(6)]

    h = _ln_modulate(xp, xs, sc1, sh1, ss)
    u_glu = _glu_proj(h, w_in[0], conv_width)
    cos_t, sin_t = _rope_tables(ss, RET_D)
    qkvg = _qkvg_proj(h, w_in[0], cos_t, sin_t, 2 * conv_width, 4 * ret_width, tp)

    u = _conv_module(u_glu, conv_w[0], conv_b[0], conv_ln_g[0], conv_ln_b[0], tp, ss)

    lg = jnp.stack([jax.nn.log_sigmoid(ret_decay_fwd[0].astype(F32)),
                    jax.nn.log_sigmoid(ret_decay_bwd[0].astype(F32))])
    gn = ret_gn_g[0].reshape(1, ret_width)
    r_p, new_f, new_b = _retention(qkvg, lg, gn, sp, 0, bp)
    r_s = _retention(qkvg, lg, gn, ss, tp // ss, bs_, state_ret_fwd, state_ret_bwd)
    r = jnp.concatenate([r_p, r_s], axis=0)

    v = _out_proj(u, r, w_out[0], xp, xs, gt1, alpha, ss)

    w_router = jnp.zeros((d, ROUTER_LANES), F32).at[:, :N_GROUPS].set(w_grp[0]).at[:, N_GROUPS:N_GROUPS + N_EXPERTS].set(w_exp[0]).astype(BF16)
    b_router = jnp.zeros((1, ROUTER_LANES), F32).at[0, :N_GROUPS].set(b_grp[0]).at[0, N_GROUPS:N_GROUPS + N_EXPERTS].set(b_exp[0])
    x1, h2, route = _post_mix(v, ln1_g[0], ln1_b[0], sc2, sh2, w_router, b_router, tp, ss)

    eid = route[:, :TOP_K].astype(jnp.int32)
    pos, row_tok, n_act, n_blocks, schedule = _dispatch_plan(eid, t)
    xg = _gather_rows(h2, row_tok, n_act, n_blocks)
    sched_up, n_up = schedule(w_gate.shape[3] // TF_MOE)
    hid = _expert_up(xg, w_gate[0], w_up[0], sched_up, n_up)
    sched_dn, n_dn = schedule(d // TN_MOE)
    yb = _expert_down(hid, w_down[0], sched_dn, n_dn)

    out_p, out_s = _combine(yb, pos, x1, route, gt2, ln2_g[0], ln2_b[0], alpha, tp, ss)
    return (out_p.reshape(bp, sp, d), out_s.reshape(bs_, ss, d), new_f, new_b)
```

```python
import functools

import numpy as np
import jax
import jax.numpy as jnp
from jax import lax
from jax.experimental import pallas as pl
from jax.experimental.pallas import tpu as pltpu

F32 = jnp.float32
BF16 = jnp.bfloat16

LN_EPS = 1e-5
CONV_K = 31
CONV_HALO = 16
RET_HEADS = 8
RET_D = 256
CHUNK = 128
GRID_W = 64
ROPE_BASE = 10000.0
N_GROUPS = 4
EXPERTS_PER_GROUP = 8
N_EXPERTS = N_GROUPS * EXPERTS_PER_GROUP
TOP_K = 2
ROUTER_LANES = 128
VMEM_LIMIT = 56 * 1024 * 1024

TM_PROJ = 1024
TM_ROW = 256
TM_MOE = 256
TF_MOE = 512


def _params(*sem):
    return pltpu.CompilerParams(dimension_semantics=tuple(sem), vmem_limit_bytes=VMEM_LIMIT)


def _sigmoid(x):
    return 1.0 / (1.0 + jnp.exp(-x))


def _ln_rows(x):
    mu = jnp.mean(x, axis=-1, keepdims=True)
    xc = x - mu
    var = jnp.mean(xc * xc, axis=-1, keepdims=True)
    return xc * lax.rsqrt(var + LN_EPS)


def _mod_kernel(c_ref, w_ref, b_ref, o_ref):
    c = c_ref[...]
    s = c * _sigmoid(c)
    o_ref[...] = jnp.dot(s.astype(BF16), w_ref[...].astype(BF16), preferred_element_type=F32) + b_ref[...]


def _modulation(cond8, w_mod, b_mod):
    d, n = w_mod.shape
    tn = 512
    return pl.pallas_call(
        _mod_kernel,
        grid=(n // tn,),
        in_specs=[pl.BlockSpec((8, d), lambda j: (0, 0)),
                  pl.BlockSpec((d, tn), lambda j: (0, j)),
                  pl.BlockSpec((1, tn), lambda j: (0, j))],
        out_specs=pl.BlockSpec((8, tn), lambda j: (0, j)),
        out_shape=jax.ShapeDtypeStruct((8, n), F32),
        compiler_params=_params("arbitrary"),
        name="modulation",
    )(cond8, w_mod, b_mod.reshape(1, n))


def _ln_mod_kernel(npb, xp_ref, xs_ref, sc_ref, sh_ref, o_ref):
    def body(x_ref):
        y = _ln_rows(x_ref[...])
        o_ref[...] = (y * (1.0 + sc_ref[...]) + sh_ref[...]).astype(o_ref.dtype)

    i = pl.program_id(0)
    pl.when(i < npb)(lambda: body(xp_ref))
    pl.when(i >= npb)(lambda: body(xs_ref))


def _ln_modulate(xp, xs, sc, sh, dec_seq):
    (tp, d), ts = xp.shape, xs.shape[0]
    tm = 512
    npb, nsb = tp // tm, ts // tm
    cidx = lambda i: jnp.where(i < npb, 0, 1 + (jnp.maximum(i - npb, 0) * tm) // dec_seq)
    return pl.pallas_call(
        functools.partial(_ln_mod_kernel, npb),
        grid=(npb + nsb,),
        in_specs=[pl.BlockSpec((tm, d), lambda i: (jnp.minimum(i, npb - 1), 0)),
                  pl.BlockSpec((tm, d), lambda i: (jnp.maximum(i - npb, 0), 0)),
                  pl.BlockSpec((None, 1, d), lambda i: (cidx(i), 0, 0)),
                  pl.BlockSpec((None, 1, d), lambda i: (cidx(i), 0, 0))],
        out_specs=pl.BlockSpec((tm, d), lambda i: (i, 0)),
        out_shape=jax.ShapeDtypeStruct((tp + ts, d), BF16),
        compiler_params=_params("arbitrary"),
        name="ln_modulate",
    )(xp, xs, sc, sh)


def _glu_proj_kernel(h_ref, wv_ref, wg_ref, o_ref):
    h = h_ref[...]
    a = jnp.dot(h, wv_ref[...].astype(BF16), preferred_element_type=F32)
    g = jnp.dot(h, wg_ref[...].astype(BF16), preferred_element_type=F32)
    o_ref[...] = a * _sigmoid(g)


def _glu_proj(h, w_in, conv_width):
    t, d = h.shape
    tm, tn = TM_PROJ, 256
    goff = conv_width // tn
    return pl.pallas_call(
        _glu_proj_kernel,
        grid=(t // tm, conv_width // tn),
        in_specs=[pl.BlockSpec((tm, d), lambda i, j: (i, 0)),
                  pl.BlockSpec((d, tn), lambda i, j: (0, j)),
                  pl.BlockSpec((d, tn), lambda i, j: (0, j + goff))],
        out_specs=pl.BlockSpec((tm, tn), lambda i, j: (i, j)),
        out_shape=jax.ShapeDtypeStruct((t, conv_width), F32),
        compiler_params=_params("arbitrary", "arbitrary"),
        name="glu_proj",
    )(h, w_in, w_in)


def _qkvg_proj_kernel(npb, tiles_per_part, k_scale, h_ref, w_ref, cos_ref, sin_ref, o_ref):
    i, j = pl.program_id(0), pl.program_id(1)
    z = jnp.dot(h_ref[...], w_ref[...].astype(BF16), preferred_element_type=F32)
    tpp = tiles_per_part

    @pl.when(j >= 3 * tpp)
    def _():
        o_ref[...] = (z * _sigmoid(z)).astype(o_ref.dtype)

    @pl.when((j >= 2 * tpp) & (j < 3 * tpp))
    def _():
        o_ref[...] = z.astype(o_ref.dtype)

    @pl.when(j < 2 * tpp)
    def _():
        zz = z * jnp.where(j >= tpp, k_scale, 1.0).astype(F32)

        @pl.when(i < npb)
        def _():
            o_ref[...] = zz.astype(o_ref.dtype)

        @pl.when(i >= npb)
        def _():
            for s in range(zz.shape[1] // 128):
                cs = slice(s * 128, (s + 1) * 128)
                ts_ = slice((s * 128) % RET_D, (s * 128) % RET_D + 128)
                zs = zz[:, cs]
                o_ref[:, cs] = (zs * cos_ref[:, ts_] + pltpu.roll(zs, 64, axis=1) * sin_ref[:, ts_]).astype(o_ref.dtype)


def _qkvg_proj(h, w_in, cos_t, sin_t, col0, n_cols, tp):
    t, d = h.shape
    tm, tn = TM_PROJ, 512
    assert cos_t.shape == (tm, RET_D), "a projection row tile is one latent sequence"
    npb = tp // tm
    tpp = (n_cols // 4) // tn
    return pl.pallas_call(
        functools.partial(_qkvg_proj_kernel, npb, tpp, RET_D ** -0.5),
        grid=(t // tm, n_cols // tn),
        in_specs=[pl.BlockSpec((tm, d), lambda i, j: (i, 0)),
                  pl.BlockSpec((d, tn), lambda i, j: (0, j + col0 // tn)),
                  pl.BlockSpec((tm, RET_D), lambda i, j: (0, 0)),
                  pl.BlockSpec((tm, RET_D), lambda i, j: (0, 0))],
        out_specs=pl.BlockSpec((tm, tn), lambda i, j: (i, j)),
        out_shape=jax.ShapeDtypeStruct((t, n_cols), BF16),
        compiler_params=_params("arbitrary", "arbitrary"),
        name="qkvg_proj",
    )(h, w_in, cos_t, sin_t)


def _rope_tables(n_tok, width):
    rows = n_tok // GRID_W
    r_idx = np.repeat(np.arange(rows), GRID_W).astype(np.float64)
    c_idx = np.tile(np.arange(GRID_W), rows).astype(np.float64)
    dq = RET_D // 2
    inv = ROPE_BASE ** (-np.arange(dq // 2, dtype=np.float64) / (dq // 2))
    sign = np.concatenate([-np.ones(dq // 2), np.ones(dq // 2)])
    cos_h, sin_h = [], []
    for idx in (r_idx, c_idx):
        ang = idx[:, None] * inv
        cos_h.append(np.concatenate([np.cos(ang), np.cos(ang)], axis=1))
        sin_h.append(np.concatenate([np.sin(ang), np.sin(ang)], axis=1) * sign)
    cos_h, sin_h = np.concatenate(cos_h, axis=1), np.concatenate(sin_h, axis=1)
    reps = width // RET_D
    return (jnp.asarray(np.tile(cos_h, (1, reps)), F32), jnp.asarray(np.tile(sin_h, (1, reps)), F32))


def _conv_kernel(npb, tiles_per_seq, uc_ref, up_ref, un_ref, w_ref, b_ref, g_ref, bt_ref, o_ref, buf_ref, y_ref):
    r = pl.program_id(0)
    tm = uc_ref.shape[0]
    ncg = buf_ref.shape[0]
    t = lax.rem(jnp.maximum(r - npb, 0), tiles_per_seq)
    is_s = r >= npb
    has_prev = is_s & (t != 0)
    has_next = is_s & (t != tiles_per_seq - 1)
    for cg in range(ncg):
        cs = slice(cg * 128, (cg + 1) * 128)
        buf_ref[cg, 0:CONV_HALO, :] = jnp.where(has_prev, up_ref[:, cs], 0.0)
        buf_ref[cg, CONV_HALO:CONV_HALO + tm, :] = uc_ref[:, cs]
        buf_ref[cg, CONV_HALO + tm:, :] = jnp.where(has_next, un_ref[:, cs], 0.0)

    off = CONV_HALO - CONV_K // 2

    def body(cg, carry):
        acc = jnp.zeros((tm, 128), F32)
        for tap in range(CONV_K):
            acc = acc + buf_ref[cg, off + tap:off + tap + tm, :] * w_ref[cg, tap:tap + 1, :]
        y_ref[cg] = acc + b_ref[cg]
        return carry

    lax.fori_loop(0, ncg, body, 0)

    n_ch = ncg * 128
    tot = y_ref[0]
    for cg in range(1, ncg):
        tot = tot + y_ref[cg]
    mu = jnp.sum(tot, axis=1, keepdims=True) * (1.0 / n_ch)
    sq = jnp.zeros((tm, 128), F32)
    for cg in range(ncg):
        dv = y_ref[cg] - mu
        sq = sq + dv * dv
    var = jnp.sum(sq, axis=1, keepdims=True) * (1.0 / n_ch)
    rstd = lax.rsqrt(var + LN_EPS)
    for cg in range(ncg):
        cs = slice(cg * 128, (cg + 1) * 128)
        v = (y_ref[cg] - mu) * rstd * g_ref[cg] + bt_ref[cg]
        o_ref[:, cs] = (v * _sigmoid(v)).astype(o_ref.dtype)


def _conv_module(u, conv_w, conv_b, ln_g, ln_b, tp, dec_seq):
    t, c = u.shape
    tm = TM_ROW
    ncg = c // 128
    npb = tp // tm
    hb = tm // CONV_HALO
    n_halo_blocks = t // CONV_HALO
    w3 = jnp.zeros((32, c), F32).at[:CONV_K].set(conv_w).reshape(32, ncg, 128).transpose(1, 0, 2)
    vec = lambda a: a.reshape(ncg, 1, 128)
    return pl.pallas_call(
        functools.partial(_conv_kernel, npb, dec_seq // tm),
        grid=(t // tm,),
        in_specs=[pl.BlockSpec((tm, c), lambda r: (r, 0)),
                  pl.BlockSpec((CONV_HALO, c), lambda r: (jnp.maximum(r * hb - 1, 0), 0)),
                  pl.BlockSpec((CONV_HALO, c), lambda r: (jnp.minimum((r + 1) * hb, n_halo_blocks - 1), 0)),
                  pl.BlockSpec((ncg, 32, 128), lambda r: (0, 0, 0)),
                  pl.BlockSpec((ncg, 1, 128), lambda r: (0, 0, 0)),
                  pl.BlockSpec((ncg, 1, 128), lambda r: (0, 0, 0)),
                  pl.BlockSpec((ncg, 1, 128), lambda r: (0, 0, 0))],
        out_specs=pl.BlockSpec((tm, c), lambda r: (r, 0)),
        out_shape=jax.ShapeDtypeStruct((t, c), BF16),
        scratch_shapes=[pltpu.VMEM((ncg, tm + 2 * CONV_HALO, 128), F32),
                        pltpu.VMEM((ncg, tm, 128), F32)],
        compiler_params=_params("arbitrary"),
        name="conv_ln_swish",
    )(u, u, u, w3, vec(conv_b), vec(ln_g), vec(ln_b))


def _retention_kernel(nc, has_init, lg_ref, q_ref, k_ref, v_ref, g_ref, gn_ref, *rest):
    if has_init:
        s0f_ref, s0b_ref, r_ref, o_ref, sf_ref, sb_ref = rest
    else:
        r_ref, sf_ref, sb_ref, o_ref = rest
    hd = pl.program_id(1)
    lgf, lgb = lg_ref[0, hd], lg_ref[1, hd]
    c = CHUNK
    row = lax.broadcasted_iota(jnp.int32, (c, c), 0)
    col = lax.broadcasted_iota(jnp.int32, (c, c), 1)
    diff = (row - col).astype(F32)
    dec = (jnp.where(diff >= 0, jnp.exp(jnp.maximum(diff, 0.0) * lgf), 0.0)
           + jnp.where(diff <= 0, jnp.exp(jnp.maximum(-diff, 0.0) * lgb), 0.0))
    pos = lax.broadcasted_iota(jnp.int32, (c, 1), 0).astype(F32)
    xi_f = jnp.exp((pos + 1.0) * lgf)
    zeta_f = jnp.exp((c - 1.0 - pos) * lgf)
    xi_b = jnp.exp((c - pos) * lgb)
    zeta_b = jnp.exp(pos * lgb)
    gch_f = jnp.exp(jnp.full((1, RET_D), c, F32) * lgf)
    gch_b = jnp.exp(jnp.full((1, RET_D), c, F32) * lgb)

    if has_init:
        sf_ref[...] = s0f_ref[...]
        sb_ref[...] = s0b_ref[...]
    else:
        sf_ref[...] = jnp.zeros_like(sf_ref)
        sb_ref[...] = jnp.zeros_like(sb_ref)

    def chunk(ci):
        sl = slice(ci * c, (ci + 1) * c)
        return q_ref[sl, :], k_ref[sl, :], v_ref[sl, :], sl

    def state_update(s_ref, kc, vc, zeta, gch):
        kz = (kc.astype(F32) * zeta).T.astype(BF16)
        s_ref[...] = gch * s_ref[...] + jnp.dot(kz, vc, preferred_element_type=F32)

    for ci in range(nc):
        qc, kc, vc, sl = chunk(ci)
        s = lax.dot_general(qc, kc, (((1,), (1,)), ((), ())), preferred_element_type=F32)
        p = (s * dec).astype(BF16)
        o = jnp.dot(p, vc, preferred_element_type=F32)
        o = o + jnp.dot(qc, sf_ref[...].astype(BF16), preferred_element_type=F32) * xi_f
        o_ref[sl, :] = o
        state_update(sf_ref, kc, vc, zeta_f, gch_f)

    for ci in reversed(range(nc)):
        qc, kc, vc, sl = chunk(ci)
        o_ref[sl, :] = o_ref[sl, :] + jnp.dot(qc, sb_ref[...].astype(BF16), preferred_element_type=F32) * xi_b
        state_update(sb_ref, kc, vc, zeta_b, gch_b)

    y = _ln_rows(o_ref[...])
    r_ref[...] = (g_ref[...].astype(F32) * (y * gn_ref[...])).astype(r_ref.dtype)


def _retention(qkvg, lg, gn, n, row_blk0, n_seq, s0f=None, s0b=None):
    has_init = s0f is not None
    nh, dd = RET_HEADS, RET_D
    col = lambda part: (lambda b, h: (b + row_blk0, part * nh + h))
    st_spec = pl.BlockSpec((None, None, None, dd, dd), lambda b, h: (b, 0, h, 0, 0))
    in_specs = [pl.BlockSpec(memory_space=pltpu.SMEM),
                pl.BlockSpec((n, dd), col(0)), pl.BlockSpec((n, dd), col(1)),
                pl.BlockSpec((n, dd), col(2)), pl.BlockSpec((n, dd), col(3)),
                pl.BlockSpec((1, dd), lambda b, h: (0, h))]
    args = [lg, qkvg, qkvg, qkvg, qkvg, gn]
    r_shape = jax.ShapeDtypeStruct((n_seq * n, nh * dd), BF16)
    r_spec = pl.BlockSpec((n, dd), lambda b, h: (b, h))
    st_scratch = pltpu.VMEM((dd, dd), F32)
    if has_init:
        in_specs += [st_spec, st_spec]
        args += [s0f, s0b]
        out_specs, out_shape = r_spec, r_shape
        scratch = [pltpu.VMEM((n, dd), F32), st_scratch, st_scratch]
    else:
        st_shape = jax.ShapeDtypeStruct((n_seq, 1, nh, dd, dd), F32)
        out_specs, out_shape = [r_spec, st_spec, st_spec], [r_shape, st_shape, st_shape]
        scratch = [pltpu.VMEM((n, dd), F32)]
    return pl.pallas_call(
        functools.partial(_retention_kernel, n // CHUNK, has_init),
        grid=(n_seq, nh),
        in_specs=in_specs, out_specs=out_specs, out_shape=out_shape,
        scratch_shapes=scratch,
        compiler_params=_params("arbitrary", "arbitrary"),
        name="retention_latent" if has_init else "retention_context",
    )(*args)


def _out_proj_kernel(npb, alpha, u_ref, r_ref, w1_ref, w2_ref, xp_ref, xs_ref, gt_ref, o_ref):
    y = (jnp.dot(u_ref[...], w1_ref[...].astype(BF16), preferred_element_type=F32)
         + jnp.dot(r_ref[...], w2_ref[...].astype(BF16), preferred_element_type=F32))
    i = pl.program_id(0)

    @pl.when(i < npb)
    def _():
        o_ref[...] = alpha * xp_ref[...] + gt_ref[...] * y

    @pl.when(i >= npb)
    def _():
        o_ref[...] = alpha * xs_ref[...] + gt_ref[...] * y


def _out_proj(u, r, w_out, xp, xs, gt, alpha, dec_seq):
    t, kh = u.shape
    d = w_out.shape[1]
    tm, tn = TM_PROJ, 512
    npb = xp.shape[0] // tm
    cidx = lambda i: jnp.where(i < npb, 0, 1 + (jnp.maximum(i - npb, 0) * tm) // dec_seq)
    return pl.pallas_call(
        functools.partial(_out_proj_kernel, npb, alpha),
        grid=(t // tm, d // tn),
        in_specs=[pl.BlockSpec((tm, kh), lambda i, j: (i, 0)),
                  pl.BlockSpec((tm, kh), lambda i, j: (i, 0)),
                  pl.BlockSpec((kh, tn), lambda i, j: (0, j)),
                  pl.BlockSpec((kh, tn), lambda i, j: (1, j)),
                  pl.BlockSpec((tm, tn), lambda i, j: (jnp.minimum(i, npb - 1), j)),
                  pl.BlockSpec((tm, tn), lambda i, j: (jnp.maximum(i - npb, 0), j)),
                  pl.BlockSpec((None, 1, tn), lambda i, j: (cidx(i), 0, j))],
        out_specs=pl.BlockSpec((tm, tn), lambda i, j: (i, j)),
        out_shape=jax.ShapeDtypeStruct((t, d), F32),
        compiler_params=_params("arbitrary", "arbitrary"),
        name="out_proj_residual",
    )(u, r, w_out, w_out, xp, xs, gt)


def _pack_bf16_pairs(x):
    n = x.shape[1] // 2
    lo = lax.bitcast_convert_type(x[:, :n].astype(BF16).astype(F32), jnp.uint32)
    hi = lax.bitcast_convert_type(x[:, n:].astype(BF16).astype(F32), jnp.uint32)
    return hi | (lo >> 16)


def _unpack_bf16_pairs(p):
    lo = lax.bitcast_convert_type(p << 16, F32)
    hi = lax.bitcast_convert_type(p & jnp.uint32(0xFFFF0000), F32)
    return jnp.concatenate([lo, hi], axis=1)


def _post_mix_kernel(v_ref, g1_ref, b1_ref, sc_ref, sh_ref, wr_ref, br_ref, x1_ref, h2_ref, rt_ref, cnt_ref):
    @pl.when(pl.program_id(0) == 0)
    def _():
        cnt_ref[...] = jnp.zeros_like(cnt_ref)

    x1 = _ln_rows(v_ref[...]) * g1_ref[...] + b1_ref[...]
    x1_ref[...] = x1
    h2 = _ln_rows(x1) * (1.0 + sc_ref[...]) + sh_ref[...]
    h2_ref[...] = _pack_bf16_pairs(h2)
    logits = jnp.dot(h2.astype(BF16), wr_ref[...], preferred_element_type=F32) + br_ref[...]
    lane = lax.broadcasted_iota(jnp.int32, logits.shape, 1)
    big = jnp.int32(ROUTER_LANES)
    neg = jnp.float32(-jnp.inf)

    def first_lane_of_max(vals):
        m = jnp.max(vals, axis=1, keepdims=True)
        return m, jnp.min(jnp.where(vals == m, lane, big), axis=1, keepdims=True)

    lgt = jnp.where(lane < N_GROUPS, logits, neg)
    eg = jnp.exp(lgt - jnp.max(lgt, axis=1, keepdims=True))
    pg = eg / jnp.sum(eg, axis=1, keepdims=True)
    grp_prob, grp = first_lane_of_max(jnp.where(lane < N_GROUPS, pg, -1.0))
    lo = N_GROUPS + grp * EXPERTS_PER_GROUP
    in_grp = (lane >= lo) & (lane < lo + EXPERTS_PER_GROUP)
    let = jnp.where(in_grp, logits, neg)
    ee = jnp.exp(let - jnp.max(let, axis=1, keepdims=True))
    pe = jnp.where(in_grp, ee / jnp.sum(ee, axis=1, keepdims=True), -1.0)
    p1, l1 = first_lane_of_max(pe)
    p2, l2 = first_lane_of_max(jnp.where(lane == l1, -1.0, pe))
    den = p1 + p2
    gate1, gate2 = grp_prob * p1 / den, grp_prob * p2 / den
    e1, e2 = l1 - N_GROUPS, l2 - N_GROUPS
    tm = logits.shape[0]
    hot1 = (lane == e1).astype(F32)
    hot2 = (lane == e2).astype(F32)
    earlier = (lax.broadcasted_iota(jnp.int32, (tm, tm), 0) > lax.broadcasted_iota(jnp.int32, (tm, tm), 1)).astype(BF16)
    before1 = jnp.dot(earlier, hot1.astype(BF16), preferred_element_type=F32) + cnt_ref[...]
    n1 = jnp.sum(hot1, axis=0, keepdims=True)
    before2 = jnp.dot(earlier, hot2.astype(BF16), preferred_element_type=F32) + cnt_ref[...] + n1
    slot1 = jnp.sum(before1 * hot1, axis=1, keepdims=True)
    slot2 = jnp.sum(before2 * hot2, axis=1, keepdims=True)
    cnt_ref[...] = cnt_ref[...] + n1 + jnp.sum(hot2, axis=0, keepdims=True)
    vals = (e1.astype(F32), e2.astype(F32), gate1, gate2, slot1, slot2)
    out = jnp.zeros_like(logits)
    for k, val in enumerate(vals):
        out = jnp.where(lane == k, val, out)
    rt_ref[...] = out


def _post_mix(v, ln_g, ln_b, sc, sh, w_router, b_router, tp, dec_seq):
    t, d = v.shape
    tm = TM_ROW
    npb = tp // tm
    cidx = lambda i: jnp.where(i < npb, 0, 1 + (jnp.maximum(i - npb, 0) * tm) // dec_seq)
    row = pl.BlockSpec((tm, d), lambda i: (i, 0))
    vec = pl.BlockSpec((1, d), lambda i: (0, 0))
    cvec = pl.BlockSpec((None, 1, d), lambda i: (cidx(i), 0, 0))
    return pl.pallas_call(
        _post_mix_kernel,
        grid=(t // tm,),
        in_specs=[row, vec, vec, cvec, cvec,
                  pl.BlockSpec((d, ROUTER_LANES), lambda i: (0, 0)),
                  pl.BlockSpec((1, ROUTER_LANES), lambda i: (0, 0))],
        out_specs=[row, pl.BlockSpec((tm, d // 2), lambda i: (i, 0)),
                   pl.BlockSpec((tm, ROUTER_LANES), lambda i: (i, 0)),
                   pl.BlockSpec((1, ROUTER_LANES), lambda i: (0, 0))],
        out_shape=[jax.ShapeDtypeStruct((t, d), F32), jax.ShapeDtypeStruct((t, d // 2), jnp.uint32),
                   jax.ShapeDtypeStruct((t, ROUTER_LANES), F32), jax.ShapeDtypeStruct((1, ROUTER_LANES), F32)],
        compiler_params=_params("arbitrary"),
        name="ln_ln_router",
    )(v, ln_g.reshape(1, d), ln_b.reshape(1, d), sc, sh, w_router, b_router)


def _row_copy(src, dst, src_row, dst_row, sem):
    return pltpu.make_async_copy(src.at[pl.ds(src_row, 1), :], dst.at[pl.ds(dst_row, 1), :], sem)


def _on_parity(blk, fn):
    for par in range(2):
        pl.when(lax.rem(blk, 2) == par)(functools.partial(fn, blk, par))


def _dispatch_kernel(pos_ref, nact_ref, h_hbm, o_ref, tok_ref, buf_ref, sems):
    b = pl.program_id(0)
    n_act = nact_ref[0]
    tm = o_ref.shape[0]
    n_tok = pos_ref.shape[0] // TOP_K

    @pl.when(b == 0)
    def _():
        def clear(r, carry):
            tok_ref[r] = 0
            return carry

        def fill(tok, carry):
            for k in range(TOP_K):
                tok_ref[pos_ref[tok * TOP_K + k]] = tok
            return carry

        lax.fori_loop(0, tok_ref.shape[0], clear, 0, unroll=8)
        lax.fori_loop(0, n_tok, fill, 0, unroll=4)

    def rows(wait, blk, par):
        def body(r, carry):
            cp = _row_copy(h_hbm, buf_ref.at[par], tok_ref[blk * tm + r], r, sems.at[par])
            cp.wait() if wait else cp.start()
            return carry
        lax.fori_loop(0, tm, body, 0, unroll=8)

    @pl.when(b == 0)
    def _():
        rows(False, 0, 0)

    @pl.when(b + 1 < n_act)
    def _():
        _on_parity(b + 1, functools.partial(rows, False))

    @pl.when(b < n_act)
    def _():
        _on_parity(b, functools.partial(rows, True))
        o_ref[...] = buf_ref[lax.rem(b, 2)]

    @pl.when(b >= n_act)
    def _():
        o_ref[...] = jnp.zeros_like(o_ref)


def _dispatch_rows(h2p, pos, n_act, n_blocks):
    tm = TM_MOE
    dp = h2p.shape[1]
    return pl.pallas_call(
        _dispatch_kernel,
        grid_spec=pltpu.PrefetchScalarGridSpec(
            num_scalar_prefetch=2,
            grid=(n_blocks,),
            in_specs=[pl.BlockSpec(memory_space=pl.ANY)],
            out_specs=pl.BlockSpec((tm, dp), lambda b, *_: (b, 0)),
            scratch_shapes=[pltpu.SMEM((n_blocks * tm,), jnp.int32),
                            pltpu.VMEM((2, tm, dp), jnp.uint32), pltpu.SemaphoreType.DMA((2,))]),
        out_shape=jax.ShapeDtypeStruct((n_blocks * tm, dp), jnp.uint32),
        compiler_params=_params("arbitrary"),
        name="moe_dispatch",
    )(pos, n_act, h2p)


def _expert_up_kernel(se_ref, sj_ref, sb_ref, oj_ref, ob_ref, fst_ref, nxt_ref, ne_ref, nj_ref, ns_ref,
                      x_ref, wg_hbm, wu_hbm, o_ref, wg_st, wu_st, wg_bf, wu_bf, sems):
    live = pl.program_id(0) < ns_ref[0]
    _stream_group_weights(se_ref, sj_ref, fst_ref, nxt_ref, ne_ref, nj_ref, live,
                          ((wg_hbm, wg_st, wg_bf), (wu_hbm, wu_st, wu_bf)), sems)

    @pl.when(live)
    def _():
        x = _unpack_bf16_pairs(x_ref[...]).astype(BF16)
        a = jnp.dot(x, wg_bf[...], preferred_element_type=F32)
        u = jnp.dot(x, wu_bf[...], preferred_element_type=F32)
        o_ref[...] = ((a * _sigmoid(a)) * u).astype(o_ref.dtype)

    @pl.when(jnp.logical_not(live))
    def _():
        o_ref[...] = jnp.zeros_like(o_ref)


def _stream_group_weights(se_ref, sj_ref, fst_ref, nxt_ref, ne_ref, nj_ref, live, weights, sems):
    s = pl.program_id(0)

    def copies(e, j):
        out = []
        for k, (w_hbm, w_st, _) in enumerate(weights):
            tn = w_st.shape[1]
            out.append(pltpu.make_async_copy(w_hbm.at[e, :, pl.ds(pl.multiple_of(j * tn, tn), tn)], w_st, sems.at[k]))
        return out

    @pl.when(live & (fst_ref[s] == 1))
    def _():
        cur = copies(se_ref[s], sj_ref[s])

        @pl.when(s == 0)
        def _():
            for cp in cur:
                cp.start()

        for cp, (_, w_st, w_bf) in zip(cur, weights):
            cp.wait()
            w_bf[...] = w_st[...].astype(BF16)

        @pl.when(nxt_ref[s] == 1)
        def _():
            for cp in copies(ne_ref[s], nj_ref[s]):
                cp.start()


_N_SCHED = 10


def _sched_map(fn):
    return lambda s, *refs: fn(s, *refs[:_N_SCHED])


def _expert_up(xs, w_gate, w_up, sched, n_steps):
    r, dp = xs.shape
    d, de = w_gate.shape[1], w_gate.shape[2]
    tm, tf = TM_MOE, TF_MOE
    return pl.pallas_call(
        _expert_up_kernel,
        grid_spec=pltpu.PrefetchScalarGridSpec(
            num_scalar_prefetch=_N_SCHED,
            grid=(n_steps,),
            in_specs=[pl.BlockSpec((tm, dp), _sched_map(lambda s, se, sj, sb, *_: (sb[s], 0))),
                      pl.BlockSpec(memory_space=pl.ANY), pl.BlockSpec(memory_space=pl.ANY)],
            out_specs=pl.BlockSpec((tm, tf), _sched_map(lambda s, se, sj, sb, oj, ob, *_: (ob[s], oj[s]))),
            scratch_shapes=[pltpu.VMEM((d, tf), F32), pltpu.VMEM((d, tf), F32),
                            pltpu.VMEM((d, tf), BF16), pltpu.VMEM((d, tf), BF16),
                            pltpu.SemaphoreType.DMA((2,))]),
        out_shape=jax.ShapeDtypeStruct((r, de), BF16),
        compiler_params=_params("arbitrary"),
        name="moe_gate_up",
    )(*sched, xs, w_gate, w_up)


def _expert_down_kernel(se_ref, sj_ref, sb_ref, oj_ref, ob_ref, fst_ref, nxt_ref, ne_ref, nj_ref, ns_ref,
                        h_ref, wd_hbm, o_ref, wd_st, wd_bf, sems):
    live = pl.program_id(0) < ns_ref[0]
    _stream_group_weights(se_ref, sj_ref, fst_ref, nxt_ref, ne_ref, nj_ref, live, ((wd_hbm, wd_st, wd_bf),), sems)

    @pl.when(live)
    def _():
        o_ref[...] = _pack_bf16_pairs(jnp.dot(h_ref[...], wd_bf[...], preferred_element_type=F32))

    @pl.when(jnp.logical_not(live))
    def _():
        o_ref[...] = jnp.zeros_like(o_ref)


def _expert_down(hid, w_down, sched, n_steps):
    r, de = hid.shape
    d = w_down.shape[2]
    tm = TM_MOE
    return pl.pallas_call(
        _expert_down_kernel,
        grid_spec=pltpu.PrefetchScalarGridSpec(
            num_scalar_prefetch=_N_SCHED,
            grid=(n_steps,),
            in_specs=[pl.BlockSpec((tm, de), _sched_map(lambda s, se, sj, sb, *_: (sb[s], 0))),
                      pl.BlockSpec(memory_space=pl.ANY)],
            out_specs=pl.BlockSpec((tm, d // 2), _sched_map(lambda s, se, sj, sb, oj, ob, *_: (ob[s], 0))),
            scratch_shapes=[pltpu.VMEM((de, d), F32), pltpu.VMEM((de, d), BF16), pltpu.SemaphoreType.DMA((1,))]),
        out_shape=jax.ShapeDtypeStruct((r, d // 2), jnp.uint32),
        compiler_params=_params("arbitrary"),
        name="moe_down",
    )(*sched, hid, w_down)


def _combine_kernel(npb, alpha, pos_ref, y_hbm, x1_ref, rt_ref, gt_ref, g_ref, b_ref,
                    op_ref, os_ref, buf_ref, sems):
    i = pl.program_id(0)
    n = pl.num_programs(0)
    tm = x1_ref.shape[0]

    def rows(wait, blk, par):
        def body(r, carry):
            for k in range(TOP_K):
                cp = _row_copy(y_hbm, buf_ref.at[par, k], pos_ref[(blk * tm + r) * TOP_K + k], r, sems.at[par])
                cp.wait() if wait else cp.start()
            return carry
        lax.fori_loop(0, tm, body, 0, unroll=8)

    @pl.when(i == 0)
    def _():
        rows(False, 0, 0)

    @pl.when(i + 1 < n)
    def _():
        _on_parity(i + 1, functools.partial(rows, False))

    _on_parity(i, functools.partial(rows, True))
    rt = rt_ref[...]
    par = lax.rem(i, 2)
    f = rt[:, 2:3] * _unpack_bf16_pairs(buf_ref[par, 0]) + rt[:, 3:4] * _unpack_bf16_pairs(buf_ref[par, 1])
    out = _ln_rows(alpha * x1_ref[...] + gt_ref[...] * f) * g_ref[...] + b_ref[...]

    @pl.when(i < npb)
    def _():
        op_ref[...] = out

    @pl.when(i >= npb)
    def _():
        os_ref[...] = out


def _combine(yb, pos, x1, route, gt, ln_g, ln_b, alpha, tp, dec_seq):
    t, d = x1.shape
    tm = TM_ROW
    npb = tp // tm
    cidx = lambda i: jnp.where(i < npb, 0, 1 + (jnp.maximum(i - npb, 0) * tm) // dec_seq)
    vec = pl.BlockSpec((1, d), lambda i, *_: (0, 0))
    return pl.pallas_call(
        functools.partial(_combine_kernel, npb, alpha),
        grid_spec=pltpu.PrefetchScalarGridSpec(
            num_scalar_prefetch=1,
            grid=(t // tm,),
            in_specs=[pl.BlockSpec(memory_space=pl.ANY),
                      pl.BlockSpec((tm, d), lambda i, *_: (i, 0)),
                      pl.BlockSpec((tm, ROUTER_LANES), lambda i, *_: (i, 0)),
                      pl.BlockSpec((None, 1, d), lambda i, *_: (cidx(i), 0, 0)),
                      vec, vec],
            out_specs=[pl.BlockSpec((tm, d), lambda i, *_: (jnp.minimum(i, npb - 1), 0)),
                       pl.BlockSpec((tm, d), lambda i, *_: (jnp.maximum(i - npb, 0), 0))],
            scratch_shapes=[pltpu.VMEM((2, TOP_K, tm, d // 2), jnp.uint32), pltpu.SemaphoreType.DMA((2,))]),
        out_shape=[jax.ShapeDtypeStruct((tp, d), F32), jax.ShapeDtypeStruct((t - tp, d), F32)],
        compiler_params=_params("arbitrary"),
        name="moe_combine_ln",
    )(pos, yb, x1, route, gt, ln_g.reshape(1, d), ln_b.reshape(1, d))


def _dispatch_plan(route, counts_f, n_tok):
    tm = TM_MOE
    n_blocks = n_tok * TOP_K // tm + N_EXPERTS
    i32 = jnp.int32
    eid = route[:, 0:TOP_K].astype(i32).reshape(-1)
    slot = route[:, 4:4 + TOP_K].astype(i32).reshape(-1)
    counts = counts_f[0, :N_EXPERTS].astype(i32)
    nb = (counts + tm - 1) // tm
    nb_end = jnp.cumsum(nb)
    bs = nb_end - nb
    n_act = nb_end[-1]
    experts = jnp.arange(N_EXPERTS, dtype=i32)
    look = lambda table, e: jnp.sum(jnp.where(e[:, None] == experts[None, :], table[None, :], 0), axis=1)

    def schedule(n_inner):
        n_steps = n_inner * n_blocks
        n_live = n_inner * n_act

        def decode(step):
            s = jnp.minimum(step, n_live - 1)
            e = jnp.minimum(jnp.sum((s[:, None] >= n_inner * nb_end[None, :]).astype(i32), axis=1), N_EXPERTS - 1)
            nbe = jnp.maximum(look(nb, e), 1)
            loc = s - n_inner * look(bs, e)
            return e, loc // nbe, loc % nbe, nbe, look(bs, e)

        step = jnp.arange(n_steps, dtype=i32)
        e, sj, bi, nbe, bse = decode(step)
        live = step < n_live
        nxt_step = jnp.minimum(step, n_live - 1) + nbe - bi
        ne, nj, _, _, _ = decode(nxt_step)
        fst = (live & (bi == 0)).astype(i32)
        nxt = (live & (bi == 0) & (nxt_step < n_live)).astype(i32)
        spare = jnp.maximum(step - n_live, 0)
        oj = jnp.where(live, sj, spare % n_inner)
        ob = jnp.where(live, bse + bi, n_act + spare // n_inner)
        sched = (e, sj, bse + bi, oj, ob, fst, nxt, ne, nj, n_live.reshape(1))
        return tuple(a.astype(i32) for a in sched), n_steps

    pos = look(bs * tm, eid) + slot
    return pos, n_act.reshape(1), n_blocks, schedule


def kernel(x_prompt, x_sample, state_ret_fwd, state_ret_bwd, c, c_ctx, w_mod, b_mod, w_in, conv_w, conv_b, conv_ln_g, conv_ln_b, ret_decay_fwd, ret_decay_bwd, ret_gn_g, w_out, ln1_g, ln1_b, w_grp, b_grp, w_exp, b_exp, w_gate, w_up, w_down, ln2_g, ln2_b):
    depth = w_mod.shape[0]
    assert depth == 1, "single-layer step"
    bp, sp, d = x_prompt.shape
    bs_, ss, _ = x_sample.shape
    tp, ts = bp * sp, bs_ * ss
    t = tp + ts
    conv_width = conv_w.shape[2]
    ret_width = ret_gn_g.shape[1]
    assert ret_width == RET_HEADS * RET_D and sp % CHUNK == 0 and ss % CHUNK == 0
    alpha = (2.0 * depth) ** 0.25

    xp = x_prompt.reshape(tp, d)
    xs = x_sample.reshape(ts, d)
    cond8 = jnp.zeros((8, d), F32).at[0].set(c_ctx).at[1:1 + bs_].set(c)
    m = _modulation(cond8, w_mod[0], b_mod[0])
    sh1, sc1, gt1, sh2, sc2, gt2 = [m[:, k * d:(k + 1) * d].reshape(8, 1, d) for k in range(6)]

    h = _ln_modulate(xp, xs, sc1, sh1, ss)
    u_glu = _glu_proj(h, w_in[0], conv_width)
    cos_t, sin_t = _rope_tables(ss, RET_D)
    qkvg = _qkvg_proj(h, w_in[0], cos_t, sin_t, 2 * conv_width, 4 * ret_width, tp)

    u = _conv_module(u_glu, conv_w[0], conv_b[0], conv_ln_g[0], conv_ln_b[0], tp, ss)

    lg = jnp.stack([jax.nn.log_sigmoid(ret_decay_fwd[0].astype(F32)),
                    jax.nn.log_sigmoid(ret_decay_bwd[0].astype(F32))])
    gn = ret_gn_g[0].reshape(1, ret_width)
    r_p, new_f, new_b = _retention(qkvg, lg, gn, sp, 0, bp)
    r_s = _retention(qkvg, lg, gn, ss, tp // ss, bs_, state_ret_fwd, state_ret_bwd)
    r = jnp.concatenate([r_p, r_s], axis=0)

    v = _out_proj(u, r, w_out[0], xp, xs, gt1, alpha, ss)

    w_router = jnp.zeros((d, ROUTER_LANES), F32).at[:, :N_GROUPS].set(w_grp[0]).at[:, N_GROUPS:N_GROUPS + N_EXPERTS].set(w_exp[0]).astype(BF16)
    b_router = jnp.zeros((1, ROUTER_LANES), F32).at[0, :N_GROUPS].set(b_grp[0]).at[0, N_GROUPS:N_GROUPS + N_EXPERTS].set(b_exp[0])
    x1, h2p, route, counts = _post_mix(v, ln1_g[0], ln1_b[0], sc2, sh2, w_router, b_router, tp, ss)

    pos, n_act, n_blocks, schedule = _dispatch_plan(route, counts, t)
    xg = _dispatch_rows(h2p, pos, n_act, n_blocks)
    sched_up, n_up = schedule(w_gate.shape[3] // TF_MOE)
    hid = _expert_up(xg, w_gate[0], w_up[0], sched_up, n_up)
    sched_dn, n_dn = schedule(1)
    yb = _expert_down(hid, w_down[0], sched_dn, n_dn)

    out_p, out_s = _combine(yb, pos, x1, route, gt2, ln2_g[0], ln2_b[0], alpha, tp, ss)
    return (out_p.reshape(bp, sp, d), out_s.reshape(bs_, ss, d), new_f, new_b)
```

```python
import functools

import numpy as np
import jax
import jax.numpy as jnp
from jax import lax
from jax.experimental import pallas as pl
from jax.experimental.pallas import tpu as pltpu

F32 = jnp.float32
BF16 = jnp.bfloat16

LN_EPS = 1e-5
CONV_K = 31
CONV_HALO = 16
RET_HEADS = 8
RET_D = 256
CHUNK = 128
GRID_W = 64
ROPE_BASE = 10000.0
N_GROUPS = 4
EXPERTS_PER_GROUP = 8
N_EXPERTS = N_GROUPS * EXPERTS_PER_GROUP
TOP_K = 2
ROUTER_LANES = 128
VMEM_LIMIT = 56 * 1024 * 1024
VMEM_LIMIT_MAX = 60 * 1024 * 1024

TM_PROJ = 1024
TM_ROW = 256
TM_MOE = 256
TF_MOE = 512


def _params(*sem):
    return pltpu.CompilerParams(dimension_semantics=tuple(sem), vmem_limit_bytes=VMEM_LIMIT)


def _sigmoid(x):
    return 1.0 / (1.0 + jnp.exp(-x))


def _ln_rows(x):
    mu = jnp.mean(x, axis=-1, keepdims=True)
    xc = x - mu
    var = jnp.mean(xc * xc, axis=-1, keepdims=True)
    return xc * lax.rsqrt(var + LN_EPS)


def _mod_kernel(c_ref, w_ref, b_ref, o_ref):
    c = c_ref[...]
    s = c * _sigmoid(c)
    o_ref[...] = jnp.dot(s.astype(BF16), w_ref[...].astype(BF16), preferred_element_type=F32) + b_ref[...]


def _modulation(cond8, w_mod, b_mod):
    d, n = w_mod.shape
    tn = 512
    return pl.pallas_call(
        _mod_kernel,
        grid=(n // tn,),
        in_specs=[pl.BlockSpec((8, d), lambda j: (0, 0)),
                  pl.BlockSpec((d, tn), lambda j: (0, j)),
                  pl.BlockSpec((1, tn), lambda j: (0, j))],
        out_specs=pl.BlockSpec((8, tn), lambda j: (0, j)),
        out_shape=jax.ShapeDtypeStruct((8, n), F32),
        compiler_params=_params("arbitrary"),
        name="modulation",
    )(cond8, w_mod, b_mod.reshape(1, n))


def _ln_mod_kernel(npb, xp_ref, xs_ref, sc_ref, sh_ref, o_ref):
    def body(x_ref):
        y = _ln_rows(x_ref[...])
        o_ref[...] = (y * (1.0 + sc_ref[...]) + sh_ref[...]).astype(o_ref.dtype)

    i = pl.program_id(0)
    pl.when(i < npb)(lambda: body(xp_ref))
    pl.when(i >= npb)(lambda: body(xs_ref))


def _ln_modulate(xp, xs, sc, sh, dec_seq):
    (tp, d), ts = xp.shape, xs.shape[0]
    tm = 512
    npb, nsb = tp // tm, ts // tm
    cidx = lambda i: jnp.where(i < npb, 0, 1 + (jnp.maximum(i - npb, 0) * tm) // dec_seq)
    return pl.pallas_call(
        functools.partial(_ln_mod_kernel, npb),
        grid=(npb + nsb,),
        in_specs=[pl.BlockSpec((tm, d), lambda i: (jnp.minimum(i, npb - 1), 0)),
                  pl.BlockSpec((tm, d), lambda i: (jnp.maximum(i - npb, 0), 0)),
                  pl.BlockSpec((None, 1, d), lambda i: (cidx(i), 0, 0)),
                  pl.BlockSpec((None, 1, d), lambda i: (cidx(i), 0, 0))],
        out_specs=pl.BlockSpec((tm, d), lambda i: (i, 0)),
        out_shape=jax.ShapeDtypeStruct((tp + ts, d), BF16),
        compiler_params=_params("arbitrary"),
        name="ln_modulate",
    )(xp, xs, sc, sh)


def _glu_proj_kernel(h_ref, wv_ref, wg_ref, o_ref):
    h = h_ref[...]
    a = jnp.dot(h, wv_ref[...].astype(BF16), preferred_element_type=F32)
    g = jnp.dot(h, wg_ref[...].astype(BF16), preferred_element_type=F32)
    o_ref[...] = a * _sigmoid(g)


def _glu_proj(h, w_in, conv_width):
    t, d = h.shape
    tm, tn = TM_PROJ, 256
    goff = conv_width // tn
    return pl.pallas_call(
        _glu_proj_kernel,
        grid=(t // tm, conv_width // tn),
        in_specs=[pl.BlockSpec((tm, d), lambda i, j: (i, 0)),
                  pl.BlockSpec((d, tn), lambda i, j: (0, j)),
                  pl.BlockSpec((d, tn), lambda i, j: (0, j + goff))],
        out_specs=pl.BlockSpec((tm, tn), lambda i, j: (i, j)),
        out_shape=jax.ShapeDtypeStruct((t, conv_width), F32),
        compiler_params=_params("arbitrary", "arbitrary"),
        name="glu_proj",
    )(h, w_in, w_in)


def _qkvg_proj_kernel(npb, tiles_per_part, k_scale, h_ref, w_ref, cos_ref, sin_ref, o_ref):
    i, j = pl.program_id(0), pl.program_id(1)
    z = jnp.dot(h_ref[...], w_ref[...].astype(BF16), preferred_element_type=F32)
    tpp = tiles_per_part

    @pl.when(j >= 3 * tpp)
    def _():
        o_ref[...] = (z * _sigmoid(z)).astype(o_ref.dtype)

    @pl.when((j >= 2 * tpp) & (j < 3 * tpp))
    def _():
        o_ref[...] = z.astype(o_ref.dtype)

    @pl.when(j < 2 * tpp)
    def _():
        zz = z * jnp.where(j >= tpp, k_scale, 1.0).astype(F32)

        @pl.when(i < npb)
        def _():
            o_ref[...] = zz.astype(o_ref.dtype)

        @pl.when(i >= npb)
        def _():
            for s in range(zz.shape[1] // 128):
                cs = slice(s * 128, (s + 1) * 128)
                ts_ = slice((s * 128) % RET_D, (s * 128) % RET_D + 128)
                zs = zz[:, cs]
                o_ref[:, cs] = (zs * cos_ref[:, ts_] + pltpu.roll(zs, 64, axis=1) * sin_ref[:, ts_]).astype(o_ref.dtype)


def _qkvg_proj(h, w_in, cos_t, sin_t, col0, n_cols, tp):
    t, d = h.shape
    tm, tn = TM_PROJ, 512
    assert cos_t.shape == (tm, RET_D), "a projection row tile is one latent sequence"
    npb = tp // tm
    tpp = (n_cols // 4) // tn
    return pl.pallas_call(
        functools.partial(_qkvg_proj_kernel, npb, tpp, RET_D ** -0.5),
        grid=(t // tm, n_cols // tn),
        in_specs=[pl.BlockSpec((tm, d), lambda i, j: (i, 0)),
                  pl.BlockSpec((d, tn), lambda i, j: (0, j + col0 // tn)),
                  pl.BlockSpec((tm, RET_D), lambda i, j: (0, 0)),
                  pl.BlockSpec((tm, RET_D), lambda i, j: (0, 0))],
        out_specs=pl.BlockSpec((tm, tn), lambda i, j: (i, j)),
        out_shape=jax.ShapeDtypeStruct((t, n_cols), BF16),
        compiler_params=_params("arbitrary", "arbitrary"),
        name="qkvg_proj",
    )(h, w_in, cos_t, sin_t)


def _rope_tables(n_tok, width):
    rows = n_tok // GRID_W
    r_idx = np.repeat(np.arange(rows), GRID_W).astype(np.float64)
    c_idx = np.tile(np.arange(GRID_W), rows).astype(np.float64)
    dq = RET_D // 2
    inv = ROPE_BASE ** (-np.arange(dq // 2, dtype=np.float64) / (dq // 2))
    sign = np.concatenate([-np.ones(dq // 2), np.ones(dq // 2)])
    cos_h, sin_h = [], []
    for idx in (r_idx, c_idx):
        ang = idx[:, None] * inv
        cos_h.append(np.concatenate([np.cos(ang), np.cos(ang)], axis=1))
        sin_h.append(np.concatenate([np.sin(ang), np.sin(ang)], axis=1) * sign)
    cos_h, sin_h = np.concatenate(cos_h, axis=1), np.concatenate(sin_h, axis=1)
    reps = width // RET_D
    return (jnp.asarray(np.tile(cos_h, (1, reps)), F32), jnp.asarray(np.tile(sin_h, (1, reps)), F32))


def _conv_kernel(npb, tiles_per_seq, uc_ref, up_ref, un_ref, w_ref, b_ref, g_ref, bt_ref, o_ref, buf_ref, y_ref):
    r = pl.program_id(0)
    tm = uc_ref.shape[0]
    ncg = buf_ref.shape[0]
    t = lax.rem(jnp.maximum(r - npb, 0), tiles_per_seq)
    is_s = r >= npb
    has_prev = is_s & (t != 0)
    has_next = is_s & (t != tiles_per_seq - 1)
    for cg in range(ncg):
        cs = slice(cg * 128, (cg + 1) * 128)
        buf_ref[cg, 0:CONV_HALO, :] = jnp.where(has_prev, up_ref[:, cs], 0.0)
        buf_ref[cg, CONV_HALO:CONV_HALO + tm, :] = uc_ref[:, cs]
        buf_ref[cg, CONV_HALO + tm:, :] = jnp.where(has_next, un_ref[:, cs], 0.0)

    off = CONV_HALO - CONV_K // 2

    def body(cg, carry):
        acc = jnp.zeros((tm, 128), F32)
        for tap in range(CONV_K):
            acc = acc + buf_ref[cg, off + tap:off + tap + tm, :] * w_ref[cg, tap:tap + 1, :]
        y_ref[cg] = acc + b_ref[cg]
        return carry

    lax.fori_loop(0, ncg, body, 0)

    n_ch = ncg * 128
    tot = y_ref[0]
    for cg in range(1, ncg):
        tot = tot + y_ref[cg]
    mu = jnp.sum(tot, axis=1, keepdims=True) * (1.0 / n_ch)
    sq = jnp.zeros((tm, 128), F32)
    for cg in range(ncg):
        dv = y_ref[cg] - mu
        sq = sq + dv * dv
    var = jnp.sum(sq, axis=1, keepdims=True) * (1.0 / n_ch)
    rstd = lax.rsqrt(var + LN_EPS)
    for cg in range(ncg):
        cs = slice(cg * 128, (cg + 1) * 128)
        v = (y_ref[cg] - mu) * rstd * g_ref[cg] + bt_ref[cg]
        o_ref[:, cs] = (v * _sigmoid(v)).astype(o_ref.dtype)


def _conv_module(u, conv_w, conv_b, ln_g, ln_b, tp, dec_seq):
    t, c = u.shape
    tm = TM_ROW
    ncg = c // 128
    npb = tp // tm
    hb = tm // CONV_HALO
    n_halo_blocks = t // CONV_HALO
    w3 = jnp.zeros((32, c), F32).at[:CONV_K].set(conv_w).reshape(32, ncg, 128).transpose(1, 0, 2)
    vec = lambda a: a.reshape(ncg, 1, 128)
    return pl.pallas_call(
        functools.partial(_conv_kernel, npb, dec_seq // tm),
        grid=(t // tm,),
        in_specs=[pl.BlockSpec((tm, c), lambda r: (r, 0)),
                  pl.BlockSpec((CONV_HALO, c), lambda r: (jnp.maximum(r * hb - 1, 0), 0)),
                  pl.BlockSpec((CONV_HALO, c), lambda r: (jnp.minimum((r + 1) * hb, n_halo_blocks - 1), 0)),
                  pl.BlockSpec((ncg, 32, 128), lambda r: (0, 0, 0)),
                  pl.BlockSpec((ncg, 1, 128), lambda r: (0, 0, 0)),
                  pl.BlockSpec((ncg, 1, 128), lambda r: (0, 0, 0)),
                  pl.BlockSpec((ncg, 1, 128), lambda r: (0, 0, 0))],
        out_specs=pl.BlockSpec((tm, c), lambda r: (r, 0)),
        out_shape=jax.ShapeDtypeStruct((t, c), BF16),
        scratch_shapes=[pltpu.VMEM((ncg, tm + 2 * CONV_HALO, 128), F32),
                        pltpu.VMEM((ncg, tm, 128), F32)],
        compiler_params=_params("arbitrary"),
        name="conv_ln_swish",
    )(u, u, u, w3, vec(conv_b), vec(ln_g), vec(ln_b))


def _retention_kernel(nc, has_init, lg_ref, q_ref, k_ref, v_ref, g_ref, gn_ref, *rest):
    if has_init:
        s0f_ref, s0b_ref, r_ref, o_ref, sf_ref, sb_ref = rest
    else:
        r_ref, sf_ref, sb_ref, o_ref = rest
    hd = pl.program_id(1)
    lgf, lgb = lg_ref[0, hd], lg_ref[1, hd]
    c = CHUNK
    row = lax.broadcasted_iota(jnp.int32, (c, c), 0)
    col = lax.broadcasted_iota(jnp.int32, (c, c), 1)
    diff = (row - col).astype(F32)
    dec = (jnp.where(diff >= 0, jnp.exp(jnp.maximum(diff, 0.0) * lgf), 0.0)
           + jnp.where(diff <= 0, jnp.exp(jnp.maximum(-diff, 0.0) * lgb), 0.0))
    pos = lax.broadcasted_iota(jnp.int32, (c, 1), 0).astype(F32)
    xi_f = jnp.exp((pos + 1.0) * lgf)
    zeta_f = jnp.exp((c - 1.0 - pos) * lgf)
    xi_b = jnp.exp((c - pos) * lgb)
    zeta_b = jnp.exp(pos * lgb)
    gch_f = jnp.exp(jnp.full((1, RET_D), c, F32) * lgf)
    gch_b = jnp.exp(jnp.full((1, RET_D), c, F32) * lgb)

    if has_init:
        sf_ref[...] = s0f_ref[...]
        sb_ref[...] = s0b_ref[...]
    else:
        sf_ref[...] = jnp.zeros_like(sf_ref)
        sb_ref[...] = jnp.zeros_like(sb_ref)

    def chunk(ci):
        sl = slice(ci * c, (ci + 1) * c)
        return q_ref[sl, :], k_ref[sl, :], v_ref[sl, :], sl

    def state_update(s_ref, kc, vc, zeta, gch):
        kz = (kc.astype(F32) * zeta).T.astype(BF16)
        s_ref[...] = gch * s_ref[...] + jnp.dot(kz, vc, preferred_element_type=F32)

    for ci in range(nc):
        qc, kc, vc, sl = chunk(ci)
        s = lax.dot_general(qc, kc, (((1,), (1,)), ((), ())), preferred_element_type=F32)
        p = (s * dec).astype(BF16)
        o = jnp.dot(p, vc, preferred_element_type=F32)
        o = o + jnp.dot(qc, sf_ref[...].astype(BF16), preferred_element_type=F32) * xi_f
        o_ref[sl, :] = o
        state_update(sf_ref, kc, vc, zeta_f, gch_f)

    for ci in reversed(range(nc)):
        qc, kc, vc, sl = chunk(ci)
        o_ref[sl, :] = o_ref[sl, :] + jnp.dot(qc, sb_ref[...].astype(BF16), preferred_element_type=F32) * xi_b
        state_update(sb_ref, kc, vc, zeta_b, gch_b)

    y = _ln_rows(o_ref[...])
    r_ref[...] = (g_ref[...].astype(F32) * (y * gn_ref[...])).astype(r_ref.dtype)


def _retention(qkvg, lg, gn, n, row_blk0, n_seq, s0f=None, s0b=None):
    has_init = s0f is not None
    nh, dd = RET_HEADS, RET_D
    col = lambda part: (lambda b, h: (b + row_blk0, part * nh + h))
    st_spec = pl.BlockSpec((None, None, None, dd, dd), lambda b, h: (b, 0, h, 0, 0))
    in_specs = [pl.BlockSpec(memory_space=pltpu.SMEM),
                pl.BlockSpec((n, dd), col(0)), pl.BlockSpec((n, dd), col(1)),
                pl.BlockSpec((n, dd), col(2)), pl.BlockSpec((n, dd), col(3)),
                pl.BlockSpec((1, dd), lambda b, h: (0, h))]
    args = [lg, qkvg, qkvg, qkvg, qkvg, gn]
    r_shape = jax.ShapeDtypeStruct((n_seq * n, nh * dd), BF16)
    r_spec = pl.BlockSpec((n, dd), lambda b, h: (b, h))
    st_scratch = pltpu.VMEM((dd, dd), F32)
    if has_init:
        in_specs += [st_spec, st_spec]
        args += [s0f, s0b]
        out_specs, out_shape = r_spec, r_shape
        scratch = [pltpu.VMEM((n, dd), F32), st_scratch, st_scratch]
    else:
        st_shape = jax.ShapeDtypeStruct((n_seq, 1, nh, dd, dd), F32)
        out_specs, out_shape = [r_spec, st_spec, st_spec], [r_shape, st_shape, st_shape]
        scratch = [pltpu.VMEM((n, dd), F32)]
    return pl.pallas_call(
        functools.partial(_retention_kernel, n // CHUNK, has_init),
        grid=(n_seq, nh),
        in_specs=in_specs, out_specs=out_specs, out_shape=out_shape,
        scratch_shapes=scratch,
        compiler_params=_params("arbitrary", "arbitrary"),
        name="retention_latent" if has_init else "retention_context",
    )(*args)


def _out_proj_kernel(npb, alpha, u_ref, r_ref, w1_ref, w2_ref, xp_ref, xs_ref, gt_ref, o_ref):
    y = (jnp.dot(u_ref[...], w1_ref[...].astype(BF16), preferred_element_type=F32)
         + jnp.dot(r_ref[...], w2_ref[...].astype(BF16), preferred_element_type=F32))
    i = pl.program_id(0)

    @pl.when(i < npb)
    def _():
        o_ref[...] = alpha * xp_ref[...] + gt_ref[...] * y

    @pl.when(i >= npb)
    def _():
        o_ref[...] = alpha * xs_ref[...] + gt_ref[...] * y


def _out_proj(u, r, w_out, xp, xs, gt, alpha, dec_seq):
    t, kh = u.shape
    d = w_out.shape[1]
    tm, tn = TM_PROJ, 512
    npb = xp.shape[0] // tm
    cidx = lambda i: jnp.where(i < npb, 0, 1 + (jnp.maximum(i - npb, 0) * tm) // dec_seq)
    return pl.pallas_call(
        functools.partial(_out_proj_kernel, npb, alpha),
        grid=(t // tm, d // tn),
        in_specs=[pl.BlockSpec((tm, kh), lambda i, j: (i, 0)),
                  pl.BlockSpec((tm, kh), lambda i, j: (i, 0)),
                  pl.BlockSpec((kh, tn), lambda i, j: (0, j)),
                  pl.BlockSpec((kh, tn), lambda i, j: (1, j)),
                  pl.BlockSpec((tm, tn), lambda i, j: (jnp.minimum(i, npb - 1), j)),
                  pl.BlockSpec((tm, tn), lambda i, j: (jnp.maximum(i - npb, 0), j)),
                  pl.BlockSpec((None, 1, tn), lambda i, j: (cidx(i), 0, j))],
        out_specs=pl.BlockSpec((tm, tn), lambda i, j: (i, j)),
        out_shape=jax.ShapeDtypeStruct((t, d), F32),
        compiler_params=_params("arbitrary", "arbitrary"),
        name="out_proj_residual",
    )(u, r, w_out, w_out, xp, xs, gt)


def _pack_bf16_pairs(x):
    n = x.shape[1] // 2
    lo = lax.bitcast_convert_type(x[:, :n].astype(BF16).astype(F32), jnp.uint32)
    hi = lax.bitcast_convert_type(x[:, n:].astype(BF16).astype(F32), jnp.uint32)
    return hi | (lo >> 16)


def _unpack_bf16_pairs(p):
    lo = lax.bitcast_convert_type(p << 16, F32)
    hi = lax.bitcast_convert_type(p & jnp.uint32(0xFFFF0000), F32)
    return jnp.concatenate([lo, hi], axis=1)


def _store_as_slabs(ref, rows):
    m, n = rows.shape
    slab = n // 128
    for s in range(slab):
        ref[pl.ds(s, m, stride=slab), :] = rows[:, s * 128:(s + 1) * 128]


def _load_from_slabs(ref, m):
    slab = ref.shape[0] // m
    return jnp.concatenate([ref[pl.ds(s, m, stride=slab), :] for s in range(slab)], axis=1)


def _slab_copy(src, dst, src_row, dst_row, slab, sem):
    return pltpu.make_async_copy(src.at[pl.ds(pl.multiple_of(src_row * slab, slab), slab), :],
                                 dst.at[pl.ds(pl.multiple_of(dst_row * slab, slab), slab), :], sem)


def _post_mix_kernel(v_ref, g1_ref, b1_ref, sc_ref, sh_ref, wr_ref, br_ref, x1_ref, h2_ref, rt_ref, cnt_ref):
    @pl.when(pl.program_id(0) == 0)
    def _():
        cnt_ref[...] = jnp.zeros_like(cnt_ref)

    x1 = _ln_rows(v_ref[...]) * g1_ref[...] + b1_ref[...]
    x1_ref[...] = x1
    h2 = _ln_rows(x1) * (1.0 + sc_ref[...]) + sh_ref[...]
    _store_as_slabs(h2_ref, _pack_bf16_pairs(h2))
    logits = jnp.dot(h2.astype(BF16), wr_ref[...], preferred_element_type=F32) + br_ref[...]
    lane = lax.broadcasted_iota(jnp.int32, logits.shape, 1)
    big = jnp.int32(ROUTER_LANES)
    neg = jnp.float32(-jnp.inf)

    def first_lane_of_max(vals):
        m = jnp.max(vals, axis=1, keepdims=True)
        return m, jnp.min(jnp.where(vals == m, lane, big), axis=1, keepdims=True)

    lgt = jnp.where(lane < N_GROUPS, logits, neg)
    eg = jnp.exp(lgt - jnp.max(lgt, axis=1, keepdims=True))
    pg = eg / jnp.sum(eg, axis=1, keepdims=True)
    grp_prob, grp = first_lane_of_max(jnp.where(lane < N_GROUPS, pg, -1.0))
    lo = N_GROUPS + grp * EXPERTS_PER_GROUP
    in_grp = (lane >= lo) & (lane < lo + EXPERTS_PER_GROUP)
    let = jnp.where(in_grp, logits, neg)
    ee = jnp.exp(let - jnp.max(let, axis=1, keepdims=True))
    pe = jnp.where(in_grp, ee / jnp.sum(ee, axis=1, keepdims=True), -1.0)
    p1, l1 = first_lane_of_max(pe)
    p2, l2 = first_lane_of_max(jnp.where(lane == l1, -1.0, pe))
    den = p1 + p2
    gate1, gate2 = grp_prob * p1 / den, grp_prob * p2 / den
    e1, e2 = l1 - N_GROUPS, l2 - N_GROUPS
    tm = logits.shape[0]
    hot1 = (lane == e1).astype(F32)
    hot2 = (lane == e2).astype(F32)
    earlier = (lax.broadcasted_iota(jnp.int32, (tm, tm), 0) > lax.broadcasted_iota(jnp.int32, (tm, tm), 1)).astype(BF16)
    before1 = jnp.dot(earlier, hot1.astype(BF16), preferred_element_type=F32) + cnt_ref[...]
    n1 = jnp.sum(hot1, axis=0, keepdims=True)
    before2 = jnp.dot(earlier, hot2.astype(BF16), preferred_element_type=F32) + cnt_ref[...] + n1
    slot1 = jnp.sum(before1 * hot1, axis=1, keepdims=True)
    slot2 = jnp.sum(before2 * hot2, axis=1, keepdims=True)
    cnt_ref[...] = cnt_ref[...] + n1 + jnp.sum(hot2, axis=0, keepdims=True)
    vals = (e1.astype(F32), e2.astype(F32), gate1, gate2, slot1, slot2)
    out = jnp.zeros_like(logits)
    for k, val in enumerate(vals):
        out = jnp.where(lane == k, val, out)
    rt_ref[...] = out


def _post_mix(v, ln_g, ln_b, sc, sh, w_router, b_router, tp, dec_seq):
    t, d = v.shape
    tm = TM_ROW
    npb = tp // tm
    cidx = lambda i: jnp.where(i < npb, 0, 1 + (jnp.maximum(i - npb, 0) * tm) // dec_seq)
    row = pl.BlockSpec((tm, d), lambda i: (i, 0))
    vec = pl.BlockSpec((1, d), lambda i: (0, 0))
    cvec = pl.BlockSpec((None, 1, d), lambda i: (cidx(i), 0, 0))
    slab = d // 2 // 128
    return pl.pallas_call(
        _post_mix_kernel,
        grid=(t // tm,),
        in_specs=[row, vec, vec, cvec, cvec,
                  pl.BlockSpec((d, ROUTER_LANES), lambda i: (0, 0)),
                  pl.BlockSpec((1, ROUTER_LANES), lambda i: (0, 0))],
        out_specs=[row, pl.BlockSpec((tm * slab, 128), lambda i: (i, 0)),
                   pl.BlockSpec((tm, ROUTER_LANES), lambda i: (i, 0)),
                   pl.BlockSpec((1, ROUTER_LANES), lambda i: (0, 0))],
        out_shape=[jax.ShapeDtypeStruct((t, d), F32), jax.ShapeDtypeStruct((t * slab, 128), jnp.uint32),
                   jax.ShapeDtypeStruct((t, ROUTER_LANES), F32), jax.ShapeDtypeStruct((1, ROUTER_LANES), F32)],
        compiler_params=_params("arbitrary"),
        name="ln_ln_router",
    )(v, ln_g.reshape(1, d), ln_b.reshape(1, d), sc, sh, w_router, b_router)


def _on_parity(blk, fn):
    for par in range(2):
        pl.when(lax.rem(blk, 2) == par)(functools.partial(fn, blk, par))


def _dispatch_kernel(pos_ref, nact_ref, h_hbm, o_ref, tok_ref, buf_ref, sems):
    b = pl.program_id(0)
    n_act = nact_ref[0]
    tm = TM_MOE
    slab = o_ref.shape[0] // tm
    n_tok = pos_ref.shape[0] // TOP_K

    @pl.when(b == 0)
    def _():
        def clear(r, carry):
            tok_ref[r] = 0
            return carry

        def fill(tok, carry):
            for k in range(TOP_K):
                tok_ref[pos_ref[tok * TOP_K + k]] = tok
            return carry

        lax.fori_loop(0, tok_ref.shape[0], clear, 0, unroll=8)
        lax.fori_loop(0, n_tok, fill, 0, unroll=4)

    def rows(wait, blk, par):
        def body(r, carry):
            cp = _slab_copy(h_hbm, buf_ref.at[par], tok_ref[blk * tm + r], r, slab, sems.at[par])
            cp.wait() if wait else cp.start()
            return carry
        lax.fori_loop(0, tm, body, 0, unroll=8)

    @pl.when(b == 0)
    def _():
        rows(False, 0, 0)

    @pl.when(b + 1 < n_act)
    def _():
        _on_parity(b + 1, functools.partial(rows, False))

    @pl.when(b < n_act)
    def _():
        _on_parity(b, functools.partial(rows, True))
        o_ref[...] = buf_ref[lax.rem(b, 2)]

    @pl.when(b >= n_act)
    def _():
        o_ref[...] = jnp.zeros_like(o_ref)


def _dispatch_rows(h2p, pos, n_act, n_blocks, slab):
    tm = TM_MOE
    return pl.pallas_call(
        _dispatch_kernel,
        grid_spec=pltpu.PrefetchScalarGridSpec(
            num_scalar_prefetch=2,
            grid=(n_blocks,),
            in_specs=[pl.BlockSpec(memory_space=pl.ANY)],
            out_specs=pl.BlockSpec((tm * slab, 128), lambda b, *_: (b, 0)),
            scratch_shapes=[pltpu.SMEM((n_blocks * tm,), jnp.int32),
                            pltpu.VMEM((2, tm * slab, 128), jnp.uint32), pltpu.SemaphoreType.DMA((2,))]),
        out_shape=jax.ShapeDtypeStruct((n_blocks * tm * slab, 128), jnp.uint32),
        compiler_params=_params("arbitrary"),
        name="moe_dispatch",
    )(pos, n_act, h2p)


def _expert_up_kernel(se_ref, sj_ref, sb_ref, oj_ref, ob_ref, fst_ref, gp_ref, nxt_ref, ne_ref, nj_ref, ns_ref,
                      x_ref, wg_hbm, wu_hbm, o_ref, wg_st, wu_st, wg_bf, wu_bf, sems):
    live = pl.program_id(0) < ns_ref[0]
    _stream_group_weights(se_ref, sj_ref, fst_ref, gp_ref, nxt_ref, ne_ref, nj_ref, live,
                          ((wg_hbm, wg_st, wg_bf), (wu_hbm, wu_st, wu_bf)), sems)

    @pl.when(live)
    def _():
        x = _unpack_bf16_pairs(_load_from_slabs(x_ref, TM_MOE)).astype(BF16)
        a = jnp.dot(x, wg_bf[...], preferred_element_type=F32)
        u = jnp.dot(x, wu_bf[...], preferred_element_type=F32)
        o_ref[...] = ((a * _sigmoid(a)) * u).astype(o_ref.dtype)

    @pl.when(jnp.logical_not(live))
    def _():
        o_ref[...] = jnp.zeros_like(o_ref)


def _stream_group_weights(se_ref, sj_ref, fst_ref, gp_ref, nxt_ref, ne_ref, nj_ref, live, weights, sems):
    s = pl.program_id(0)

    def copies(e, j, slot):
        out = []
        for k, (w_hbm, w_st, _) in enumerate(weights):
            n_slots, _, tn = w_st.shape
            src = w_hbm.at[e, :, pl.ds(pl.multiple_of(j * tn, tn), tn)]
            out.append(pltpu.make_async_copy(src, w_st.at[slot % n_slots], sems.at[k, slot % n_slots]))
        return out

    def start_next(slot):
        @pl.when(nxt_ref[s] == 1)
        def _():
            for cp in copies(ne_ref[s], nj_ref[s], slot):
                cp.start()

    two_slots = all(w_st.shape[0] == 2 for _, w_st, _ in weights)

    def first_step(slot):
        cur = copies(se_ref[s], sj_ref[s], slot)

        @pl.when(s == 0)
        def _():
            for cp in cur:
                cp.start()

        if two_slots:
            start_next(1 - slot)
        for cp, (_, w_st, w_bf) in zip(cur, weights):
            cp.wait()
            w_bf[...] = w_st[slot % w_st.shape[0]].astype(BF16)
        if not two_slots:
            start_next(slot)

    first = live & (fst_ref[s] == 1)
    if two_slots:
        for slot in range(2):
            pl.when(first & (gp_ref[s] == slot))(functools.partial(first_step, slot))
    else:
        pl.when(first)(functools.partial(first_step, 0))


_N_SCHED = 11


def _sched_map(fn):
    return lambda s, *refs: fn(s, *refs[:_N_SCHED])


def _expert_up(xs, w_gate, w_up, sched, n_steps):
    d, de = w_gate.shape[1], w_gate.shape[2]
    tm, tf = TM_MOE, TF_MOE
    slab = d // 2 // 128
    r = xs.shape[0] // slab
    return pl.pallas_call(
        _expert_up_kernel,
        grid_spec=pltpu.PrefetchScalarGridSpec(
            num_scalar_prefetch=_N_SCHED,
            grid=(n_steps,),
            in_specs=[pl.BlockSpec((tm * slab, 128), _sched_map(lambda s, se, sj, sb, *_: (sb[s], 0))),
                      pl.BlockSpec(memory_space=pl.ANY), pl.BlockSpec(memory_space=pl.ANY)],
            out_specs=pl.BlockSpec((tm, tf), _sched_map(lambda s, se, sj, sb, oj, ob, *_: (ob[s], oj[s]))),
            scratch_shapes=[pltpu.VMEM((2, d, tf), F32), pltpu.VMEM((2, d, tf), F32),
                            pltpu.VMEM((d, tf), BF16), pltpu.VMEM((d, tf), BF16),
                            pltpu.SemaphoreType.DMA((2, 2))]),
        out_shape=jax.ShapeDtypeStruct((r, de), BF16),
        compiler_params=pltpu.CompilerParams(dimension_semantics=("arbitrary",), vmem_limit_bytes=VMEM_LIMIT_MAX),
        name="moe_gate_up",
    )(*sched, xs, w_gate, w_up)


def _expert_down_kernel(se_ref, sj_ref, sb_ref, oj_ref, ob_ref, fst_ref, gp_ref, nxt_ref, ne_ref, nj_ref, ns_ref,
                        h_ref, wd_hbm, o_ref, wd_st, wd_bf, sems):
    live = pl.program_id(0) < ns_ref[0]
    _stream_group_weights(se_ref, sj_ref, fst_ref, gp_ref, nxt_ref, ne_ref, nj_ref, live,
                          ((wd_hbm, wd_st, wd_bf),), sems)

    @pl.when(live)
    def _():
        _store_as_slabs(o_ref, _pack_bf16_pairs(jnp.dot(h_ref[...], wd_bf[...], preferred_element_type=F32)))

    @pl.when(jnp.logical_not(live))
    def _():
        o_ref[...] = jnp.zeros_like(o_ref)


def _expert_down(hid, w_down, sched, n_steps):
    r, de = hid.shape
    d = w_down.shape[2]
    tm = TM_MOE
    slab = d // 2 // 128
    return pl.pallas_call(
        _expert_down_kernel,
        grid_spec=pltpu.PrefetchScalarGridSpec(
            num_scalar_prefetch=_N_SCHED,
            grid=(n_steps,),
            in_specs=[pl.BlockSpec((tm, de), _sched_map(lambda s, se, sj, sb, *_: (sb[s], 0))),
                      pl.BlockSpec(memory_space=pl.ANY)],
            out_specs=pl.BlockSpec((tm * slab, 128), _sched_map(lambda s, se, sj, sb, oj, ob, *_: (ob[s], 0))),
            scratch_shapes=[pltpu.VMEM((1, de, d), F32), pltpu.VMEM((de, d), BF16), pltpu.SemaphoreType.DMA((1, 1))]),
        out_shape=jax.ShapeDtypeStruct((r * slab, 128), jnp.uint32),
        compiler_params=_params("arbitrary"),
        name="moe_down",
    )(*sched, hid, w_down)


def _combine_kernel(npb, alpha, pos_ref, y_hbm, x1_ref, rt_ref, gt_ref, g_ref, b_ref,
                    op_ref, os_ref, buf_ref, sems):
    i = pl.program_id(0)
    n = pl.num_programs(0)
    tm = x1_ref.shape[0]
    slab = buf_ref.shape[2] // tm

    def rows(wait, blk, par):
        def body(r, carry):
            for k in range(TOP_K):
                cp = _slab_copy(y_hbm, buf_ref.at[par, k], pos_ref[(blk * tm + r) * TOP_K + k], r, slab, sems.at[par])
                cp.wait() if wait else cp.start()
            return carry
        lax.fori_loop(0, tm, body, 0, unroll=8)

    @pl.when(i == 0)
    def _():
        rows(False, 0, 0)

    @pl.when(i + 1 < n)
    def _():
        _on_parity(i + 1, functools.partial(rows, False))

    _on_parity(i, functools.partial(rows, True))
    rt = rt_ref[...]
    par = lax.rem(i, 2)
    expert_rows = lambda k: _unpack_bf16_pairs(_load_from_slabs(buf_ref.at[par, k], tm))
    f = rt[:, 2:3] * expert_rows(0) + rt[:, 3:4] * expert_rows(1)
    out = _ln_rows(alpha * x1_ref[...] + gt_ref[...] * f) * g_ref[...] + b_ref[...]

    @pl.when(i < npb)
    def _():
        op_ref[...] = out

    @pl.when(i >= npb)
    def _():
        os_ref[...] = out


def _combine(yb, pos, x1, route, gt, ln_g, ln_b, alpha, tp, dec_seq):
    t, d = x1.shape
    tm = TM_ROW
    npb = tp // tm
    cidx = lambda i: jnp.where(i < npb, 0, 1 + (jnp.maximum(i - npb, 0) * tm) // dec_seq)
    vec = pl.BlockSpec((1, d), lambda i, *_: (0, 0))
    return pl.pallas_call(
        functools.partial(_combine_kernel, npb, alpha),
        grid_spec=pltpu.PrefetchScalarGridSpec(
            num_scalar_prefetch=1,
            grid=(t // tm,),
            in_specs=[pl.BlockSpec(memory_space=pl.ANY),
                      pl.BlockSpec((tm, d), lambda i, *_: (i, 0)),
                      pl.BlockSpec((tm, ROUTER_LANES), lambda i, *_: (i, 0)),
                      pl.BlockSpec((None, 1, d), lambda i, *_: (cidx(i), 0, 0)),
                      vec, vec],
            out_specs=[pl.BlockSpec((tm, d), lambda i, *_: (jnp.minimum(i, npb - 1), 0)),
                       pl.BlockSpec((tm, d), lambda i, *_: (jnp.maximum(i - npb, 0), 0))],
            scratch_shapes=[pltpu.VMEM((2, TOP_K, tm * (d // 2 // 128), 128), jnp.uint32),
                            pltpu.SemaphoreType.DMA((2,))]),
        out_shape=[jax.ShapeDtypeStruct((tp, d), F32), jax.ShapeDtypeStruct((t - tp, d), F32)],
        compiler_params=_params("arbitrary"),
        name="moe_combine_ln",
    )(pos, yb, x1, route, gt, ln_g.reshape(1, d), ln_b.reshape(1, d))


def _dispatch_plan(route, counts_f, n_tok):
    tm = TM_MOE
    n_blocks = n_tok * TOP_K // tm + N_EXPERTS
    i32 = jnp.int32
    eid = route[:, 0:TOP_K].astype(i32).reshape(-1)
    slot = route[:, 4:4 + TOP_K].astype(i32).reshape(-1)
    counts = counts_f[0, :N_EXPERTS].astype(i32)
    nb = (counts + tm - 1) // tm
    nb_end = jnp.cumsum(nb)
    bs = nb_end - nb
    n_act = nb_end[-1]
    experts = jnp.arange(N_EXPERTS, dtype=i32)
    look = lambda table, e: jnp.sum(jnp.where(e[:, None] == experts[None, :], table[None, :], 0), axis=1)

    def schedule(n_inner):
        n_steps = n_inner * n_blocks
        n_live = n_inner * n_act

        def decode(step):
            s = jnp.minimum(step, n_live - 1)
            e = jnp.minimum(jnp.sum((s[:, None] >= n_inner * nb_end[None, :]).astype(i32), axis=1), N_EXPERTS - 1)
            nbe = jnp.maximum(look(nb, e), 1)
            loc = s - n_inner * look(bs, e)
            return e, loc // nbe, loc % nbe, nbe, look(bs, e)

        step = jnp.arange(n_steps, dtype=i32)
        e, sj, bi, nbe, bse = decode(step)
        live = step < n_live
        nxt_step = jnp.minimum(step, n_live - 1) + nbe - bi
        ne, nj, _, _, _ = decode(nxt_step)
        fst = (live & (bi == 0)).astype(i32)
        gp = (jnp.cumsum(fst) - 1) % 2
        nxt = (live & (bi == 0) & (nxt_step < n_live)).astype(i32)
        spare = jnp.maximum(step - n_live, 0)
        oj = jnp.where(live, sj, spare % n_inner)
        ob = jnp.where(live, bse + bi, n_act + spare // n_inner)
        sched = (e, sj, bse + bi, oj, ob, fst, gp, nxt, ne, nj, n_live.reshape(1))
        return tuple(a.astype(i32) for a in sched), n_steps

    pos = look(bs * tm, eid) + slot
    return pos, n_act.reshape(1), n_blocks, schedule


def kernel(x_prompt, x_sample, state_ret_fwd, state_ret_bwd, c, c_ctx, w_mod, b_mod, w_in, conv_w, conv_b, conv_ln_g, conv_ln_b, ret_decay_fwd, ret_decay_bwd, ret_gn_g, w_out, ln1_g, ln1_b, w_grp, b_grp, w_exp, b_exp, w_gate, w_up, w_down, ln2_g, ln2_b):
    depth = w_mod.shape[0]
    assert depth == 1, "single-layer step"
    bp, sp, d = x_prompt.shape
    bs_, ss, _ = x_sample.shape
    tp, ts = bp * sp, bs_ * ss
    t = tp + ts
    conv_width = conv_w.shape[2]
    ret_width = ret_gn_g.shape[1]
    assert ret_width == RET_HEADS * RET_D and sp % CHUNK == 0 and ss % CHUNK == 0
    alpha = (2.0 * depth) ** 0.25

    xp = x_prompt.reshape(tp, d)
    xs = x_sample.reshape(ts, d)
    cond8 = jnp.zeros((8, d), F32).at[0].set(c_ctx).at[1:1 + bs_].set(c)
    m = _modulation(cond8, w_mod[0], b_mod[0])
    sh1, sc1, gt1, sh2, sc2, gt2 = [m[:, k * d:(k + 1) * d].reshape(8, 1, d) for k in range(6)]

    h = _ln_modulate(xp, xs, sc1, sh1, ss)
    u_glu = _glu_proj(h, w_in[0], conv_width)
    cos_t, sin_t = _rope_tables(ss, RET_D)
    qkvg = _qkvg_proj(h, w_in[0], cos_t, sin_t, 2 * conv_width, 4 * ret_width, tp)

    u = _conv_module(u_glu, conv_w[0], conv_b[0], conv_ln_g[0], conv_ln_b[0], tp, ss)

    lg = jnp.stack([jax.nn.log_sigmoid(ret_decay_fwd[0].astype(F32)),
                    jax.nn.log_sigmoid(ret_decay_bwd[0].astype(F32))])
    gn = ret_gn_g[0].reshape(1, ret_width)
    r_p, new_f, new_b = _retention(qkvg, lg, gn, sp, 0, bp)
    r_s = _retention(qkvg, lg, gn, ss, tp // ss, bs_, state_ret_fwd, state_ret_bwd)
    r = jnp.concatenate([r_p, r_s], axis=0)

    v = _out_proj(u, r, w_out[0], xp, xs, gt1, alpha, ss)

    w_router = jnp.zeros((d, ROUTER_LANES), F32).at[:, :N_GROUPS].set(w_grp[0]).at[:, N_GROUPS:N_GROUPS + N_EXPERTS].set(w_exp[0]).astype(BF16)
    b_router = jnp.zeros((1, ROUTER_LANES), F32).at[0, :N_GROUPS].set(b_grp[0]).at[0, N_GROUPS:N_GROUPS + N_EXPERTS].set(b_exp[0])
    x1, h2p, route, counts = _post_mix(v, ln1_g[0], ln1_b[0], sc2, sh2, w_router, b_router, tp, ss)

    pos, n_act, n_blocks, schedule = _dispatch_plan(route, counts, t)
    xg = _dispatch_rows(h2p, pos, n_act, n_blocks, d // 2 // 128)
    sched_up, n_up = schedule(w_gate.shape[3] // TF_MOE)
    hid = _expert_up(xg, w_gate[0], w_up[0], sched_up, n_up)
    sched_dn, n_dn = schedule(1)
    yb = _expert_down(hid, w_down[0], sched_dn, n_dn)

    out_p, out_s = _combine(yb, pos, x1, route, gt2, ln2_g[0], ln2_b[0], alpha, tp, ss)
    return (out_p.reshape(bp, sp, d), out_s.reshape(bs_, ss, d), new_f, new_b)
```

```python
import functools

import numpy as np
import jax
import jax.numpy as jnp
from jax import lax
from jax.experimental import pallas as pl
from jax.experimental.pallas import tpu as pltpu

F32 = jnp.float32
BF16 = jnp.bfloat16

LN_EPS = 1e-5
CONV_K = 31
CONV_HALO = 16
RET_HEADS = 8
RET_D = 256
CHUNK = 128
GRID_W = 64
ROPE_BASE = 10000.0
N_GROUPS = 4
EXPERTS_PER_GROUP = 8
N_EXPERTS = N_GROUPS * EXPERTS_PER_GROUP
TOP_K = 2
ROUTER_LANES = 128
VMEM_LIMIT = 56 * 1024 * 1024
TM_PROJ = 1024
TM_ROW = 256
TM_MOE = 256
TF_MOE = 512
ROW_UNROLL = 8


def _params(*sem):
    return pltpu.CompilerParams(dimension_semantics=tuple(sem), vmem_limit_bytes=VMEM_LIMIT)


def _sigmoid(x):
    return 1.0 / (1.0 + jnp.exp(-x))


def _ln_rows(x):
    mu = jnp.mean(x, axis=-1, keepdims=True)
    xc = x - mu
    var = jnp.mean(xc * xc, axis=-1, keepdims=True)
    return xc * lax.rsqrt(var + LN_EPS)


def _mod_kernel(c_ref, w_ref, b_ref, o_ref):
    c = c_ref[...]
    s = c * _sigmoid(c)
    o_ref[...] = jnp.dot(s.astype(BF16), w_ref[...].astype(BF16), preferred_element_type=F32) + b_ref[...]


def _modulation(cond8, w_mod, b_mod):
    d, n = w_mod.shape
    tn = 512
    return pl.pallas_call(
        _mod_kernel,
        grid=(n // tn,),
        in_specs=[pl.BlockSpec((8, d), lambda j: (0, 0)),
                  pl.BlockSpec((d, tn), lambda j: (0, j)),
                  pl.BlockSpec((1, tn), lambda j: (0, j))],
        out_specs=pl.BlockSpec((8, tn), lambda j: (0, j)),
        out_shape=jax.ShapeDtypeStruct((8, n), F32),
        compiler_params=_params("arbitrary"),
        name="modulation",
    )(cond8, w_mod, b_mod.reshape(1, n))


def _ln_mod_kernel(npb, xp_ref, xs_ref, sc_ref, sh_ref, o_ref):
    def body(x_ref):
        y = _ln_rows(x_ref[...])
        o_ref[...] = (y * (1.0 + sc_ref[...]) + sh_ref[...]).astype(o_ref.dtype)

    i = pl.program_id(0)
    pl.when(i < npb)(lambda: body(xp_ref))
    pl.when(i >= npb)(lambda: body(xs_ref))


def _ln_modulate(xp, xs, sc, sh, dec_seq):
    (tp, d), ts = xp.shape, xs.shape[0]
    tm = 512
    npb, nsb = tp // tm, ts // tm
    cidx = lambda i: jnp.where(i < npb, 0, 1 + (jnp.maximum(i - npb, 0) * tm) // dec_seq)
    return pl.pallas_call(
        functools.partial(_ln_mod_kernel, npb),
        grid=(npb + nsb,),
        in_specs=[pl.BlockSpec((tm, d), lambda i: (jnp.minimum(i, npb - 1), 0)),
                  pl.BlockSpec((tm, d), lambda i: (jnp.maximum(i - npb, 0), 0)),
                  pl.BlockSpec((None, 1, d), lambda i: (cidx(i), 0, 0)),
                  pl.BlockSpec((None, 1, d), lambda i: (cidx(i), 0, 0))],
        out_specs=pl.BlockSpec((tm, d), lambda i: (i, 0)),
        out_shape=jax.ShapeDtypeStruct((tp + ts, d), BF16),
        compiler_params=_params("arbitrary"),
        name="ln_modulate",
    )(xp, xs, sc, sh)


def _glu_proj_kernel(h_ref, wv_ref, wg_ref, o_ref):
    h = h_ref[...]
    a = jnp.dot(h, wv_ref[...].astype(BF16), preferred_element_type=F32)
    g = jnp.dot(h, wg_ref[...].astype(BF16), preferred_element_type=F32)
    o_ref[...] = a * _sigmoid(g)


def _glu_proj(h, w_in, conv_width):
    t, d = h.shape
    tm, tn = TM_PROJ, 256
    goff = conv_width // tn
    return pl.pallas_call(
        _glu_proj_kernel,
        grid=(t // tm, conv_width // tn),
        in_specs=[pl.BlockSpec((tm, d), lambda i, j: (i, 0)),
                  pl.BlockSpec((d, tn), lambda i, j: (0, j)),
                  pl.BlockSpec((d, tn), lambda i, j: (0, j + goff))],
        out_specs=pl.BlockSpec((tm, tn), lambda i, j: (i, j)),
        out_shape=jax.ShapeDtypeStruct((t, conv_width), F32),
        compiler_params=_params("arbitrary", "arbitrary"),
        name="glu_proj",
    )(h, w_in, w_in)


def _qkvg_proj_kernel(npb, tiles_per_part, k_scale, h_ref, w_ref, cos_ref, sin_ref, o_ref):
    i, j = pl.program_id(0), pl.program_id(1)
    tpp = tiles_per_part
    is_k = (j >= tpp) & (j < 2 * tpp)
    roped = (j < 2 * tpp) & (i >= npb)
    is_gate = j >= 3 * tpp
    scale = jnp.where(is_k, k_scale, 1.0).astype(F32)
    h = h_ref[...]
    for hd in range(o_ref.shape[1] // RET_D):
        hs = slice(hd * RET_D, (hd + 1) * RET_D)
        zz = jnp.dot(h, w_ref[:, hs].astype(BF16), preferred_element_type=F32) * scale
        for s in range(RET_D // 128):
            ts_ = slice(s * 128, (s + 1) * 128)
            zs = zz[:, ts_]
            rot = zs * cos_ref[:, ts_] + pltpu.roll(zs, 64, axis=1) * sin_ref[:, ts_]
            out = jnp.where(is_gate, zs * _sigmoid(zs), jnp.where(roped, rot, zs))
            o_ref[:, hd * RET_D + s * 128:hd * RET_D + (s + 1) * 128] = out.astype(o_ref.dtype)


def _qkvg_proj(h, w_in, cos_t, sin_t, col0, n_cols, tp):
    t, d = h.shape
    tm, tn = TM_PROJ, 512
    assert cos_t.shape == (tm, RET_D), "a projection row tile is one latent sequence"
    npb = tp // tm
    tpp = (n_cols // 4) // tn
    return pl.pallas_call(
        functools.partial(_qkvg_proj_kernel, npb, tpp, RET_D ** -0.5),
        grid=(t // tm, n_cols // tn),
        in_specs=[pl.BlockSpec((tm, d), lambda i, j: (i, 0)),
                  pl.BlockSpec((d, tn), lambda i, j: (0, j + col0 // tn)),
                  pl.BlockSpec((tm, RET_D), lambda i, j: (0, 0)),
                  pl.BlockSpec((tm, RET_D), lambda i, j: (0, 0))],
        out_specs=pl.BlockSpec((tm, tn), lambda i, j: (i, j)),
        out_shape=jax.ShapeDtypeStruct((t, n_cols), BF16),
        compiler_params=_params("arbitrary", "arbitrary"),
        name="qkvg_proj",
    )(h, w_in, cos_t, sin_t)


def _rope_tables(n_tok, width):
    rows = n_tok // GRID_W
    r_idx = np.repeat(np.arange(rows), GRID_W).astype(np.float64)
    c_idx = np.tile(np.arange(GRID_W), rows).astype(np.float64)
    dq = RET_D // 2
    inv = ROPE_BASE ** (-np.arange(dq // 2, dtype=np.float64) / (dq // 2))
    sign = np.concatenate([-np.ones(dq // 2), np.ones(dq // 2)])
    cos_h, sin_h = [], []
    for idx in (r_idx, c_idx):
        ang = idx[:, None] * inv
        cos_h.append(np.concatenate([np.cos(ang), np.cos(ang)], axis=1))
        sin_h.append(np.concatenate([np.sin(ang), np.sin(ang)], axis=1) * sign)
    cos_h, sin_h = np.concatenate(cos_h, axis=1), np.concatenate(sin_h, axis=1)
    reps = width // RET_D
    return (jnp.asarray(np.tile(cos_h, (1, reps)), F32), jnp.asarray(np.tile(sin_h, (1, reps)), F32))


def _conv_kernel(npb, tiles_per_seq, uc_ref, up_ref, un_ref, w_ref, b_ref, g_ref, bt_ref, o_ref, buf_ref, y_ref):
    r = pl.program_id(0)
    tm = uc_ref.shape[0]
    ncg = buf_ref.shape[0]
    t = lax.rem(jnp.maximum(r - npb, 0), tiles_per_seq)
    is_s = r >= npb
    has_prev = is_s & (t != 0)
    has_next = is_s & (t != tiles_per_seq - 1)
    for cg in range(ncg):
        cs = slice(cg * 128, (cg + 1) * 128)
        buf_ref[cg, 0:CONV_HALO, :] = jnp.where(has_prev, up_ref[:, cs], 0.0)
        buf_ref[cg, CONV_HALO:CONV_HALO + tm, :] = uc_ref[:, cs]
        buf_ref[cg, CONV_HALO + tm:, :] = jnp.where(has_next, un_ref[:, cs], 0.0)

    off = CONV_HALO - CONV_K // 2

    def body(cg, carry):
        acc = jnp.zeros((tm, 128), F32)
        for tap in range(CONV_K):
            acc = acc + buf_ref[cg, off + tap:off + tap + tm, :] * w_ref[cg, tap:tap + 1, :]
        y_ref[cg] = acc + b_ref[cg]
        return carry

    lax.fori_loop(0, ncg, body, 0)

    n_ch = ncg * 128
    tot = y_ref[0]
    for cg in range(1, ncg):
        tot = tot + y_ref[cg]
    mu = jnp.sum(tot, axis=1, keepdims=True) * (1.0 / n_ch)
    sq = jnp.zeros((tm, 128), F32)
    for cg in range(ncg):
        dv = y_ref[cg] - mu
        sq = sq + dv * dv
    var = jnp.sum(sq, axis=1, keepdims=True) * (1.0 / n_ch)
    rstd = lax.rsqrt(var + LN_EPS)
    for cg in range(ncg):
        cs = slice(cg * 128, (cg + 1) * 128)
        v = (y_ref[cg] - mu) * rstd * g_ref[cg] + bt_ref[cg]
        o_ref[:, cs] = (v * _sigmoid(v)).astype(o_ref.dtype)


def _conv_module(u, conv_w, conv_b, ln_g, ln_b, tp, dec_seq):
    t, c = u.shape
    tm = TM_ROW
    ncg = c // 128
    npb = tp // tm
    hb = tm // CONV_HALO
    n_halo_blocks = t // CONV_HALO
    w3 = jnp.zeros((32, c), F32).at[:CONV_K].set(conv_w).reshape(32, ncg, 128).transpose(1, 0, 2)
    vec = lambda a: a.reshape(ncg, 1, 128)
    return pl.pallas_call(
        functools.partial(_conv_kernel, npb, dec_seq // tm),
        grid=(t // tm,),
        in_specs=[pl.BlockSpec((tm, c), lambda r: (r, 0)),
                  pl.BlockSpec((CONV_HALO, c), lambda r: (jnp.maximum(r * hb - 1, 0), 0)),
                  pl.BlockSpec((CONV_HALO, c), lambda r: (jnp.minimum((r + 1) * hb, n_halo_blocks - 1), 0)),
                  pl.BlockSpec((ncg, 32, 128), lambda r: (0, 0, 0)),
                  pl.BlockSpec((ncg, 1, 128), lambda r: (0, 0, 0)),
                  pl.BlockSpec((ncg, 1, 128), lambda r: (0, 0, 0)),
                  pl.BlockSpec((ncg, 1, 128), lambda r: (0, 0, 0))],
        out_specs=pl.BlockSpec((tm, c), lambda r: (r, 0)),
        out_shape=jax.ShapeDtypeStruct((t, c), BF16),
        scratch_shapes=[pltpu.VMEM((ncg, tm + 2 * CONV_HALO, 128), F32),
                        pltpu.VMEM((ncg, tm, 128), F32)],
        compiler_params=_params("arbitrary"),
        name="conv_ln_swish",
    )(u, u, u, w3, vec(conv_b), vec(ln_g), vec(ln_b))


def _retention_kernel(nc, has_init, lg_ref, q_ref, k_ref, v_ref, g_ref, gn_ref, *rest):
    if has_init:
        s0f_ref, s0b_ref, r_ref, o_ref, sf_ref, sb_ref = rest
    else:
        r_ref, sf_ref, sb_ref, o_ref = rest
    hd = pl.program_id(1)
    lgf, lgb = lg_ref[0, hd], lg_ref[1, hd]
    c = CHUNK
    row = lax.broadcasted_iota(jnp.int32, (c, c), 0)
    col = lax.broadcasted_iota(jnp.int32, (c, c), 1)
    diff = (row - col).astype(F32)
    dec = (jnp.where(diff >= 0, jnp.exp(jnp.maximum(diff, 0.0) * lgf), 0.0)
           + jnp.where(diff <= 0, jnp.exp(jnp.maximum(-diff, 0.0) * lgb), 0.0))
    pos = lax.broadcasted_iota(jnp.int32, (c, 1), 0).astype(F32)
    xi_f = jnp.exp((pos + 1.0) * lgf)
    zeta_f = jnp.exp((c - 1.0 - pos) * lgf)
    xi_b = jnp.exp((c - pos) * lgb)
    zeta_b = jnp.exp(pos * lgb)
    gch_f = jnp.exp(jnp.full((1, RET_D), c, F32) * lgf)
    gch_b = jnp.exp(jnp.full((1, RET_D), c, F32) * lgb)

    if has_init:
        sf_ref[...] = s0f_ref[...]
        sb_ref[...] = s0b_ref[...]
    else:
        sf_ref[...] = jnp.zeros_like(sf_ref)
        sb_ref[...] = jnp.zeros_like(sb_ref)

    def chunk(ci):
        sl = slice(ci * c, (ci + 1) * c)
        return q_ref[sl, :], k_ref[sl, :], v_ref[sl, :], sl

    def state_update(s_ref, kc, vc, zeta, gch):
        kz = (kc.astype(F32) * zeta).T.astype(BF16)
        s_ref[...] = gch * s_ref[...] + jnp.dot(kz, vc, preferred_element_type=F32)

    for ci in range(nc):
        qc, kc, vc, sl = chunk(ci)
        s = lax.dot_general(qc, kc, (((1,), (1,)), ((), ())), preferred_element_type=F32)
        p = (s * dec).astype(BF16)
        o = jnp.dot(p, vc, preferred_element_type=F32)
        o = o + jnp.dot(qc, sf_ref[...].astype(BF16), preferred_element_type=F32) * xi_f
        o_ref[sl, :] = o
        state_update(sf_ref, kc, vc, zeta_f, gch_f)

    for ci in reversed(range(nc)):
        qc, kc, vc, sl = chunk(ci)
        o_ref[sl, :] = o_ref[sl, :] + jnp.dot(qc, sb_ref[...].astype(BF16), preferred_element_type=F32) * xi_b
        state_update(sb_ref, kc, vc, zeta_b, gch_b)

    y = _ln_rows(o_ref[...])
    r_ref[...] = (g_ref[...].astype(F32) * (y * gn_ref[...])).astype(r_ref.dtype)


def _retention(qkvg, lg, gn, n, row_blk0, n_seq, s0f=None, s0b=None):
    has_init = s0f is not None
    nh, dd = RET_HEADS, RET_D
    col = lambda part: (lambda b, h: (b + row_blk0, part * nh + h))
    st_spec = pl.BlockSpec((None, None, None, dd, dd), lambda b, h: (b, 0, h, 0, 0))
    in_specs = [pl.BlockSpec(memory_space=pltpu.SMEM),
                pl.BlockSpec((n, dd), col(0)), pl.BlockSpec((n, dd), col(1)),
                pl.BlockSpec((n, dd), col(2)), pl.BlockSpec((n, dd), col(3)),
                pl.BlockSpec((1, dd), lambda b, h: (0, h))]
    args = [lg, qkvg, qkvg, qkvg, qkvg, gn]
    r_shape = jax.ShapeDtypeStruct((n_seq * n, nh * dd), BF16)
    r_spec = pl.BlockSpec((n, dd), lambda b, h: (b, h))
    st_scratch = pltpu.VMEM((dd, dd), F32)
    if has_init:
        in_specs += [st_spec, st_spec]
        args += [s0f, s0b]
        out_specs, out_shape = r_spec, r_shape
        scratch = [pltpu.VMEM((n, dd), F32), st_scratch, st_scratch]
    else:
        st_shape = jax.ShapeDtypeStruct((n_seq, 1, nh, dd, dd), F32)
        out_specs, out_shape = [r_spec, st_spec, st_spec], [r_shape, st_shape, st_shape]
        scratch = [pltpu.VMEM((n, dd), F32)]
    return pl.pallas_call(
        functools.partial(_retention_kernel, n // CHUNK, has_init),
        grid=(n_seq, nh),
        in_specs=in_specs, out_specs=out_specs, out_shape=out_shape,
        scratch_shapes=scratch,
        compiler_params=_params("arbitrary", "arbitrary"),
        name="retention_latent" if has_init else "retention_context",
    )(*args)


def _out_proj_kernel(npb, alpha, u_ref, r_ref, w1_ref, w2_ref, xp_ref, xs_ref, gt_ref, o_ref):
    y = (jnp.dot(u_ref[...], w1_ref[...].astype(BF16), preferred_element_type=F32)
         + jnp.dot(r_ref[...], w2_ref[...].astype(BF16), preferred_element_type=F32))
    i = pl.program_id(0)

    @pl.when(i < npb)
    def _():
        o_ref[...] = alpha * xp_ref[...] + gt_ref[...] * y

    @pl.when(i >= npb)
    def _():
        o_ref[...] = alpha * xs_ref[...] + gt_ref[...] * y


def _out_proj(u, r, w_out, xp, xs, gt, alpha, dec_seq):
    t, kh = u.shape
    d = w_out.shape[1]
    tm, tn = TM_PROJ, 512
    npb = xp.shape[0] // tm
    cidx = lambda i: jnp.where(i < npb, 0, 1 + (jnp.maximum(i - npb, 0) * tm) // dec_seq)
    return pl.pallas_call(
        functools.partial(_out_proj_kernel, npb, alpha),
        grid=(t // tm, d // tn),
        in_specs=[pl.BlockSpec((tm, kh), lambda i, j: (i, 0)),
                  pl.BlockSpec((tm, kh), lambda i, j: (i, 0)),
                  pl.BlockSpec((kh, tn), lambda i, j: (0, j)),
                  pl.BlockSpec((kh, tn), lambda i, j: (1, j)),
                  pl.BlockSpec((tm, tn), lambda i, j: (jnp.minimum(i, npb - 1), j)),
                  pl.BlockSpec((tm, tn), lambda i, j: (jnp.maximum(i - npb, 0), j)),
                  pl.BlockSpec((None, 1, tn), lambda i, j: (cidx(i), 0, j))],
        out_specs=pl.BlockSpec((tm, tn), lambda i, j: (i, j)),
        out_shape=jax.ShapeDtypeStruct((t, d), F32),
        compiler_params=_params("arbitrary", "arbitrary"),
        name="out_proj_residual",
    )(u, r, w_out, w_out, xp, xs, gt)


def _pack_bf16_pairs(x):
    n = x.shape[1] // 2
    lo = lax.bitcast_convert_type(x[:, :n].astype(BF16).astype(F32), jnp.uint32)
    hi = lax.bitcast_convert_type(x[:, n:].astype(BF16).astype(F32), jnp.uint32)
    return hi | (lo >> 16)


def _unpack_bf16_pairs(p):
    lo = lax.bitcast_convert_type(p << 16, F32)
    hi = lax.bitcast_convert_type(p & jnp.uint32(0xFFFF0000), F32)
    return jnp.concatenate([lo, hi], axis=1)


def _store_as_slabs(ref, rows):
    m, n = rows.shape
    slab = n // 128
    for s in range(slab):
        ref[pl.ds(s, m, stride=slab), :] = rows[:, s * 128:(s + 1) * 128]


def _load_from_slabs(ref, m):
    slab = ref.shape[0] // m
    return jnp.concatenate([ref[pl.ds(s, m, stride=slab), :] for s in range(slab)], axis=1)


def _slab_copy(src, dst, src_row, dst_row, slab, sem):
    return pltpu.make_async_copy(src.at[pl.ds(pl.multiple_of(src_row * slab, slab), slab), :],
                                 dst.at[pl.ds(pl.multiple_of(dst_row * slab, slab), slab), :], sem)


def _post_mix_kernel(v_ref, g1_ref, b1_ref, sc_ref, sh_ref, wr_ref, br_ref, x1_ref, h2_ref, rt_ref, cnt_ref):
    @pl.when(pl.program_id(0) == 0)
    def _():
        cnt_ref[...] = jnp.zeros_like(cnt_ref)

    x1 = _ln_rows(v_ref[...]) * g1_ref[...] + b1_ref[...]
    x1_ref[...] = x1
    h2 = _ln_rows(x1) * (1.0 + sc_ref[...]) + sh_ref[...]
    _store_as_slabs(h2_ref, _pack_bf16_pairs(h2))
    logits = jnp.dot(h2.astype(BF16), wr_ref[...], preferred_element_type=F32) + br_ref[...]
    lane = lax.broadcasted_iota(jnp.int32, logits.shape, 1)
    big = jnp.int32(ROUTER_LANES)
    neg = jnp.float32(-jnp.inf)

    def first_lane_of_max(vals):
        m = jnp.max(vals, axis=1, keepdims=True)
        return m, jnp.min(jnp.where(vals == m, lane, big), axis=1, keepdims=True)

    lgt = jnp.where(lane < N_GROUPS, logits, neg)
    eg = jnp.exp(lgt - jnp.max(lgt, axis=1, keepdims=True))
    pg = eg / jnp.sum(eg, axis=1, keepdims=True)
    grp_prob, grp = first_lane_of_max(jnp.where(lane < N_GROUPS, pg, -1.0))
    lo = N_GROUPS + grp * EXPERTS_PER_GROUP
    in_grp = (lane >= lo) & (lane < lo + EXPERTS_PER_GROUP)
    let = jnp.where(in_grp, logits, neg)
    ee = jnp.exp(let - jnp.max(let, axis=1, keepdims=True))
    pe = jnp.where(in_grp, ee / jnp.sum(ee, axis=1, keepdims=True), -1.0)
    p1, l1 = first_lane_of_max(pe)
    p2, l2 = first_lane_of_max(jnp.where(lane == l1, -1.0, pe))
    den = p1 + p2
    gate1, gate2 = grp_prob * p1 / den, grp_prob * p2 / den
    e1, e2 = l1 - N_GROUPS, l2 - N_GROUPS
    tm = logits.shape[0]
    hot1 = (lane == e1).astype(F32)
    hot2 = (lane == e2).astype(F32)
    earlier = (lax.broadcasted_iota(jnp.int32, (tm, tm), 0) > lax.broadcasted_iota(jnp.int32, (tm, tm), 1)).astype(BF16)
    before1 = jnp.dot(earlier, hot1.astype(BF16), preferred_element_type=F32) + cnt_ref[...]
    n1 = jnp.sum(hot1, axis=0, keepdims=True)
    before2 = jnp.dot(earlier, hot2.astype(BF16), preferred_element_type=F32) + cnt_ref[...] + n1
    slot1 = jnp.sum(before1 * hot1, axis=1, keepdims=True)
    slot2 = jnp.sum(before2 * hot2, axis=1, keepdims=True)
    cnt_ref[...] = cnt_ref[...] + n1 + jnp.sum(hot2, axis=0, keepdims=True)
    vals = (e1.astype(F32), e2.astype(F32), gate1, gate2, slot1, slot2)
    out = jnp.zeros_like(logits)
    for k, val in enumerate(vals):
        out = jnp.where(lane == k, val, out)
    rt_ref[...] = out


def _post_mix(v, ln_g, ln_b, sc, sh, w_router, b_router, tp, dec_seq):
    t, d = v.shape
    tm = TM_ROW
    npb = tp // tm
    cidx = lambda i: jnp.where(i < npb, 0, 1 + (jnp.maximum(i - npb, 0) * tm) // dec_seq)
    row = pl.BlockSpec((tm, d), lambda i: (i, 0))
    vec = pl.BlockSpec((1, d), lambda i: (0, 0))
    cvec = pl.BlockSpec((None, 1, d), lambda i: (cidx(i), 0, 0))
    slab = d // 2 // 128
    return pl.pallas_call(
        _post_mix_kernel,
        grid=(t // tm,),
        in_specs=[row, vec, vec, cvec, cvec,
                  pl.BlockSpec((d, ROUTER_LANES), lambda i: (0, 0)),
                  pl.BlockSpec((1, ROUTER_LANES), lambda i: (0, 0))],
        out_specs=[row, pl.BlockSpec((tm * slab, 128), lambda i: (i, 0)),
                   pl.BlockSpec((tm, ROUTER_LANES), lambda i: (i, 0)),
                   pl.BlockSpec((1, ROUTER_LANES), lambda i: (0, 0))],
        out_shape=[jax.ShapeDtypeStruct((t, d), F32), jax.ShapeDtypeStruct((t * slab, 128), jnp.uint32),
                   jax.ShapeDtypeStruct((t, ROUTER_LANES), F32), jax.ShapeDtypeStruct((1, ROUTER_LANES), F32)],
        compiler_params=_params("arbitrary"),
        name="ln_ln_router",
    )(v, ln_g.reshape(1, d), ln_b.reshape(1, d), sc, sh, w_router, b_router)


def _on_parity(blk, fn):
    for par in range(2):
        pl.when(lax.rem(blk, 2) == par)(functools.partial(fn, blk, par))


def _dispatch_kernel(pos_ref, nact_ref, h_hbm, o_ref, tok_ref, buf_ref, sems):
    b = pl.program_id(0)
    n_act = nact_ref[0]
    tm = TM_MOE
    slab = o_ref.shape[0] // tm
    n_tok = pos_ref.shape[0] // TOP_K

    @pl.when(b == 0)
    def _():
        def clear(r, carry):
            tok_ref[r] = 0
            return carry

        def fill(tok, carry):
            for k in range(TOP_K):
                tok_ref[pos_ref[tok * TOP_K + k]] = tok
            return carry

        lax.fori_loop(0, tok_ref.shape[0], clear, 0, unroll=8)
        lax.fori_loop(0, n_tok, fill, 0, unroll=4)

    def rows(wait, blk, par):
        def body(g, carry):
            for u in range(ROW_UNROLL):
                r = g * ROW_UNROLL + u
                cp = _slab_copy(h_hbm, buf_ref.at[par], tok_ref[blk * tm + r], r, slab, sems.at[par])
                cp.wait() if wait else cp.start(priority=u % 2)
            return carry
        lax.fori_loop(0, tm // ROW_UNROLL, body, 0)

    @pl.when(b == 0)
    def _():
        rows(False, 0, 0)

    @pl.when(b + 1 < n_act)
    def _():
        _on_parity(b + 1, functools.partial(rows, False))

    @pl.when(b < n_act)
    def _():
        _on_parity(b, functools.partial(rows, True))
        o_ref[...] = buf_ref[lax.rem(b, 2)]

    @pl.when(b >= n_act)
    def _():
        o_ref[...] = jnp.zeros_like(o_ref)


def _dispatch_rows(h2p, pos, n_act, n_blocks, slab):
    tm = TM_MOE
    return pl.pallas_call(
        _dispatch_kernel,
        grid_spec=pltpu.PrefetchScalarGridSpec(
            num_scalar_prefetch=2,
            grid=(n_blocks,),
            in_specs=[pl.BlockSpec(memory_space=pl.ANY)],
            out_specs=pl.BlockSpec((tm * slab, 128), lambda b, *_: (b, 0)),
            scratch_shapes=[pltpu.SMEM((n_blocks * tm,), jnp.int32),
                            pltpu.VMEM((2, tm * slab, 128), jnp.uint32), pltpu.SemaphoreType.DMA((2,))]),
        out_shape=jax.ShapeDtypeStruct((n_blocks * tm * slab, 128), jnp.uint32),
        compiler_params=_params("arbitrary"),
        name="moe_dispatch",
    )(pos, n_act, h2p)


def _expert_up_kernel(se_ref, sj_ref, sb_ref, oj_ref, ob_ref, fst_ref, gp_ref, nxt_ref, ne_ref, nj_ref, ns_ref,
                      x_ref, wg_hbm, wu_hbm, o_ref, wg_st, wu_st, sems):
    live = pl.program_id(0) < ns_ref[0]
    _stream_group_weights(se_ref, sj_ref, fst_ref, gp_ref, nxt_ref, ne_ref, nj_ref, live,
                          ((wg_hbm, wg_st), (wu_hbm, wu_st)), sems)

    @pl.when(live)
    def _():
        x = _unpack_bf16_pairs(_load_from_slabs(x_ref, TM_MOE))
        slot = gp_ref[pl.program_id(0)]
        a = jnp.dot(x, wg_st[slot], preferred_element_type=F32)
        u = jnp.dot(x, wu_st[slot], preferred_element_type=F32)
        o_ref[...] = ((a * _sigmoid(a)) * u).astype(o_ref.dtype)

    @pl.when(jnp.logical_not(live))
    def _():
        o_ref[...] = jnp.zeros_like(o_ref)


def _stream_group_weights(se_ref, sj_ref, fst_ref, gp_ref, nxt_ref, ne_ref, nj_ref, live, weights, sems):
    s = pl.program_id(0)

    def copies(e, j, slot):
        out = []
        for k, (w_hbm, w_st) in enumerate(weights):
            tn = w_st.shape[2]
            src = w_hbm.at[e, :, pl.ds(pl.multiple_of(j * tn, tn), tn)]
            out.append(pltpu.make_async_copy(src, w_st.at[slot], sems.at[k, slot]))
        return out

    def first_step(slot):
        cur = copies(se_ref[s], sj_ref[s], slot)

        @pl.when(s == 0)
        def _():
            for cp in cur:
                cp.start()

        @pl.when(nxt_ref[s] == 1)
        def _():
            for cp in copies(ne_ref[s], nj_ref[s], 1 - slot):
                cp.start()

        for cp in cur:
            cp.wait()

    for slot in range(2):
        pl.when(live & (fst_ref[s] == 1) & (gp_ref[s] == slot))(functools.partial(first_step, slot))


_N_SCHED = 11


def _sched_map(fn):
    return lambda s, *refs: fn(s, *refs[:_N_SCHED])


def _expert_up(xs, w_gate, w_up, sched, n_steps):
    d, de = w_gate.shape[1], w_gate.shape[2]
    tm, tf = TM_MOE, TF_MOE
    slab = d // 2 // 128
    r = xs.shape[0] // slab
    return pl.pallas_call(
        _expert_up_kernel,
        grid_spec=pltpu.PrefetchScalarGridSpec(
            num_scalar_prefetch=_N_SCHED,
            grid=(n_steps,),
            in_specs=[pl.BlockSpec((tm * slab, 128), _sched_map(lambda s, se, sj, sb, *_: (sb[s], 0))),
                      pl.BlockSpec(memory_space=pl.ANY), pl.BlockSpec(memory_space=pl.ANY)],
            out_specs=pl.BlockSpec((tm, tf), _sched_map(lambda s, se, sj, sb, oj, ob, *_: (ob[s], oj[s]))),
            scratch_shapes=[pltpu.VMEM((2, d, tf), F32), pltpu.VMEM((2, d, tf), F32),
                            pltpu.SemaphoreType.DMA((2, 2))]),
        out_shape=jax.ShapeDtypeStruct((r, de), BF16),
        compiler_params=_params("arbitrary"),
        name="moe_gate_up",
    )(*sched, xs, w_gate, w_up)


def _expert_down_kernel(se_ref, sj_ref, sb_ref, oj_ref, ob_ref, fst_ref, gp_ref, nxt_ref, ne_ref, nj_ref, ns_ref,
                        h_ref, wd_hbm, o_ref, wd_st, sems):
    live = pl.program_id(0) < ns_ref[0]
    _stream_group_weights(se_ref, sj_ref, fst_ref, gp_ref, nxt_ref, ne_ref, nj_ref, live, ((wd_hbm, wd_st),), sems)

    @pl.when(live)
    def _():
        y = jnp.dot(h_ref[...].astype(F32), wd_st[gp_ref[pl.program_id(0)]], preferred_element_type=F32)
        _store_as_slabs(o_ref, _pack_bf16_pairs(y))

    @pl.when(jnp.logical_not(live))
    def _():
        o_ref[...] = jnp.zeros_like(o_ref)


def _expert_down(hid, w_down, sched, n_steps):
    r, de = hid.shape
    d = w_down.shape[2]
    tm = TM_MOE
    slab = d // 2 // 128
    return pl.pallas_call(
        _expert_down_kernel,
        grid_spec=pltpu.PrefetchScalarGridSpec(
            num_scalar_prefetch=_N_SCHED,
            grid=(n_steps,),
            in_specs=[pl.BlockSpec((tm, de), _sched_map(lambda s, se, sj, sb, *_: (sb[s], 0))),
                      pl.BlockSpec(memory_space=pl.ANY)],
            out_specs=pl.BlockSpec((tm * slab, 128), _sched_map(lambda s, se, sj, sb, oj, ob, *_: (ob[s], 0))),
            scratch_shapes=[pltpu.VMEM((2, de, d), F32), pltpu.SemaphoreType.DMA((1, 2))]),
        out_shape=jax.ShapeDtypeStruct((r * slab, 128), jnp.uint32),
        compiler_params=_params("arbitrary"),
        name="moe_down",
    )(*sched, hid, w_down)


def _combine_kernel(npb, alpha, pos_ref, y_hbm, x1_ref, rt_ref, gt_ref, g_ref, b_ref,
                    op_ref, os_ref, buf_ref, sems):
    i = pl.program_id(0)
    n = pl.num_programs(0)
    tm = x1_ref.shape[0]
    slab = buf_ref.shape[2] // tm

    def rows(wait, blk, par):
        def body(g, carry):
            for u in range(ROW_UNROLL):
                r = g * ROW_UNROLL + u
                for k in range(TOP_K):
                    cp = _slab_copy(y_hbm, buf_ref.at[par, k], pos_ref[(blk * tm + r) * TOP_K + k], r, slab,
                                    sems.at[par])
                    cp.wait() if wait else cp.start(priority=k % 2)
            return carry
        lax.fori_loop(0, tm // ROW_UNROLL, body, 0)

    @pl.when(i == 0)
    def _():
        rows(False, 0, 0)

    @pl.when(i + 1 < n)
    def _():
        _on_parity(i + 1, functools.partial(rows, False))

    _on_parity(i, functools.partial(rows, True))
    rt = rt_ref[...]
    par = lax.rem(i, 2)
    expert_rows = lambda k: _unpack_bf16_pairs(_load_from_slabs(buf_ref.at[par, k], tm))
    f = rt[:, 2:3] * expert_rows(0) + rt[:, 3:4] * expert_rows(1)
    out = _ln_rows(alpha * x1_ref[...] + gt_ref[...] * f) * g_ref[...] + b_ref[...]

    @pl.when(i < npb)
    def _():
        op_ref[...] = out

    @pl.when(i >= npb)
    def _():
        os_ref[...] = out


def _combine(yb, pos, x1, route, gt, ln_g, ln_b, alpha, tp, dec_seq):
    t, d = x1.shape
    tm = TM_ROW
    npb = tp // tm
    cidx = lambda i: jnp.where(i < npb, 0, 1 + (jnp.maximum(i - npb, 0) * tm) // dec_seq)
    vec = pl.BlockSpec((1, d), lambda i, *_: (0, 0))
    return pl.pallas_call(
        functools.partial(_combine_kernel, npb, alpha),
        grid_spec=pltpu.PrefetchScalarGridSpec(
            num_scalar_prefetch=1,
            grid=(t // tm,),
            in_specs=[pl.BlockSpec(memory_space=pl.ANY),
                      pl.BlockSpec((tm, d), lambda i, *_: (i, 0)),
                      pl.BlockSpec((tm, ROUTER_LANES), lambda i, *_: (i, 0)),
                      pl.BlockSpec((None, 1, d), lambda i, *_: (cidx(i), 0, 0)),
                      vec, vec],
            out_specs=[pl.BlockSpec((tm, d), lambda i, *_: (jnp.minimum(i, npb - 1), 0)),
                       pl.BlockSpec((tm, d), lambda i, *_: (jnp.maximum(i - npb, 0), 0))],
            scratch_shapes=[pltpu.VMEM((2, TOP_K, tm * (d // 2 // 128), 128), jnp.uint32),
                            pltpu.SemaphoreType.DMA((2,))]),
        out_shape=[jax.ShapeDtypeStruct((tp, d), F32), jax.ShapeDtypeStruct((t - tp, d), F32)],
        compiler_params=_params("arbitrary"),
        name="moe_combine_ln",
    )(pos, yb, x1, route, gt, ln_g.reshape(1, d), ln_b.reshape(1, d))


def _dispatch_plan(route, counts_f, n_tok):
    tm = TM_MOE
    n_blocks = n_tok * TOP_K // tm + N_EXPERTS
    i32 = jnp.int32
    eid = route[:, 0:TOP_K].astype(i32).reshape(-1)
    slot = route[:, 4:4 + TOP_K].astype(i32).reshape(-1)
    counts = counts_f[0, :N_EXPERTS].astype(i32)
    nb = (counts + tm - 1) // tm
    nb_end = jnp.cumsum(nb)
    bs = nb_end - nb
    n_act = nb_end[-1]
    experts = jnp.arange(N_EXPERTS, dtype=i32)
    look = lambda table, e: jnp.sum(jnp.where(e[:, None] == experts[None, :], table[None, :], 0), axis=1)

    def schedule(n_inner):
        n_steps = n_inner * n_blocks
        n_live = n_inner * n_act

        def decode(step):
            s = jnp.minimum(step, n_live - 1)
            e = jnp.minimum(jnp.sum((s[:, None] >= n_inner * nb_end[None, :]).astype(i32), axis=1), N_EXPERTS - 1)
            nbe = jnp.maximum(look(nb, e), 1)
            loc = s - n_inner * look(bs, e)
            return e, loc // nbe, loc % nbe, nbe, look(bs, e)

        step = jnp.arange(n_steps, dtype=i32)
        e, sj, bi, nbe, bse = decode(step)
        live = step < n_live
        nxt_step = jnp.minimum(step, n_live - 1) + nbe - bi
        ne, nj, _, _, _ = decode(nxt_step)
        fst = (live & (bi == 0)).astype(i32)
        gp = (jnp.cumsum(fst) - 1) % 2
        nxt = (live & (bi == 0) & (nxt_step < n_live)).astype(i32)
        spare = jnp.maximum(step - n_live, 0)
        oj = jnp.where(live, sj, spare % n_inner)
        ob = jnp.where(live, bse + bi, n_act + spare // n_inner)
        sched = (e, sj, bse + bi, oj, ob, fst, gp, nxt, ne, nj, n_live.reshape(1))
        return tuple(a.astype(i32) for a in sched), n_steps

    pos = look(bs * tm, eid) + slot
    return pos, n_act.reshape(1), n_blocks, schedule


def kernel(x_prompt, x_sample, state_ret_fwd, state_ret_bwd, c, c_ctx, w_mod, b_mod, w_in, conv_w, conv_b, conv_ln_g, conv_ln_b, ret_decay_fwd, ret_decay_bwd, ret_gn_g, w_out, ln1_g, ln1_b, w_grp, b_grp, w_exp, b_exp, w_gate, w_up, w_down, ln2_g, ln2_b):
    depth = w_mod.shape[0]
    assert depth == 1, "single-layer step"
    bp, sp, d = x_prompt.shape
    bs_, ss, _ = x_sample.shape
    tp, ts = bp * sp, bs_ * ss
    t = tp + ts
    conv_width = conv_w.shape[2]
    ret_width = ret_gn_g.shape[1]
    assert ret_width == RET_HEADS * RET_D and sp % CHUNK == 0 and ss % CHUNK == 0
    alpha = (2.0 * depth) ** 0.25

    xp = x_prompt.reshape(tp, d)
    xs = x_sample.reshape(ts, d)
    cond8 = jnp.zeros((8, d), F32).at[0].set(c_ctx).at[1:1 + bs_].set(c)
    m = _modulation(cond8, w_mod[0], b_mod[0])
    sh1, sc1, gt1, sh2, sc2, gt2 = [m[:, k * d:(k + 1) * d].reshape(8, 1, d) for k in range(6)]

    h = _ln_modulate(xp, xs, sc1, sh1, ss)
    u_glu = _glu_proj(h, w_in[0], conv_width)
    cos_t, sin_t = _rope_tables(ss, RET_D)
    qkvg = _qkvg_proj(h, w_in[0], cos_t, sin_t, 2 * conv_width, 4 * ret_width, tp)

    u = _conv_module(u_glu, conv_w[0], conv_b[0], conv_ln_g[0], conv_ln_b[0], tp, ss)

    lg = jnp.stack([jax.nn.log_sigmoid(ret_decay_fwd[0].astype(F32)),
                    jax.nn.log_sigmoid(ret_decay_bwd[0].astype(F32))])
    gn = ret_gn_g[0].reshape(1, ret_width)
    r_p, new_f, new_b = _retention(qkvg, lg, gn, sp, 0, bp)
    r_s = _retention(qkvg, lg, gn, ss, tp // ss, bs_, state_ret_fwd, state_ret_bwd)
    r = jnp.concatenate([r_p, r_s], axis=0)

    v = _out_proj(u, r, w_out[0], xp, xs, gt1, alpha, ss)

    w_router = jnp.zeros((d, ROUTER_LANES), F32).at[:, :N_GROUPS].set(w_grp[0]).at[:, N_GROUPS:N_GROUPS + N_EXPERTS].set(w_exp[0]).astype(BF16)
    b_router = jnp.zeros((1, ROUTER_LANES), F32).at[0, :N_GROUPS].set(b_grp[0]).at[0, N_GROUPS:N_GROUPS + N_EXPERTS].set(b_exp[0])
    x1, h2p, route, counts = _post_mix(v, ln1_g[0], ln1_b[0], sc2, sh2, w_router, b_router, tp, ss)

    pos, n_act, n_blocks, schedule = _dispatch_plan(route, counts, t)
    xg = _dispatch_rows(h2p, pos, n_act, n_blocks, d // 2 // 128)
    sched_up, n_up = schedule(w_gate.shape[3] // TF_MOE)
    hid = _expert_up(xg, w_gate[0], w_up[0], sched_up, n_up)
    sched_dn, n_dn = schedule(1)
    yb = _expert_down(hid, w_down[0], sched_dn, n_dn)

    out_p, out_s = _combine(yb, pos, x1, route, gt2, ln2_g[0], ln2_b[0], alpha, tp, ss)
    return (out_p.reshape(bp, sp, d), out_s.reshape(bs_, ss, d), new_f, new_b)
```

```python
import functools

import numpy as np
import jax
import jax.numpy as jnp
from jax import lax
from jax.experimental import pallas as pl
from jax.experimental.pallas import tpu as pltpu

F32 = jnp.float32
BF16 = jnp.bfloat16

LN_EPS = 1e-5
CONV_K = 31
CONV_HALO = 16
RET_HEADS = 8
RET_D = 256
CHUNK = 128
GRID_W = 64
ROPE_BASE = 10000.0
N_GROUPS = 4
EXPERTS_PER_GROUP = 8
N_EXPERTS = N_GROUPS * EXPERTS_PER_GROUP
TOP_K = 2
ROUTER_LANES = 128
VMEM_LIMIT = 56 * 1024 * 1024
TM_PROJ = 1024
TM_ROW = 256
TM_MOE = 256
TF_MOE = 512
ROW_UNROLL = 8


def _params(*sem):
    return pltpu.CompilerParams(dimension_semantics=tuple(sem), vmem_limit_bytes=VMEM_LIMIT)


def _sigmoid(x):
    return 1.0 / (1.0 + jnp.exp(-x))


def _ln_rows(x):
    mu = jnp.mean(x, axis=-1, keepdims=True)
    xc = x - mu
    var = jnp.mean(xc * xc, axis=-1, keepdims=True)
    return xc * lax.rsqrt(var + LN_EPS)


def _mod_kernel(c_ref, w_ref, b_ref, o_ref):
    c = c_ref[...]
    s = c * _sigmoid(c)
    o_ref[...] = jnp.dot(s.astype(BF16), w_ref[...].astype(BF16), preferred_element_type=F32) + b_ref[...]


def _modulation(cond8, w_mod, b_mod):
    d, n = w_mod.shape
    tn = 512
    return pl.pallas_call(
        _mod_kernel,
        grid=(n // tn,),
        in_specs=[pl.BlockSpec((8, d), lambda j: (0, 0)),
                  pl.BlockSpec((d, tn), lambda j: (0, j)),
                  pl.BlockSpec((1, tn), lambda j: (0, j))],
        out_specs=pl.BlockSpec((8, tn), lambda j: (0, j)),
        out_shape=jax.ShapeDtypeStruct((8, n), F32),
        compiler_params=_params("arbitrary"),
        name="modulation",
    )(cond8, w_mod, b_mod.reshape(1, n))


def _ln_mod_kernel(npb, xp_ref, xs_ref, sc_ref, sh_ref, o_ref):
    def body(x_ref):
        y = _ln_rows(x_ref[...])
        o_ref[...] = (y * (1.0 + sc_ref[...]) + sh_ref[...]).astype(o_ref.dtype)

    i = pl.program_id(0)
    pl.when(i < npb)(lambda: body(xp_ref))
    pl.when(i >= npb)(lambda: body(xs_ref))


def _ln_modulate(xp, xs, sc, sh, dec_seq):
    (tp, d), ts = xp.shape, xs.shape[0]
    tm = 512
    npb, nsb = tp // tm, ts // tm
    cidx = lambda i: jnp.where(i < npb, 0, 1 + (jnp.maximum(i - npb, 0) * tm) // dec_seq)
    return pl.pallas_call(
        functools.partial(_ln_mod_kernel, npb),
        grid=(npb + nsb,),
        in_specs=[pl.BlockSpec((tm, d), lambda i: (jnp.minimum(i, npb - 1), 0)),
                  pl.BlockSpec((tm, d), lambda i: (jnp.maximum(i - npb, 0), 0)),
                  pl.BlockSpec((None, 1, d), lambda i: (cidx(i), 0, 0)),
                  pl.BlockSpec((None, 1, d), lambda i: (cidx(i), 0, 0))],
        out_specs=pl.BlockSpec((tm, d), lambda i: (i, 0)),
        out_shape=jax.ShapeDtypeStruct((tp + ts, d), BF16),
        compiler_params=_params("arbitrary"),
        name="ln_modulate",
    )(xp, xs, sc, sh)


def _glu_proj_kernel(h_ref, wv_ref, wg_ref, o_ref):
    h = h_ref[...]
    a = jnp.dot(h, wv_ref[...].astype(BF16), preferred_element_type=F32)
    g = jnp.dot(h, wg_ref[...].astype(BF16), preferred_element_type=F32)
    o_ref[...] = a * _sigmoid(g)


def _glu_proj(h, w_in, conv_width):
    t, d = h.shape
    tm, tn = TM_PROJ, 256
    goff = conv_width // tn
    return pl.pallas_call(
        _glu_proj_kernel,
        grid=(t // tm, conv_width // tn),
        in_specs=[pl.BlockSpec((tm, d), lambda i, j: (i, 0)),
                  pl.BlockSpec((d, tn), lambda i, j: (0, j)),
                  pl.BlockSpec((d, tn), lambda i, j: (0, j + goff))],
        out_specs=pl.BlockSpec((tm, tn), lambda i, j: (i, j)),
        out_shape=jax.ShapeDtypeStruct((t, conv_width), F32),
        compiler_params=_params("arbitrary", "arbitrary"),
        name="glu_proj",
    )(h, w_in, w_in)


def _qkvg_proj_kernel(npb, tiles_per_part, k_scale, h_ref, w_ref, cos_ref, sin_ref, o_ref):
    i, j = pl.program_id(0), pl.program_id(1)
    z = jnp.dot(h_ref[...], w_ref[...].astype(BF16), preferred_element_type=F32)
    tpp = tiles_per_part

    @pl.when(j >= 3 * tpp)
    def _():
        o_ref[...] = (z * _sigmoid(z)).astype(o_ref.dtype)

    @pl.when((j >= 2 * tpp) & (j < 3 * tpp))
    def _():
        o_ref[...] = z.astype(o_ref.dtype)

    @pl.when(j < 2 * tpp)
    def _():
        zz = z * jnp.where(j >= tpp, k_scale, 1.0).astype(F32)

        @pl.when(i < npb)
        def _():
            o_ref[...] = zz.astype(o_ref.dtype)

        @pl.when(i >= npb)
        def _():
            for s in range(zz.shape[1] // 128):
                cs = slice(s * 128, (s + 1) * 128)
                ts_ = slice((s * 128) % RET_D, (s * 128) % RET_D + 128)
                zs = zz[:, cs]
                o_ref[:, cs] = (zs * cos_ref[:, ts_] + pltpu.roll(zs, 64, axis=1) * sin_ref[:, ts_]).astype(o_ref.dtype)


def _qkvg_proj(h, w_in, cos_t, sin_t, col0, n_cols, tp):
    t, d = h.shape
    tm, tn = TM_PROJ, 512
    assert cos_t.shape == (tm, RET_D), "a projection row tile is one latent sequence"
    npb = tp // tm
    tpp = (n_cols // 4) // tn
    return pl.pallas_call(
        functools.partial(_qkvg_proj_kernel, npb, tpp, RET_D ** -0.5),
        grid=(t // tm, n_cols // tn),
        in_specs=[pl.BlockSpec((tm, d), lambda i, j: (i, 0)),
                  pl.BlockSpec((d, tn), lambda i, j: (0, j + col0 // tn)),
                  pl.BlockSpec((tm, RET_D), lambda i, j: (0, 0)),
                  pl.BlockSpec((tm, RET_D), lambda i, j: (0, 0))],
        out_specs=pl.BlockSpec((tm, tn), lambda i, j: (i, j)),
        out_shape=jax.ShapeDtypeStruct((t, n_cols), BF16),
        compiler_params=_params("arbitrary", "arbitrary"),
        name="qkvg_proj",
    )(h, w_in, cos_t, sin_t)


def _rope_tables(n_tok, width):
    rows = n_tok // GRID_W
    r_idx = np.repeat(np.arange(rows), GRID_W).astype(np.float64)
    c_idx = np.tile(np.arange(GRID_W), rows).astype(np.float64)
    dq = RET_D // 2
    inv = ROPE_BASE ** (-np.arange(dq // 2, dtype=np.float64) / (dq // 2))
    sign = np.concatenate([-np.ones(dq // 2), np.ones(dq // 2)])
    cos_h, sin_h = [], []
    for idx in (r_idx, c_idx):
        ang = idx[:, None] * inv
        cos_h.append(np.concatenate([np.cos(ang), np.cos(ang)], axis=1))
        sin_h.append(np.concatenate([np.sin(ang), np.sin(ang)], axis=1) * sign)
    cos_h, sin_h = np.concatenate(cos_h, axis=1), np.concatenate(sin_h, axis=1)
    reps = width // RET_D
    return (jnp.asarray(np.tile(cos_h, (1, reps)), F32), jnp.asarray(np.tile(sin_h, (1, reps)), F32))


def _conv_kernel(npb, tiles_per_seq, uc_ref, up_ref, un_ref, w_ref, b_ref, g_ref, bt_ref, o_ref, buf_ref, y_ref):
    r = pl.program_id(0)
    tm = uc_ref.shape[0]
    ncg = buf_ref.shape[0]
    t = lax.rem(jnp.maximum(r - npb, 0), tiles_per_seq)
    is_s = r >= npb
    has_prev = is_s & (t != 0)
    has_next = is_s & (t != tiles_per_seq - 1)
    for cg in range(ncg):
        cs = slice(cg * 128, (cg + 1) * 128)
        buf_ref[cg, 0:CONV_HALO, :] = jnp.where(has_prev, up_ref[:, cs], 0.0)
        buf_ref[cg, CONV_HALO:CONV_HALO + tm, :] = uc_ref[:, cs]
        buf_ref[cg, CONV_HALO + tm:, :] = jnp.where(has_next, un_ref[:, cs], 0.0)

    off = CONV_HALO - CONV_K // 2

    def body(cg, carry):
        acc = jnp.zeros((tm, 128), F32)
        for tap in range(CONV_K):
            acc = acc + buf_ref[cg, off + tap:off + tap + tm, :] * w_ref[cg, tap:tap + 1, :]
        y_ref[cg] = acc + b_ref[cg]
        return carry

    lax.fori_loop(0, ncg, body, 0)

    n_ch = ncg * 128
    tot = y_ref[0]
    for cg in range(1, ncg):
        tot = tot + y_ref[cg]
    mu = jnp.sum(tot, axis=1, keepdims=True) * (1.0 / n_ch)
    sq = jnp.zeros((tm, 128), F32)
    for cg in range(ncg):
        dv = y_ref[cg] - mu
        sq = sq + dv * dv
    var = jnp.sum(sq, axis=1, keepdims=True) * (1.0 / n_ch)
    rstd = lax.rsqrt(var + LN_EPS)
    for cg in range(ncg):
        cs = slice(cg * 128, (cg + 1) * 128)
        v = (y_ref[cg] - mu) * rstd * g_ref[cg] + bt_ref[cg]
        o_ref[:, cs] = (v * _sigmoid(v)).astype(o_ref.dtype)


def _conv_module(u, conv_w, conv_b, ln_g, ln_b, tp, dec_seq):
    t, c = u.shape
    tm = TM_ROW
    ncg = c // 128
    npb = tp // tm
    hb = tm // CONV_HALO
    n_halo_blocks = t // CONV_HALO
    w3 = jnp.zeros((32, c), F32).at[:CONV_K].set(conv_w).reshape(32, ncg, 128).transpose(1, 0, 2)
    vec = lambda a: a.reshape(ncg, 1, 128)
    return pl.pallas_call(
        functools.partial(_conv_kernel, npb, dec_seq // tm),
        grid=(t // tm,),
        in_specs=[pl.BlockSpec((tm, c), lambda r: (r, 0)),
                  pl.BlockSpec((CONV_HALO, c), lambda r: (jnp.maximum(r * hb - 1, 0), 0)),
                  pl.BlockSpec((CONV_HALO, c), lambda r: (jnp.minimum((r + 1) * hb, n_halo_blocks - 1), 0)),
                  pl.BlockSpec((ncg, 32, 128), lambda r: (0, 0, 0)),
                  pl.BlockSpec((ncg, 1, 128), lambda r: (0, 0, 0)),
                  pl.BlockSpec((ncg, 1, 128), lambda r: (0, 0, 0)),
                  pl.BlockSpec((ncg, 1, 128), lambda r: (0, 0, 0))],
        out_specs=pl.BlockSpec((tm, c), lambda r: (r, 0)),
        out_shape=jax.ShapeDtypeStruct((t, c), BF16),
        scratch_shapes=[pltpu.VMEM((ncg, tm + 2 * CONV_HALO, 128), F32),
                        pltpu.VMEM((ncg, tm, 128), F32)],
        compiler_params=_params("arbitrary"),
        name="conv_ln_swish",
    )(u, u, u, w3, vec(conv_b), vec(ln_g), vec(ln_b))


def _retention_kernel(nc, has_init, lg_ref, q_ref, k_ref, v_ref, g_ref, gn_ref, *rest):
    if has_init:
        s0f_ref, s0b_ref, r_ref, o_ref, sf_ref, sb_ref = rest
    else:
        r_ref, sf_ref, sb_ref, o_ref = rest
    hd = pl.program_id(1)
    lgf, lgb = lg_ref[0, hd], lg_ref[1, hd]
    c = CHUNK
    row = lax.broadcasted_iota(jnp.int32, (c, c), 0)
    col = lax.broadcasted_iota(jnp.int32, (c, c), 1)
    diff = (row - col).astype(F32)
    dec = (jnp.where(diff >= 0, jnp.exp(jnp.maximum(diff, 0.0) * lgf), 0.0)
           + jnp.where(diff <= 0, jnp.exp(jnp.maximum(-diff, 0.0) * lgb), 0.0))
    pos = lax.broadcasted_iota(jnp.int32, (c, 1), 0).astype(F32)
    xi_f = jnp.exp((pos + 1.0) * lgf)
    zeta_f = jnp.exp((c - 1.0 - pos) * lgf)
    xi_b = jnp.exp((c - pos) * lgb)
    zeta_b = jnp.exp(pos * lgb)
    gch_f = jnp.exp(jnp.full((1, RET_D), c, F32) * lgf)
    gch_b = jnp.exp(jnp.full((1, RET_D), c, F32) * lgb)

    if has_init:
        sf_ref[...] = s0f_ref[...]
        sb_ref[...] = s0b_ref[...]
    else:
        sf_ref[...] = jnp.zeros_like(sf_ref)
        sb_ref[...] = jnp.zeros_like(sb_ref)

    def chunk(ci):
        sl = slice(ci * c, (ci + 1) * c)
        return q_ref[sl, :], k_ref[sl, :], v_ref[sl, :], sl

    def state_update(s_ref, kc, vc, zeta, gch):
        kz = (kc.astype(F32) * zeta).T.astype(BF16)
        s_ref[...] = gch * s_ref[...] + jnp.dot(kz, vc, preferred_element_type=F32)

    for ci in range(nc):
        qc, kc, vc, sl = chunk(ci)
        s = lax.dot_general(qc, kc, (((1,), (1,)), ((), ())), preferred_element_type=F32)
        p = (s * dec).astype(BF16)
        o = jnp.dot(p, vc, preferred_element_type=F32)
        o = o + jnp.dot(qc, sf_ref[...].astype(BF16), preferred_element_type=F32) * xi_f
        o_ref[sl, :] = o
        state_update(sf_ref, kc, vc, zeta_f, gch_f)

    for ci in reversed(range(nc)):
        qc, kc, vc, sl = chunk(ci)
        o_ref[sl, :] = o_ref[sl, :] + jnp.dot(qc, sb_ref[...].astype(BF16), preferred_element_type=F32) * xi_b
        state_update(sb_ref, kc, vc, zeta_b, gch_b)

    y = _ln_rows(o_ref[...])
    r_ref[...] = (g_ref[...].astype(F32) * (y * gn_ref[...])).astype(r_ref.dtype)


def _retention(qkvg, lg, gn, n, row_blk0, n_seq, s0f=None, s0b=None):
    has_init = s0f is not None
    nh, dd = RET_HEADS, RET_D
    col = lambda part: (lambda b, h: (b + row_blk0, part * nh + h))
    st_spec = pl.BlockSpec((None, None, None, dd, dd), lambda b, h: (b, 0, h, 0, 0))
    in_specs = [pl.BlockSpec(memory_space=pltpu.SMEM),
                pl.BlockSpec((n, dd), col(0)), pl.BlockSpec((n, dd), col(1)),
                pl.BlockSpec((n, dd), col(2)), pl.BlockSpec((n, dd), col(3)),
                pl.BlockSpec((1, dd), lambda b, h: (0, h))]
    args = [lg, qkvg, qkvg, qkvg, qkvg, gn]
    r_shape = jax.ShapeDtypeStruct((n_seq * n, nh * dd), BF16)
    r_spec = pl.BlockSpec((n, dd), lambda b, h: (b, h))
    st_scratch = pltpu.VMEM((dd, dd), F32)
    if has_init:
        in_specs += [st_spec, st_spec]
        args += [s0f, s0b]
        out_specs, out_shape = r_spec, r_shape
        scratch = [pltpu.VMEM((n, dd), F32), st_scratch, st_scratch]
    else:
        st_shape = jax.ShapeDtypeStruct((n_seq, 1, nh, dd, dd), F32)
        out_specs, out_shape = [r_spec, st_spec, st_spec], [r_shape, st_shape, st_shape]
        scratch = [pltpu.VMEM((n, dd), F32)]
    return pl.pallas_call(
        functools.partial(_retention_kernel, n // CHUNK, has_init),
        grid=(n_seq, nh),
        in_specs=in_specs, out_specs=out_specs, out_shape=out_shape,
        scratch_shapes=scratch,
        compiler_params=_params("arbitrary", "arbitrary"),
        name="retention_latent" if has_init else "retention_context",
    )(*args)


def _out_proj_kernel(npb, alpha, u_ref, r_ref, w1_ref, w2_ref, xp_ref, xs_ref, gt_ref, o_ref):
    y = (jnp.dot(u_ref[...], w1_ref[...].astype(BF16), preferred_element_type=F32)
         + jnp.dot(r_ref[...], w2_ref[...].astype(BF16), preferred_element_type=F32))
    i = pl.program_id(0)

    @pl.when(i < npb)
    def _():
        o_ref[...] = alpha * xp_ref[...] + gt_ref[...] * y

    @pl.when(i >= npb)
    def _():
        o_ref[...] = alpha * xs_ref[...] + gt_ref[...] * y


def _out_proj(u, r, w_out, xp, xs, gt, alpha, dec_seq):
    t, kh = u.shape
    d = w_out.shape[1]
    tm, tn = TM_PROJ, 512
    npb = xp.shape[0] // tm
    cidx = lambda i: jnp.where(i < npb, 0, 1 + (jnp.maximum(i - npb, 0) * tm) // dec_seq)
    return pl.pallas_call(
        functools.partial(_out_proj_kernel, npb, alpha),
        grid=(t // tm, d // tn),
        in_specs=[pl.BlockSpec((tm, kh), lambda i, j: (i, 0)),
                  pl.BlockSpec((tm, kh), lambda i, j: (i, 0)),
                  pl.BlockSpec((kh, tn), lambda i, j: (0, j)),
                  pl.BlockSpec((kh, tn), lambda i, j: (1, j)),
                  pl.BlockSpec((tm, tn), lambda i, j: (jnp.minimum(i, npb - 1), j)),
                  pl.BlockSpec((tm, tn), lambda i, j: (jnp.maximum(i - npb, 0), j)),
                  pl.BlockSpec((None, 1, tn), lambda i, j: (cidx(i), 0, j))],
        out_specs=pl.BlockSpec((tm, tn), lambda i, j: (i, j)),
        out_shape=jax.ShapeDtypeStruct((t, d), F32),
        compiler_params=_params("arbitrary", "arbitrary"),
        name="out_proj_residual",
    )(u, r, w_out, w_out, xp, xs, gt)


def _pack_bf16_pairs(x):
    n = x.shape[1] // 2
    lo = lax.bitcast_convert_type(x[:, :n].astype(BF16).astype(F32), jnp.uint32)
    hi = lax.bitcast_convert_type(x[:, n:].astype(BF16).astype(F32), jnp.uint32)
    return hi | (lo >> 16)


def _unpack_bf16_pairs(p):
    lo = lax.bitcast_convert_type(p << 16, F32)
    hi = lax.bitcast_convert_type(p & jnp.uint32(0xFFFF0000), F32)
    return jnp.concatenate([lo, hi], axis=1)


def _store_as_slabs(ref, rows):
    m, n = rows.shape
    slab = n // 128
    for s in range(slab):
        ref[pl.ds(s, m, stride=slab), :] = rows[:, s * 128:(s + 1) * 128]


def _load_from_slabs(ref, m):
    slab = ref.shape[0] // m
    return jnp.concatenate([ref[pl.ds(s, m, stride=slab), :] for s in range(slab)], axis=1)


def _slab_copy(src, dst, src_row, dst_row, slab, sem):
    return pltpu.make_async_copy(src.at[pl.ds(pl.multiple_of(src_row * slab, slab), slab), :],
                                 dst.at[pl.ds(pl.multiple_of(dst_row * slab, slab), slab), :], sem)


def _post_mix_kernel(v_ref, g1_ref, b1_ref, sc_ref, sh_ref, wr_ref, br_ref, x1_ref, h2_ref, rt_ref, cnt_ref, tab_ref):
    @pl.when(pl.program_id(0) == 0)
    def _():
        cnt_ref[...] = jnp.zeros_like(cnt_ref)

    x1 = _ln_rows(v_ref[...]) * g1_ref[...] + b1_ref[...]
    x1_ref[...] = x1
    h2 = (_ln_rows(x1) * (1.0 + sc_ref[...]) + sh_ref[...]).astype(BF16)
    h2_ref[...] = h2
    logits = jnp.dot(h2, wr_ref[...], preferred_element_type=F32) + br_ref[...]
    lane = lax.broadcasted_iota(jnp.int32, logits.shape, 1)
    big = jnp.int32(ROUTER_LANES)
    neg = jnp.float32(-jnp.inf)

    def first_lane_of_max(vals):
        m = jnp.max(vals, axis=1, keepdims=True)
        return m, jnp.min(jnp.where(vals == m, lane, big), axis=1, keepdims=True)

    lgt = jnp.where(lane < N_GROUPS, logits, neg)
    eg = jnp.exp(lgt - jnp.max(lgt, axis=1, keepdims=True))
    pg = eg / jnp.sum(eg, axis=1, keepdims=True)
    grp_prob, grp = first_lane_of_max(jnp.where(lane < N_GROUPS, pg, -1.0))
    lo = N_GROUPS + grp * EXPERTS_PER_GROUP
    in_grp = (lane >= lo) & (lane < lo + EXPERTS_PER_GROUP)
    let = jnp.where(in_grp, logits, neg)
    ee = jnp.exp(let - jnp.max(let, axis=1, keepdims=True))
    pe = jnp.where(in_grp, ee / jnp.sum(ee, axis=1, keepdims=True), -1.0)
    p1, l1 = first_lane_of_max(pe)
    p2, l2 = first_lane_of_max(jnp.where(lane == l1, -1.0, pe))
    den = p1 + p2
    gate1, gate2 = grp_prob * p1 / den, grp_prob * p2 / den
    e1, e2 = l1 - N_GROUPS, l2 - N_GROUPS
    tm = logits.shape[0]
    hot1 = (lane == e1).astype(F32)
    hot2 = (lane == e2).astype(F32)
    earlier = (lax.broadcasted_iota(jnp.int32, (tm, tm), 0) > lax.broadcasted_iota(jnp.int32, (tm, tm), 1)).astype(BF16)
    before1 = jnp.dot(earlier, hot1.astype(BF16), preferred_element_type=F32)
    n1 = jnp.sum(hot1, axis=0, keepdims=True)
    before2 = jnp.dot(earlier, hot2.astype(BF16), preferred_element_type=F32) + n1
    cnt = cnt_ref[...]
    slot1 = jnp.sum((before1 + cnt) * hot1, axis=1, keepdims=True)
    slot2 = jnp.sum((before2 + cnt) * hot2, axis=1, keepdims=True)
    n_blk = n1 + jnp.sum(hot2, axis=0, keepdims=True)
    lower = (lax.broadcasted_iota(jnp.int32, (ROUTER_LANES, ROUTER_LANES), 0)
             < lax.broadcasted_iota(jnp.int32, (ROUTER_LANES, ROUTER_LANES), 1)).astype(BF16)
    start = jnp.dot(jnp.broadcast_to(n_blk, (8, ROUTER_LANES)).astype(BF16), lower, preferred_element_type=F32)[0:1]
    local1 = jnp.sum((before1 + start) * hot1, axis=1, keepdims=True)
    local2 = jnp.sum((before2 + start) * hot2, axis=1, keepdims=True)
    cnt_ref[...] = cnt + n_blk
    vals = (e1.astype(F32), e2.astype(F32), gate1, gate2, slot1, slot2, local1, local2)
    out = jnp.zeros_like(logits)
    for k, val in enumerate(vals):
        out = jnp.where(lane == k, val, out)
    rt_ref[...] = out
    sub = lax.broadcasted_iota(jnp.int32, (8, ROUTER_LANES), 0)
    tab_ref[...] = jnp.where(sub == 0, n_blk, jnp.where(sub == 1, start, jnp.where(sub == 2, cnt, 0.0)))


def _post_mix(v, ln_g, ln_b, sc, sh, w_router, b_router, tp, dec_seq):
    t, d = v.shape
    tm = TM_ROW
    npb = tp // tm
    cidx = lambda i: jnp.where(i < npb, 0, 1 + (jnp.maximum(i - npb, 0) * tm) // dec_seq)
    row = pl.BlockSpec((tm, d), lambda i: (i, 0))
    vec = pl.BlockSpec((1, d), lambda i: (0, 0))
    cvec = pl.BlockSpec((None, 1, d), lambda i: (cidx(i), 0, 0))
    return pl.pallas_call(
        _post_mix_kernel,
        grid=(t // tm,),
        in_specs=[row, vec, vec, cvec, cvec,
                  pl.BlockSpec((d, ROUTER_LANES), lambda i: (0, 0)),
                  pl.BlockSpec((1, ROUTER_LANES), lambda i: (0, 0))],
        out_specs=[row, row,
                   pl.BlockSpec((tm, ROUTER_LANES), lambda i: (i, 0)),
                   pl.BlockSpec((1, ROUTER_LANES), lambda i: (0, 0)),
                   pl.BlockSpec((None, 8, ROUTER_LANES), lambda i: (i, 0, 0))],
        out_shape=[jax.ShapeDtypeStruct((t, d), F32), jax.ShapeDtypeStruct((t, d), BF16),
                   jax.ShapeDtypeStruct((t, ROUTER_LANES), F32), jax.ShapeDtypeStruct((1, ROUTER_LANES), F32),
                   jax.ShapeDtypeStruct((t // tm, 8, ROUTER_LANES), F32)],
        compiler_params=_params("arbitrary"),
        name="ln_ln_router",
    )(v, ln_g.reshape(1, d), ln_b.reshape(1, d), sc, sh, w_router, b_router)


def _on_parity(blk, fn):
    for par in range(2):
        pl.when(lax.rem(blk, 2) == par)(functools.partial(fn, blk, par))


def _dispatch_kernel(run_n_ref, run_src_ref, run_dst_ref, pad_row_ref, pad_n_ref, nact_ref,
                     h_ref, rt_ref, o_hbm, stage_ref, zero_ref, sems):
    tb = pl.program_id(0)
    n_tb = pl.num_programs(0)
    tm, d = h_ref.shape
    slab = d // 2 // 128
    n_rows_out = o_hbm.shape[0] // slab
    fill_sem = sems.at[2]

    def pieces(wait, src_ref, src_row, dst_row, n, max_size, sem):
        done = jnp.int32(0)
        size = max_size
        while size >= 1:
            take = (n & size) != 0
            src0 = 0 if src_row is None else pl.multiple_of((src_row + done) * slab, slab)
            cp = pltpu.make_async_copy(
                src_ref.at[pl.ds(src0, size * slab), :],
                o_hbm.at[pl.ds(pl.multiple_of((dst_row + done) * slab, slab), size * slab), :], sem)
            pl.when(take)(cp.wait if wait else cp.start)
            done = done + jnp.where(take, size, 0)
            size //= 2

    def runs(wait, blk, par):
        def body(e, carry):
            k = blk * N_EXPERTS + e
            pieces(wait, stage_ref.at[par], run_src_ref[k], run_dst_ref[k], run_n_ref[k], tm, sems.at[par])
            return carry
        lax.fori_loop(0, N_EXPERTS, body, 0)

    def fill(wait):
        def tail(e, carry):
            pieces(wait, zero_ref, None, pad_row_ref[e], pad_n_ref[e], tm // 2, fill_sem)
            return carry

        def spare(b, carry):
            pieces(wait, zero_ref, None, b * tm, jnp.int32(tm), tm, fill_sem)
            return carry

        lax.fori_loop(0, N_EXPERTS, tail, 0)
        lax.fori_loop(nact_ref[0], n_rows_out // tm, spare, 0)

    @pl.when(tb == 0)
    def _():
        zero_ref[...] = jnp.zeros_like(zero_ref)
        fill(False)

    @pl.when(tb >= 2)
    def _():
        _on_parity(tb, lambda blk, par: runs(True, blk - 2, par))

    rt = rt_ref[...]
    local1 = rt[:, 6:7].astype(jnp.int32)
    local2 = rt[:, 7:8].astype(jnp.int32)
    place = lax.broadcasted_iota(jnp.int32, (tm, TOP_K * tm), 1)
    onehot = ((place == local1) | (place == local2)).astype(F32)
    xp = jnp.dot(onehot.T.astype(BF16), h_ref[...], preferred_element_type=F32)
    packed = (lax.bitcast_convert_type(xp[:, d // 2:], jnp.uint32)
              | (lax.bitcast_convert_type(xp[:, :d // 2], jnp.uint32) >> 16))
    _store_as_slabs(stage_ref.at[lax.rem(tb, 2)], packed)
    _on_parity(tb, functools.partial(runs, False))

    @pl.when(tb == n_tb - 1)
    def _():
        _on_parity(tb - 1, functools.partial(runs, True))
        _on_parity(tb, functools.partial(runs, True))
        fill(True)


def _dispatch_rows(h2, route, tables, n_blocks):
    t, d = h2.shape
    tm = TM_ROW
    slab = d // 2 // 128
    assert tm == TM_MOE and t // tm >= 2
    return pl.pallas_call(
        _dispatch_kernel,
        grid_spec=pltpu.PrefetchScalarGridSpec(
            num_scalar_prefetch=len(tables),
            grid=(t // tm,),
            in_specs=[pl.BlockSpec((tm, d), lambda i, *_: (i, 0)),
                      pl.BlockSpec((tm, ROUTER_LANES), lambda i, *_: (i, 0))],
            out_specs=pl.BlockSpec(memory_space=pl.ANY),
            scratch_shapes=[pltpu.VMEM((2, TOP_K * tm * slab, 128), jnp.uint32),
                            pltpu.VMEM((tm * slab, 128), jnp.uint32), pltpu.SemaphoreType.DMA((3,))]),
        out_shape=jax.ShapeDtypeStruct((n_blocks * TM_MOE * slab, 128), jnp.uint32),
        compiler_params=_params("arbitrary"),
        name="moe_dispatch",
    )(*tables, h2, route)


def _expert_up_kernel(se_ref, sj_ref, sb_ref, oj_ref, ob_ref, fst_ref, gp_ref, nxt_ref, ne_ref, nj_ref, ns_ref,
                      x_ref, wg_hbm, wu_hbm, o_ref, wg_st, wu_st, sems):
    live = pl.program_id(0) < ns_ref[0]
    _stream_group_weights(se_ref, sj_ref, fst_ref, gp_ref, nxt_ref, ne_ref, nj_ref, live,
                          ((wg_hbm, wg_st), (wu_hbm, wu_st)), sems)

    @pl.when(live)
    def _():
        x = _unpack_bf16_pairs(_load_from_slabs(x_ref, TM_MOE))
        slot = gp_ref[pl.program_id(0)]
        a = jnp.dot(x, wg_st[slot], preferred_element_type=F32)
        u = jnp.dot(x, wu_st[slot], preferred_element_type=F32)
        o_ref[...] = ((a * _sigmoid(a)) * u).astype(o_ref.dtype)

    @pl.when(jnp.logical_not(live))
    def _():
        o_ref[...] = jnp.zeros_like(o_ref)


def _stream_group_weights(se_ref, sj_ref, fst_ref, gp_ref, nxt_ref, ne_ref, nj_ref, live, weights, sems):
    s = pl.program_id(0)

    def copies(e, j, slot):
        out = []
        for k, (w_hbm, w_st) in enumerate(weights):
            tn = w_st.shape[2]
            src = w_hbm.at[e, :, pl.ds(pl.multiple_of(j * tn, tn), tn)]
            out.append(pltpu.make_async_copy(src, w_st.at[slot], sems.at[k, slot]))
        return out

    def first_step(slot):
        cur = copies(se_ref[s], sj_ref[s], slot)

        @pl.when(s == 0)
        def _():
            for cp in cur:
                cp.start()

        @pl.when(nxt_ref[s] == 1)
        def _():
            for cp in copies(ne_ref[s], nj_ref[s], 1 - slot):
                cp.start()

        for cp in cur:
            cp.wait()

    for slot in range(2):
        pl.when(live & (fst_ref[s] == 1) & (gp_ref[s] == slot))(functools.partial(first_step, slot))


_N_SCHED = 11


def _sched_map(fn):
    return lambda s, *refs: fn(s, *refs[:_N_SCHED])


def _expert_up(xs, w_gate, w_up, sched, n_steps):
    d, de = w_gate.shape[1], w_gate.shape[2]
    tm, tf = TM_MOE, TF_MOE
    slab = d // 2 // 128
    r = xs.shape[0] // slab
    return pl.pallas_call(
        _expert_up_kernel,
        grid_spec=pltpu.PrefetchScalarGridSpec(
            num_scalar_prefetch=_N_SCHED,
            grid=(n_steps,),
            in_specs=[pl.BlockSpec((tm * slab, 128), _sched_map(lambda s, se, sj, sb, *_: (sb[s], 0))),
                      pl.BlockSpec(memory_space=pl.ANY), pl.BlockSpec(memory_space=pl.ANY)],
            out_specs=pl.BlockSpec((tm, tf), _sched_map(lambda s, se, sj, sb, oj, ob, *_: (ob[s], oj[s]))),
            scratch_shapes=[pltpu.VMEM((2, d, tf), F32), pltpu.VMEM((2, d, tf), F32),
                            pltpu.SemaphoreType.DMA((2, 2))]),
        out_shape=jax.ShapeDtypeStruct((r, de), BF16),
        compiler_params=_params("arbitrary"),
        name="moe_gate_up",
    )(*sched, xs, w_gate, w_up)


def _expert_down_kernel(se_ref, sj_ref, sb_ref, oj_ref, ob_ref, fst_ref, gp_ref, nxt_ref, ne_ref, nj_ref, ns_ref,
                        h_ref, wd_hbm, o_ref, wd_st, sems):
    live = pl.program_id(0) < ns_ref[0]
    _stream_group_weights(se_ref, sj_ref, fst_ref, gp_ref, nxt_ref, ne_ref, nj_ref, live, ((wd_hbm, wd_st),), sems)

    @pl.when(live)
    def _():
        y = jnp.dot(h_ref[...].astype(F32), wd_st[gp_ref[pl.program_id(0)]], preferred_element_type=F32)
        _store_as_slabs(o_ref, _pack_bf16_pairs(y))

    @pl.when(jnp.logical_not(live))
    def _():
        o_ref[...] = jnp.zeros_like(o_ref)


def _expert_down(hid, w_down, sched, n_steps):
    r, de = hid.shape
    d = w_down.shape[2]
    tm = TM_MOE
    slab = d // 2 // 128
    return pl.pallas_call(
        _expert_down_kernel,
        grid_spec=pltpu.PrefetchScalarGridSpec(
            num_scalar_prefetch=_N_SCHED,
            grid=(n_steps,),
            in_specs=[pl.BlockSpec((tm, de), _sched_map(lambda s, se, sj, sb, *_: (sb[s], 0))),
                      pl.BlockSpec(memory_space=pl.ANY)],
            out_specs=pl.BlockSpec((tm * slab, 128), _sched_map(lambda s, se, sj, sb, oj, ob, *_: (ob[s], 0))),
            scratch_shapes=[pltpu.VMEM((2, de, d), F32), pltpu.SemaphoreType.DMA((1, 2))]),
        out_shape=jax.ShapeDtypeStruct((r * slab, 128), jnp.uint32),
        compiler_params=_params("arbitrary"),
        name="moe_down",
    )(*sched, hid, w_down)


def _combine_kernel(npb, alpha, pos_ref, y_hbm, x1_ref, rt_ref, gt_ref, g_ref, b_ref,
                    op_ref, os_ref, buf_ref, sems):
    i = pl.program_id(0)
    n = pl.num_programs(0)
    tm = x1_ref.shape[0]
    slab = buf_ref.shape[2] // tm

    def rows(wait, blk, par):
        def body(g, carry):
            for u in range(ROW_UNROLL):
                r = g * ROW_UNROLL + u
                for k in range(TOP_K):
                    cp = _slab_copy(y_hbm, buf_ref.at[par, k], pos_ref[(blk * tm + r) * TOP_K + k], r, slab,
                                    sems.at[par])
                    cp.wait() if wait else cp.start(priority=k % 2)
            return carry
        lax.fori_loop(0, tm // ROW_UNROLL, body, 0)

    @pl.when(i == 0)
    def _():
        rows(False, 0, 0)

    @pl.when(i + 1 < n)
    def _():
        _on_parity(i + 1, functools.partial(rows, False))

    _on_parity(i, functools.partial(rows, True))
    rt = rt_ref[...]
    par = lax.rem(i, 2)
    expert_rows = lambda k: _unpack_bf16_pairs(_load_from_slabs(buf_ref.at[par, k], tm))
    f = rt[:, 2:3] * expert_rows(0) + rt[:, 3:4] * expert_rows(1)
    out = _ln_rows(alpha * x1_ref[...] + gt_ref[...] * f) * g_ref[...] + b_ref[...]

    @pl.when(i < npb)
    def _():
        op_ref[...] = out

    @pl.when(i >= npb)
    def _():
        os_ref[...] = out


def _combine(yb, pos, x1, route, gt, ln_g, ln_b, alpha, tp, dec_seq):
    t, d = x1.shape
    tm = TM_ROW
    npb = tp // tm
    cidx = lambda i: jnp.where(i < npb, 0, 1 + (jnp.maximum(i - npb, 0) * tm) // dec_seq)
    vec = pl.BlockSpec((1, d), lambda i, *_: (0, 0))
    return pl.pallas_call(
        functools.partial(_combine_kernel, npb, alpha),
        grid_spec=pltpu.PrefetchScalarGridSpec(
            num_scalar_prefetch=1,
            grid=(t // tm,),
            in_specs=[pl.BlockSpec(memory_space=pl.ANY),
                      pl.BlockSpec((tm, d), lambda i, *_: (i, 0)),
                      pl.BlockSpec((tm, ROUTER_LANES), lambda i, *_: (i, 0)),
                      pl.BlockSpec((None, 1, d), lambda i, *_: (cidx(i), 0, 0)),
                      vec, vec],
            out_specs=[pl.BlockSpec((tm, d), lambda i, *_: (jnp.minimum(i, npb - 1), 0)),
                       pl.BlockSpec((tm, d), lambda i, *_: (jnp.maximum(i - npb, 0), 0))],
            scratch_shapes=[pltpu.VMEM((2, TOP_K, tm * (d // 2 // 128), 128), jnp.uint32),
                            pltpu.SemaphoreType.DMA((2,))]),
        out_shape=[jax.ShapeDtypeStruct((tp, d), F32), jax.ShapeDtypeStruct((t - tp, d), F32)],
        compiler_params=_params("arbitrary"),
        name="moe_combine_ln",
    )(pos, yb, x1, route, gt, ln_g.reshape(1, d), ln_b.reshape(1, d))


def _dispatch_plan(route, counts_f, block_tab, n_tok):
    tm = TM_MOE
    n_blocks = n_tok * TOP_K // tm + N_EXPERTS
    i32 = jnp.int32
    eid = route[:, 0:TOP_K].astype(i32).reshape(-1)
    slot = route[:, 4:4 + TOP_K].astype(i32).reshape(-1)
    counts = counts_f[0, :N_EXPERTS].astype(i32)
    nb = (counts + tm - 1) // tm
    nb_end = jnp.cumsum(nb)
    bs = nb_end - nb
    n_act = nb_end[-1]
    experts = jnp.arange(N_EXPERTS, dtype=i32)
    look = lambda table, e: jnp.sum(jnp.where(e[:, None] == experts[None, :], table[None, :], 0), axis=1)
    tab = block_tab[:, :3, :N_EXPERTS].astype(i32)
    run_dst = (bs * tm)[None, :] + tab[:, 2]
    dispatch_tables = (tab[:, 0].reshape(-1), tab[:, 1].reshape(-1), run_dst.reshape(-1),
                       bs * tm + counts, nb * tm - counts, n_act.reshape(1))

    def schedule(n_inner):
        n_steps = n_inner * n_blocks
        n_live = n_inner * n_act

        def decode(step):
            s = jnp.minimum(step, n_live - 1)
            e = jnp.minimum(jnp.sum((s[:, None] >= n_inner * nb_end[None, :]).astype(i32), axis=1), N_EXPERTS - 1)
            nbe = jnp.maximum(look(nb, e), 1)
            loc = s - n_inner * look(bs, e)
            return e, loc // nbe, loc % nbe, nbe, look(bs, e)

        step = jnp.arange(n_steps, dtype=i32)
        e, sj, bi, nbe, bse = decode(step)
        live = step < n_live
        nxt_step = jnp.minimum(step, n_live - 1) + nbe - bi
        ne, nj, _, _, _ = decode(nxt_step)
        fst = (live & (bi == 0)).astype(i32)
        gp = (jnp.cumsum(fst) - 1) % 2
        nxt = (live & (bi == 0) & (nxt_step < n_live)).astype(i32)
        spare = jnp.maximum(step - n_live, 0)
        oj = jnp.where(live, sj, spare % n_inner)
        ob = jnp.where(live, bse + bi, n_act + spare // n_inner)
        sched = (e, sj, bse + bi, oj, ob, fst, gp, nxt, ne, nj, n_live.reshape(1))
        return tuple(a.astype(i32) for a in sched), n_steps

    pos = look(bs * tm, eid) + slot
    return pos, dispatch_tables, n_blocks, schedule


def kernel(x_prompt, x_sample, state_ret_fwd, state_ret_bwd, c, c_ctx, w_mod, b_mod, w_in, conv_w, conv_b, conv_ln_g, conv_ln_b, ret_decay_fwd, ret_decay_bwd, ret_gn_g, w_out, ln1_g, ln1_b, w_grp, b_grp, w_exp, b_exp, w_gate, w_up, w_down, ln2_g, ln2_b):
    depth = w_mod.shape[0]
    assert depth == 1, "single-layer step"
    bp, sp, d = x_prompt.shape
    bs_, ss, _ = x_sample.shape
    tp, ts = bp * sp, bs_ * ss
    t = tp + ts
    conv_width = conv_w.shape[2]
    ret_width = ret_gn_g.shape[1]
    assert ret_width == RET_HEADS * RET_D and sp % CHUNK == 0 and ss % CHUNK == 0
    alpha = (2.0 * depth) ** 0.25

    xp = x_prompt.reshape(tp, d)
    xs = x_sample.reshape(ts, d)
    cond8 = jnp.zeros((8, d), F32).at[0].set(c_ctx).at[1:1 + bs_].set(c)
    m = _modulation(cond8, w_mod[0], b_mod[0])
    sh1, sc1, gt1, sh2, sc2, gt2 = [m[:, k * d:(k + 1) * d].reshape(8, 1, d) for k in range(6)]

    h = _ln_modulate(xp, xs, sc1, sh1, ss)
    u_glu = _glu_proj(h, w_in[0], conv_width)
    cos_t, sin_t = _rope_tables(ss, RET_D)
    qkvg = _qkvg_proj(h, w_in[0], cos_t, sin_t, 2 * conv_width, 4 * ret_width, tp)

    u = _conv_module(u_glu, conv_w[0], conv_b[0], conv_ln_g[0], conv_ln_b[0], tp, ss)

    lg = jnp.stack([jax.nn.log_sigmoid(ret_decay_fwd[0].astype(F32)),
                    jax.nn.log_sigmoid(ret_decay_bwd[0].astype(F32))])
    gn = ret_gn_g[0].reshape(1, ret_width)
    r_p, new_f, new_b = _retention(qkvg, lg, gn, sp, 0, bp)
    r_s = _retention(qkvg, lg, gn, ss, tp // ss, bs_, state_ret_fwd, state_ret_bwd)
    r = jnp.concatenate([r_p, r_s], axis=0)

    v = _out_proj(u, r, w_out[0], xp, xs, gt1, alpha, ss)

    w_router = jnp.zeros((d, ROUTER_LANES), F32).at[:, :N_GROUPS].set(w_grp[0]).at[:, N_GROUPS:N_GROUPS + N_EXPERTS].set(w_exp[0]).astype(BF16)
    b_router = jnp.zeros((1, ROUTER_LANES), F32).at[0, :N_GROUPS].set(b_grp[0]).at[0, N_GROUPS:N_GROUPS + N_EXPERTS].set(b_exp[0])
    x1, h2, route, counts, block_tab = _post_mix(v, ln1_g[0], ln1_b[0], sc2, sh2, w_router, b_router, tp, ss)

    pos, dispatch_tables, n_blocks, schedule = _dispatch_plan(route, counts, block_tab, t)
    xg = _dispatch_rows(h2, route, dispatch_tables, n_blocks)
    sched_up, n_up = schedule(w_gate.shape[3] // TF_MOE)
    hid = _expert_up(xg, w_gate[0], w_up[0], sched_up, n_up)
    sched_dn, n_dn = schedule(1)
    yb = _expert_down(hid, w_down[0], sched_dn, n_dn)

    out_p, out_s = _combine(yb, pos, x1, route, gt2, ln2_g[0], ln2_b[0], alpha, tp, ss)
    return (out_p.reshape(bp, sp, d), out_s.reshape(bs_, ss, d), new_f, new_b)
```

```python
import functools

import numpy as np
import jax
import jax.numpy as jnp
from jax import lax
from jax.experimental import pallas as pl
from jax.experimental.pallas import tpu as pltpu

F32 = jnp.float32
BF16 = jnp.bfloat16

LN_EPS = 1e-5
CONV_K = 31
CONV_HALO = 16
RET_HEADS = 8
RET_D = 256
HEADS_PER_STEP = 4
CHUNK = 128
GRID_W = 64
ROPE_BASE = 10000.0
N_GROUPS = 4
EXPERTS_PER_GROUP = 8
N_EXPERTS = N_GROUPS * EXPERTS_PER_GROUP
TOP_K = 2
ROUTER_LANES = 128
VMEM_LIMIT = 56 * 1024 * 1024
TM_PROJ = 1024
TM_ROW = 256
TM_MOE = 256
TF_MOE = 512
ROW_UNROLL = 8


def _params(*sem):
    return pltpu.CompilerParams(dimension_semantics=tuple(sem), vmem_limit_bytes=VMEM_LIMIT)


def _sigmoid(x):
    return 1.0 / (1.0 + jnp.exp(-x))


def _ln_rows(x):
    mu = jnp.mean(x, axis=-1, keepdims=True)
    xc = x - mu
    var = jnp.mean(xc * xc, axis=-1, keepdims=True)
    return xc * lax.rsqrt(var + LN_EPS)


def _mod_kernel(c_ref, w_ref, b_ref, o_ref):
    c = c_ref[...]
    s = c * _sigmoid(c)
    o_ref[...] = jnp.dot(s.astype(BF16), w_ref[...].astype(BF16), preferred_element_type=F32) + b_ref[...]


def _modulation(cond8, w_mod, b_mod):
    d, n = w_mod.shape
    tn = 512
    return pl.pallas_call(
        _mod_kernel,
        grid=(n // tn,),
        in_specs=[pl.BlockSpec((8, d), lambda j: (0, 0)),
                  pl.BlockSpec((d, tn), lambda j: (0, j)),
                  pl.BlockSpec((1, tn), lambda j: (0, j))],
        out_specs=pl.BlockSpec((8, tn), lambda j: (0, j)),
        out_shape=jax.ShapeDtypeStruct((8, n), F32),
        compiler_params=_params("arbitrary"),
        name="modulation",
    )(cond8, w_mod, b_mod.reshape(1, n))


def _ln_mod_kernel(npb, xp_ref, xs_ref, sc_ref, sh_ref, o_ref):
    def body(x_ref):
        y = _ln_rows(x_ref[...])
        o_ref[...] = (y * (1.0 + sc_ref[...]) + sh_ref[...]).astype(o_ref.dtype)

    i = pl.program_id(0)
    pl.when(i < npb)(lambda: body(xp_ref))
    pl.when(i >= npb)(lambda: body(xs_ref))


def _ln_modulate(xp, xs, sc, sh, dec_seq):
    (tp, d), ts = xp.shape, xs.shape[0]
    tm = 512
    npb, nsb = tp // tm, ts // tm
    cidx = lambda i: jnp.where(i < npb, 0, 1 + (jnp.maximum(i - npb, 0) * tm) // dec_seq)
    return pl.pallas_call(
        functools.partial(_ln_mod_kernel, npb),
        grid=(npb + nsb,),
        in_specs=[pl.BlockSpec((tm, d), lambda i: (jnp.minimum(i, npb - 1), 0)),
                  pl.BlockSpec((tm, d), lambda i: (jnp.maximum(i - npb, 0), 0)),
                  pl.BlockSpec((None, 1, d), lambda i: (cidx(i), 0, 0)),
                  pl.BlockSpec((None, 1, d), lambda i: (cidx(i), 0, 0))],
        out_specs=pl.BlockSpec((tm, d), lambda i: (i, 0)),
        out_shape=jax.ShapeDtypeStruct((tp + ts, d), BF16),
        compiler_params=_params("arbitrary"),
        name="ln_modulate",
    )(xp, xs, sc, sh)


def _glu_proj_kernel(h_ref, wv_ref, wg_ref, o_ref):
    h = h_ref[...]
    a = jnp.dot(h, wv_ref[...].astype(BF16), preferred_element_type=F32)
    g = jnp.dot(h, wg_ref[...].astype(BF16), preferred_element_type=F32)
    o_ref[...] = a * _sigmoid(g)


def _glu_proj(h, w_in, conv_width):
    t, d = h.shape
    tm, tn = TM_PROJ, 256
    goff = conv_width // tn
    return pl.pallas_call(
        _glu_proj_kernel,
        grid=(t // tm, conv_width // tn),
        in_specs=[pl.BlockSpec((tm, d), lambda i, j: (i, 0)),
                  pl.BlockSpec((d, tn), lambda i, j: (0, j)),
                  pl.BlockSpec((d, tn), lambda i, j: (0, j + goff))],
        out_specs=pl.BlockSpec((tm, tn), lambda i, j: (i, j)),
        out_shape=jax.ShapeDtypeStruct((t, conv_width), F32),
        compiler_params=_params("arbitrary", "arbitrary"),
        name="glu_proj",
    )(h, w_in, w_in)


def _qkvg_proj_kernel(npb, tiles_per_part, k_scale, h_ref, w_ref, cos_ref, sin_ref, o_ref):
    i, j = pl.program_id(0), pl.program_id(1)
    z = jnp.dot(h_ref[...], w_ref[...].astype(BF16), preferred_element_type=F32)
    tpp = tiles_per_part

    @pl.when(j >= 3 * tpp)
    def _():
        o_ref[...] = (z * _sigmoid(z)).astype(o_ref.dtype)

    @pl.when((j >= 2 * tpp) & (j < 3 * tpp))
    def _():
        o_ref[...] = z.astype(o_ref.dtype)

    @pl.when(j < 2 * tpp)
    def _():
        zz = z * jnp.where(j >= tpp, k_scale, 1.0).astype(F32)

        @pl.when(i < npb)
        def _():
            o_ref[...] = zz.astype(o_ref.dtype)

        @pl.when(i >= npb)
        def _():
            for s in range(zz.shape[1] // 128):
                cs = slice(s * 128, (s + 1) * 128)
                ts_ = slice((s * 128) % RET_D, (s * 128) % RET_D + 128)
                zs = zz[:, cs]
                o_ref[:, cs] = (zs * cos_ref[:, ts_] + pltpu.roll(zs, 64, axis=1) * sin_ref[:, ts_]).astype(o_ref.dtype)


def _qkvg_proj(h, w_in, cos_t, sin_t, col0, n_cols, tp):
    t, d = h.shape
    tm, tn = TM_PROJ, 512
    assert cos_t.shape == (tm, RET_D), "a projection row tile is one latent sequence"
    npb = tp // tm
    tpp = (n_cols // 4) // tn
    return pl.pallas_call(
        functools.partial(_qkvg_proj_kernel, npb, tpp, RET_D ** -0.5),
        grid=(t // tm, n_cols // tn),
        in_specs=[pl.BlockSpec((tm, d), lambda i, j: (i, 0)),
                  pl.BlockSpec((d, tn), lambda i, j: (0, j + col0 // tn)),
                  pl.BlockSpec((tm, RET_D), lambda i, j: (0, 0)),
                  pl.BlockSpec((tm, RET_D), lambda i, j: (0, 0))],
        out_specs=pl.BlockSpec((tm, tn), lambda i, j: (i, j)),
        out_shape=jax.ShapeDtypeStruct((t, n_cols), BF16),
        compiler_params=_params("arbitrary", "arbitrary"),
        name="qkvg_proj",
    )(h, w_in, cos_t, sin_t)


def _rope_tables(n_tok, width):
    rows = n_tok // GRID_W
    r_idx = np.repeat(np.arange(rows), GRID_W).astype(np.float64)
    c_idx = np.tile(np.arange(GRID_W), rows).astype(np.float64)
    dq = RET_D // 2
    inv = ROPE_BASE ** (-np.arange(dq // 2, dtype=np.float64) / (dq // 2))
    sign = np.concatenate([-np.ones(dq // 2), np.ones(dq // 2)])
    cos_h, sin_h = [], []
    for idx in (r_idx, c_idx):
        ang = idx[:, None] * inv
        cos_h.append(np.concatenate([np.cos(ang), np.cos(ang)], axis=1))
        sin_h.append(np.concatenate([np.sin(ang), np.sin(ang)], axis=1) * sign)
    cos_h, sin_h = np.concatenate(cos_h, axis=1), np.concatenate(sin_h, axis=1)
    reps = width // RET_D
    return (jnp.asarray(np.tile(cos_h, (1, reps)), F32), jnp.asarray(np.tile(sin_h, (1, reps)), F32))


def _conv_kernel(npb, tiles_per_seq, uc_ref, up_ref, un_ref, w_ref, b_ref, g_ref, bt_ref, o_ref, buf_ref, y_ref):
    r = pl.program_id(0)
    tm = uc_ref.shape[0]
    ncg = buf_ref.shape[0]
    t = lax.rem(jnp.maximum(r - npb, 0), tiles_per_seq)
    is_s = r >= npb
    has_prev = is_s & (t != 0)
    has_next = is_s & (t != tiles_per_seq - 1)
    for cg in range(ncg):
        cs = slice(cg * 128, (cg + 1) * 128)
        buf_ref[cg, 0:CONV_HALO, :] = jnp.where(has_prev, up_ref[:, cs], 0.0)
        buf_ref[cg, CONV_HALO:CONV_HALO + tm, :] = uc_ref[:, cs]
        buf_ref[cg, CONV_HALO + tm:, :] = jnp.where(has_next, un_ref[:, cs], 0.0)

    off = CONV_HALO - CONV_K // 2

    def body(cg, carry):
        acc = jnp.zeros((tm, 128), F32)
        for tap in range(CONV_K):
            acc = acc + buf_ref[cg, off + tap:off + tap + tm, :] * w_ref[cg, tap:tap + 1, :]
        y_ref[cg] = acc + b_ref[cg]
        return carry

    lax.fori_loop(0, ncg, body, 0)

    n_ch = ncg * 128
    tot = y_ref[0]
    for cg in range(1, ncg):
        tot = tot + y_ref[cg]
    mu = jnp.sum(tot, axis=1, keepdims=True) * (1.0 / n_ch)
    sq = jnp.zeros((tm, 128), F32)
    for cg in range(ncg):
        dv = y_ref[cg] - mu
        sq = sq + dv * dv
    var = jnp.sum(sq, axis=1, keepdims=True) * (1.0 / n_ch)
    rstd = lax.rsqrt(var + LN_EPS)
    for cg in range(ncg):
        cs = slice(cg * 128, (cg + 1) * 128)
        v = (y_ref[cg] - mu) * rstd * g_ref[cg] + bt_ref[cg]
        o_ref[:, cs] = (v * _sigmoid(v)).astype(o_ref.dtype)


def _conv_module(u, conv_w, conv_b, ln_g, ln_b, tp, dec_seq):
    t, c = u.shape
    tm = TM_ROW
    ncg = c // 128
    npb = tp // tm
    hb = tm // CONV_HALO
    n_halo_blocks = t // CONV_HALO
    w3 = jnp.zeros((32, c), F32).at[:CONV_K].set(conv_w).reshape(32, ncg, 128).transpose(1, 0, 2)
    vec = lambda a: a.reshape(ncg, 1, 128)
    return pl.pallas_call(
        functools.partial(_conv_kernel, npb, dec_seq // tm),
        grid=(t // tm,),
        in_specs=[pl.BlockSpec((tm, c), lambda r: (r, 0)),
                  pl.BlockSpec((CONV_HALO, c), lambda r: (jnp.maximum(r * hb - 1, 0), 0)),
                  pl.BlockSpec((CONV_HALO, c), lambda r: (jnp.minimum((r + 1) * hb, n_halo_blocks - 1), 0)),
                  pl.BlockSpec((ncg, 32, 128), lambda r: (0, 0, 0)),
                  pl.BlockSpec((ncg, 1, 128), lambda r: (0, 0, 0)),
                  pl.BlockSpec((ncg, 1, 128), lambda r: (0, 0, 0)),
                  pl.BlockSpec((ncg, 1, 128), lambda r: (0, 0, 0))],
        out_specs=pl.BlockSpec((tm, c), lambda r: (r, 0)),
        out_shape=jax.ShapeDtypeStruct((t, c), BF16),
        scratch_shapes=[pltpu.VMEM((ncg, tm + 2 * CONV_HALO, 128), F32),
                        pltpu.VMEM((ncg, tm, 128), F32)],
        compiler_params=_params("arbitrary"),
        name="conv_ln_swish",
    )(u, u, u, w3, vec(conv_b), vec(ln_g), vec(ln_b))


def _retention_kernel(nc, has_init, lg_ref, q_ref, k_ref, v_ref, g_ref, gn_ref, *rest):
    if has_init:
        s0f_ref, s0b_ref, r_ref, o_ref, sf_ref, sb_ref = rest
    else:
        s0f_ref = s0b_ref = None
        r_ref, sf_ref, sb_ref, o_ref = rest
    for hh in range(HEADS_PER_STEP):
        cols = slice(hh * RET_D, (hh + 1) * RET_D)
        _retention_head(nc, lg_ref, pl.program_id(1) * HEADS_PER_STEP + hh,
                        q_ref.at[:, cols], k_ref.at[:, cols], v_ref.at[:, cols], g_ref.at[:, cols],
                        gn_ref.at[:, cols], None if s0f_ref is None else s0f_ref.at[hh],
                        None if s0b_ref is None else s0b_ref.at[hh],
                        r_ref.at[:, cols], sf_ref.at[hh], sb_ref.at[hh], o_ref.at[hh])


def _retention_head(nc, lg_ref, hd, q_ref, k_ref, v_ref, g_ref, gn_ref, s0f_ref, s0b_ref, r_ref, sf_ref, sb_ref, o_ref):
    has_init = s0f_ref is not None
    lgf, lgb = lg_ref[0, hd], lg_ref[1, hd]
    c = CHUNK
    row = lax.broadcasted_iota(jnp.int32, (c, c), 0)
    col = lax.broadcasted_iota(jnp.int32, (c, c), 1)
    diff = (row - col).astype(F32)
    dec = (jnp.where(diff >= 0, jnp.exp(jnp.maximum(diff, 0.0) * lgf), 0.0)
           + jnp.where(diff <= 0, jnp.exp(jnp.maximum(-diff, 0.0) * lgb), 0.0))
    pos = lax.broadcasted_iota(jnp.int32, (c, 1), 0).astype(F32)
    xi_f = jnp.exp((pos + 1.0) * lgf)
    zeta_f = jnp.exp((c - 1.0 - pos) * lgf)
    xi_b = jnp.exp((c - pos) * lgb)
    zeta_b = jnp.exp(pos * lgb)
    gch_f = jnp.exp(jnp.full((1, RET_D), c, F32) * lgf)
    gch_b = jnp.exp(jnp.full((1, RET_D), c, F32) * lgb)

    if has_init:
        sf_ref[...] = s0f_ref[...]
        sb_ref[...] = s0b_ref[...]
    else:
        sf_ref[...] = jnp.zeros_like(sf_ref)
        sb_ref[...] = jnp.zeros_like(sb_ref)

    def chunk(ci):
        sl = slice(ci * c, (ci + 1) * c)
        return q_ref[sl, :], k_ref[sl, :], v_ref[sl, :], sl

    def state_update(s_ref, kc, vc, zeta, gch):
        kz = (kc.astype(F32) * zeta).T.astype(BF16)
        s_ref[...] = gch * s_ref[...] + jnp.dot(kz, vc, preferred_element_type=F32)

    for ci in range(nc):
        qc, kc, vc, sl = chunk(ci)
        s = lax.dot_general(qc, kc, (((1,), (1,)), ((), ())), preferred_element_type=F32)
        p = (s * dec).astype(BF16)
        o = jnp.dot(p, vc, preferred_element_type=F32)
        o = o + jnp.dot(qc, sf_ref[...].astype(BF16), preferred_element_type=F32) * xi_f
        o_ref[sl, :] = o
        state_update(sf_ref, kc, vc, zeta_f, gch_f)

    for ci in reversed(range(nc)):
        qc, kc, vc, sl = chunk(ci)
        o_ref[sl, :] = o_ref[sl, :] + jnp.dot(qc, sb_ref[...].astype(BF16), preferred_element_type=F32) * xi_b
        state_update(sb_ref, kc, vc, zeta_b, gch_b)

    y = _ln_rows(o_ref[...])
    r_ref[...] = (g_ref[...].astype(F32) * (y * gn_ref[...])).astype(r_ref.dtype)


def _retention(qkvg, lg, gn, n, row_blk0, n_seq, s0f=None, s0b=None):
    has_init = s0f is not None
    nh, dd, hps = RET_HEADS, RET_D, HEADS_PER_STEP
    wd = hps * dd
    col = lambda part: (lambda b, h: (b + row_blk0, part * (nh // hps) + h))
    st_spec = pl.BlockSpec((None, None, hps, dd, dd), lambda b, h: (b, 0, h, 0, 0))
    in_specs = [pl.BlockSpec(memory_space=pltpu.SMEM),
                pl.BlockSpec((n, wd), col(0)), pl.BlockSpec((n, wd), col(1)),
                pl.BlockSpec((n, wd), col(2)), pl.BlockSpec((n, wd), col(3)),
                pl.BlockSpec((1, wd), lambda b, h: (0, h))]
    args = [lg, qkvg, qkvg, qkvg, qkvg, gn]
    r_shape = jax.ShapeDtypeStruct((n_seq * n, nh * dd), BF16)
    r_spec = pl.BlockSpec((n, wd), lambda b, h: (b, h))
    st_scratch = pltpu.VMEM((hps, dd, dd), F32)
    if has_init:
        in_specs += [st_spec, st_spec]
        args += [s0f, s0b]
        out_specs, out_shape = r_spec, r_shape
        scratch = [pltpu.VMEM((hps, n, dd), F32), st_scratch, st_scratch]
    else:
        st_shape = jax.ShapeDtypeStruct((n_seq, 1, nh, dd, dd), F32)
        out_specs, out_shape = [r_spec, st_spec, st_spec], [r_shape, st_shape, st_shape]
        scratch = [pltpu.VMEM((hps, n, dd), F32)]
    return pl.pallas_call(
        functools.partial(_retention_kernel, n // CHUNK, has_init),
        grid=(n_seq, nh // hps),
        in_specs=in_specs, out_specs=out_specs, out_shape=out_shape,
        scratch_shapes=scratch,
        compiler_params=_params("arbitrary", "arbitrary"),
        name="retention_latent" if has_init else "retention_context",
    )(*args)


def _out_proj_kernel(npb, alpha, u_ref, r_ref, w1_ref, w2_ref, xp_ref, xs_ref, gt_ref, o_ref):
    y = (jnp.dot(u_ref[...], w1_ref[...].astype(BF16), preferred_element_type=F32)
         + jnp.dot(r_ref[...], w2_ref[...].astype(BF16), preferred_element_type=F32))
    i = pl.program_id(0)

    @pl.when(i < npb)
    def _():
        o_ref[...] = alpha * xp_ref[...] + gt_ref[...] * y

    @pl.when(i >= npb)
    def _():
        o_ref[...] = alpha * xs_ref[...] + gt_ref[...] * y


def _out_proj(u, r, w_out, xp, xs, gt, alpha, dec_seq):
    t, kh = u.shape
    d = w_out.shape[1]
    tm, tn = TM_PROJ, 512
    npb = xp.shape[0] // tm
    cidx = lambda i: jnp.where(i < npb, 0, 1 + (jnp.maximum(i - npb, 0) * tm) // dec_seq)
    return pl.pallas_call(
        functools.partial(_out_proj_kernel, npb, alpha),
        grid=(t // tm, d // tn),
        in_specs=[pl.BlockSpec((tm, kh), lambda i, j: (i, 0)),
                  pl.BlockSpec((tm, kh), lambda i, j: (i, 0)),
                  pl.BlockSpec((kh, tn), lambda i, j: (0, j)),
                  pl.BlockSpec((kh, tn), lambda i, j: (1, j)),
                  pl.BlockSpec((tm, tn), lambda i, j: (jnp.minimum(i, npb - 1), j)),
                  pl.BlockSpec((tm, tn), lambda i, j: (jnp.maximum(i - npb, 0), j)),
                  pl.BlockSpec((None, 1, tn), lambda i, j: (cidx(i), 0, j))],
        out_specs=pl.BlockSpec((tm, tn), lambda i, j: (i, j)),
        out_shape=jax.ShapeDtypeStruct((t, d), F32),
        compiler_params=_params("arbitrary", "arbitrary"),
        name="out_proj_residual",
    )(u, r, w_out, w_out, xp, xs, gt)


def _pack_bf16_pairs(x):
    n = x.shape[1] // 2
    lo = lax.bitcast_convert_type(x[:, :n].astype(BF16).astype(F32), jnp.uint32)
    hi = lax.bitcast_convert_type(x[:, n:].astype(BF16).astype(F32), jnp.uint32)
    return hi | (lo >> 16)


def _unpack_bf16_pairs(p):
    lo = lax.bitcast_convert_type(p << 16, F32)
    hi = lax.bitcast_convert_type(p & jnp.uint32(0xFFFF0000), F32)
    return jnp.concatenate([lo, hi], axis=1)


def _store_as_slabs(ref, rows):
    m, n = rows.shape
    slab = n // 128
    for s in range(slab):
        ref[pl.ds(s, m, stride=slab), :] = rows[:, s * 128:(s + 1) * 128]


def _load_from_slabs(ref, m):
    slab = ref.shape[0] // m
    return jnp.concatenate([ref[pl.ds(s, m, stride=slab), :] for s in range(slab)], axis=1)


def _slab_copy(src, dst, src_row, dst_row, slab, sem):
    return pltpu.make_async_copy(src.at[pl.ds(pl.multiple_of(src_row * slab, slab), slab), :],
                                 dst.at[pl.ds(pl.multiple_of(dst_row * slab, slab), slab), :], sem)


def _post_mix_kernel(v_ref, g1_ref, b1_ref, sc_ref, sh_ref, wr_ref, br_ref, x1_ref, h2_ref, rt_ref, cnt_ref, tab_ref):
    @pl.when(pl.program_id(0) == 0)
    def _():
        cnt_ref[...] = jnp.zeros_like(cnt_ref)

    x1 = _ln_rows(v_ref[...]) * g1_ref[...] + b1_ref[...]
    x1_ref[...] = x1
    h2 = (_ln_rows(x1) * (1.0 + sc_ref[...]) + sh_ref[...]).astype(BF16)
    h2_ref[...] = h2
    logits = jnp.dot(h2, wr_ref[...], preferred_element_type=F32) + br_ref[...]
    lane = lax.broadcasted_iota(jnp.int32, logits.shape, 1)
    big = jnp.int32(ROUTER_LANES)
    neg = jnp.float32(-jnp.inf)

    def first_lane_of_max(vals):
        m = jnp.max(vals, axis=1, keepdims=True)
        return m, jnp.min(jnp.where(vals == m, lane, big), axis=1, keepdims=True)

    lgt = jnp.where(lane < N_GROUPS, logits, neg)
    eg = jnp.exp(lgt - jnp.max(lgt, axis=1, keepdims=True))
    pg = eg / jnp.sum(eg, axis=1, keepdims=True)
    grp_prob, grp = first_lane_of_max(jnp.where(lane < N_GROUPS, pg, -1.0))
    lo = N_GROUPS + grp * EXPERTS_PER_GROUP
    in_grp = (lane >= lo) & (lane < lo + EXPERTS_PER_GROUP)
    let = jnp.where(in_grp, logits, neg)
    ee = jnp.exp(let - jnp.max(let, axis=1, keepdims=True))
    pe = jnp.where(in_grp, ee / jnp.sum(ee, axis=1, keepdims=True), -1.0)
    p1, l1 = first_lane_of_max(pe)
    p2, l2 = first_lane_of_max(jnp.where(lane == l1, -1.0, pe))
    den = p1 + p2
    gate1, gate2 = grp_prob * p1 / den, grp_prob * p2 / den
    e1, e2 = l1 - N_GROUPS, l2 - N_GROUPS
    tm = logits.shape[0]
    hot1 = (lane == e1).astype(F32)
    hot2 = (lane == e2).astype(F32)
    earlier = (lax.broadcasted_iota(jnp.int32, (tm, tm), 0) > lax.broadcasted_iota(jnp.int32, (tm, tm), 1)).astype(BF16)
    before1 = jnp.dot(earlier, hot1.astype(BF16), preferred_element_type=F32)
    n1 = jnp.sum(hot1, axis=0, keepdims=True)
    before2 = jnp.dot(earlier, hot2.astype(BF16), preferred_element_type=F32) + n1
    cnt = cnt_ref[...]
    slot1 = jnp.sum((before1 + cnt) * hot1, axis=1, keepdims=True)
    slot2 = jnp.sum((before2 + cnt) * hot2, axis=1, keepdims=True)
    n_blk = n1 + jnp.sum(hot2, axis=0, keepdims=True)
    lower = (lax.broadcasted_iota(jnp.int32, (ROUTER_LANES, ROUTER_LANES), 0)
             < lax.broadcasted_iota(jnp.int32, (ROUTER_LANES, ROUTER_LANES), 1)).astype(BF16)
    start = jnp.dot(jnp.broadcast_to(n_blk, (8, ROUTER_LANES)).astype(BF16), lower, preferred_element_type=F32)[0:1]
    local1 = jnp.sum((before1 + start) * hot1, axis=1, keepdims=True)
    local2 = jnp.sum((before2 + start) * hot2, axis=1, keepdims=True)
    cnt_ref[...] = cnt + n_blk
    vals = (e1.astype(F32), e2.astype(F32), gate1, gate2, slot1, slot2, local1, local2)
    out = jnp.zeros_like(logits)
    for k, val in enumerate(vals):
        out = jnp.where(lane == k, val, out)
    rt_ref[...] = out
    sub = lax.broadcasted_iota(jnp.int32, (8, ROUTER_LANES), 0)
    tab_ref[...] = jnp.where(sub == 0, n_blk, jnp.where(sub == 1, start, jnp.where(sub == 2, cnt, 0.0)))


def _post_mix(v, ln_g, ln_b, sc, sh, w_router, b_router, tp, dec_seq):
    t, d = v.shape
    tm = TM_ROW
    npb = tp // tm
    cidx = lambda i: jnp.where(i < npb, 0, 1 + (jnp.maximum(i - npb, 0) * tm) // dec_seq)
    row = pl.BlockSpec((tm, d), lambda i: (i, 0))
    vec = pl.BlockSpec((1, d), lambda i: (0, 0))
    cvec = pl.BlockSpec((None, 1, d), lambda i: (cidx(i), 0, 0))
    return pl.pallas_call(
        _post_mix_kernel,
        grid=(t // tm,),
        in_specs=[row, vec, vec, cvec, cvec,
                  pl.BlockSpec((d, ROUTER_LANES), lambda i: (0, 0)),
                  pl.BlockSpec((1, ROUTER_LANES), lambda i: (0, 0))],
        out_specs=[row, row,
                   pl.BlockSpec((tm, ROUTER_LANES), lambda i: (i, 0)),
                   pl.BlockSpec((1, ROUTER_LANES), lambda i: (0, 0)),
                   pl.BlockSpec((None, 8, ROUTER_LANES), lambda i: (i, 0, 0))],
        out_shape=[jax.ShapeDtypeStruct((t, d), F32), jax.ShapeDtypeStruct((t, d), BF16),
                   jax.ShapeDtypeStruct((t, ROUTER_LANES), F32), jax.ShapeDtypeStruct((1, ROUTER_LANES), F32),
                   jax.ShapeDtypeStruct((t // tm, 8, ROUTER_LANES), F32)],
        compiler_params=_params("arbitrary"),
        name="ln_ln_router",
    )(v, ln_g.reshape(1, d), ln_b.reshape(1, d), sc, sh, w_router, b_router)


def _on_parity(blk, fn):
    for par in range(2):
        pl.when(lax.rem(blk, 2) == par)(functools.partial(fn, blk, par))


def _dispatch_kernel(run_n_ref, run_src_ref, run_dst_ref, pad_row_ref, pad_n_ref, nact_ref,
                     h_ref, rt_ref, o_hbm, stage_ref, zero_ref, sems):
    tb = pl.program_id(0)
    n_tb = pl.num_programs(0)
    tm, d = h_ref.shape
    slab = d // 2 // 128
    n_rows_out = o_hbm.shape[0] // slab
    fill_sem = sems.at[2]

    def pieces(wait, src_ref, src_row, dst_row, n, max_size, sem):
        done = jnp.int32(0)
        size = max_size
        while size >= 1:
            take = (n & size) != 0
            src0 = 0 if src_row is None else pl.multiple_of((src_row + done) * slab, slab)
            cp = pltpu.make_async_copy(
                src_ref.at[pl.ds(src0, size * slab), :],
                o_hbm.at[pl.ds(pl.multiple_of((dst_row + done) * slab, slab), size * slab), :], sem)
            pl.when(take)(cp.wait if wait else cp.start)
            done = done + jnp.where(take, size, 0)
            size //= 2

    def runs(wait, blk, par):
        def body(e, carry):
            k = blk * N_EXPERTS + e
            pieces(wait, stage_ref.at[par], run_src_ref[k], run_dst_ref[k], run_n_ref[k], tm, sems.at[par])
            return carry
        lax.fori_loop(0, N_EXPERTS, body, 0)

    def fill(wait):
        def tail(e, carry):
            pieces(wait, zero_ref, None, pad_row_ref[e], pad_n_ref[e], tm // 2, fill_sem)
            return carry

        def spare(b, carry):
            pieces(wait, zero_ref, None, b * tm, jnp.int32(tm), tm, fill_sem)
            return carry

        lax.fori_loop(0, N_EXPERTS, tail, 0)
        lax.fori_loop(nact_ref[0], n_rows_out // tm, spare, 0)

    @pl.when(tb == 0)
    def _():
        zero_ref[...] = jnp.zeros_like(zero_ref)
        fill(False)

    @pl.when(tb >= 2)
    def _():
        _on_parity(tb, lambda blk, par: runs(True, blk - 2, par))

    rt = rt_ref[...]
    local1 = rt[:, 6:7].astype(jnp.int32)
    local2 = rt[:, 7:8].astype(jnp.int32)
    place = lax.broadcasted_iota(jnp.int32, (tm, TOP_K * tm), 1)
    onehot = ((place == local1) | (place == local2)).astype(F32)
    xp = jnp.dot(onehot.T.astype(BF16), h_ref[...], preferred_element_type=F32)
    packed = (lax.bitcast_convert_type(xp[:, d // 2:], jnp.uint32)
              | (lax.bitcast_convert_type(xp[:, :d // 2], jnp.uint32) >> 16))
    _store_as_slabs(stage_ref.at[lax.rem(tb, 2)], packed)
    _on_parity(tb, functools.partial(runs, False))

    @pl.when(tb == n_tb - 1)
    def _():
        _on_parity(tb - 1, functools.partial(runs, True))
        _on_parity(tb, functools.partial(runs, True))
        fill(True)


def _dispatch_rows(h2, route, tables, n_blocks):
    t, d = h2.shape
    tm = TM_ROW
    slab = d // 2 // 128
    assert tm == TM_MOE and t // tm >= 2
    return pl.pallas_call(
        _dispatch_kernel,
        grid_spec=pltpu.PrefetchScalarGridSpec(
            num_scalar_prefetch=len(tables),
            grid=(t // tm,),
            in_specs=[pl.BlockSpec((tm, d), lambda i, *_: (i, 0)),
                      pl.BlockSpec((tm, ROUTER_LANES), lambda i, *_: (i, 0))],
            out_specs=pl.BlockSpec(memory_space=pl.ANY),
            scratch_shapes=[pltpu.VMEM((2, TOP_K * tm * slab, 128), jnp.uint32),
                            pltpu.VMEM((tm * slab, 128), jnp.uint32), pltpu.SemaphoreType.DMA((3,))]),
        out_shape=jax.ShapeDtypeStruct((n_blocks * TM_MOE * slab, 128), jnp.uint32),
        compiler_params=_params("arbitrary"),
        name="moe_dispatch",
    )(*tables, h2, route)


def _expert_up_kernel(se_ref, sj_ref, sb_ref, oj_ref, ob_ref, fst_ref, gp_ref, nxt_ref, ne_ref, nj_ref, ns_ref,
                      x_ref, wg_hbm, wu_hbm, o_ref, wg_st, wu_st, sems):
    live = pl.program_id(0) < ns_ref[0]
    _stream_group_weights(se_ref, sj_ref, fst_ref, gp_ref, nxt_ref, ne_ref, nj_ref, live,
                          ((wg_hbm, wg_st), (wu_hbm, wu_st)), sems)

    @pl.when(live)
    def _():
        x = _unpack_bf16_pairs(_load_from_slabs(x_ref, TM_MOE))
        slot = gp_ref[pl.program_id(0)]
        a = jnp.dot(x, wg_st[slot], preferred_element_type=F32)
        u = jnp.dot(x, wu_st[slot], preferred_element_type=F32)
        o_ref[...] = ((a * _sigmoid(a)) * u).astype(o_ref.dtype)

    @pl.when(jnp.logical_not(live))
    def _():
        o_ref[...] = jnp.zeros_like(o_ref)


def _stream_group_weights(se_ref, sj_ref, fst_ref, gp_ref, nxt_ref, ne_ref, nj_ref, live, weights, sems):
    s = pl.program_id(0)

    def copies(e, j, slot):
        out = []
        for k, (w_hbm, w_st) in enumerate(weights):
            tn = w_st.shape[2]
            src = w_hbm.at[e, :, pl.ds(pl.multiple_of(j * tn, tn), tn)]
            out.append(pltpu.make_async_copy(src, w_st.at[slot], sems.at[k, slot]))
        return out

    def first_step(slot):
        cur = copies(se_ref[s], sj_ref[s], slot)

        @pl.when(s == 0)
        def _():
            for cp in cur:
                cp.start()

        @pl.when(nxt_ref[s] == 1)
        def _():
            for cp in copies(ne_ref[s], nj_ref[s], 1 - slot):
                cp.start()

        for cp in cur:
            cp.wait()

    for slot in range(2):
        pl.when(live & (fst_ref[s] == 1) & (gp_ref[s] == slot))(functools.partial(first_step, slot))


_N_SCHED = 11


def _sched_map(fn):
    return lambda s, *refs: fn(s, *refs[:_N_SCHED])


def _expert_up(xs, w_gate, w_up, sched, n_steps):
    d, de = w_gate.shape[1], w_gate.shape[2]
    tm, tf = TM_MOE, TF_MOE
    slab = d // 2 // 128
    r = xs.shape[0] // slab
    return pl.pallas_call(
        _expert_up_kernel,
        grid_spec=pltpu.PrefetchScalarGridSpec(
            num_scalar_prefetch=_N_SCHED,
            grid=(n_steps,),
            in_specs=[pl.BlockSpec((tm * slab, 128), _sched_map(lambda s, se, sj, sb, *_: (sb[s], 0))),
                      pl.BlockSpec(memory_space=pl.ANY), pl.BlockSpec(memory_space=pl.ANY)],
            out_specs=pl.BlockSpec((tm, tf), _sched_map(lambda s, se, sj, sb, oj, ob, *_: (ob[s], oj[s]))),
            scratch_shapes=[pltpu.VMEM((2, d, tf), F32), pltpu.VMEM((2, d, tf), F32),
                            pltpu.SemaphoreType.DMA((2, 2))]),
        out_shape=jax.ShapeDtypeStruct((r, de), BF16),
        compiler_params=_params("arbitrary"),
        name="moe_gate_up",
    )(*sched, xs, w_gate, w_up)


def _expert_down_kernel(se_ref, sj_ref, sb_ref, oj_ref, ob_ref, fst_ref, gp_ref, nxt_ref, ne_ref, nj_ref, ns_ref,
                        h_ref, wd_hbm, o_ref, wd_st, sems):
    live = pl.program_id(0) < ns_ref[0]
    _stream_group_weights(se_ref, sj_ref, fst_ref, gp_ref, nxt_ref, ne_ref, nj_ref, live, ((wd_hbm, wd_st),), sems)

    @pl.when(live)
    def _():
        y = jnp.dot(h_ref[...].astype(F32), wd_st[gp_ref[pl.program_id(0)]], preferred_element_type=F32)
        _store_as_slabs(o_ref, _pack_bf16_pairs(y))

    @pl.when(jnp.logical_not(live))
    def _():
        o_ref[...] = jnp.zeros_like(o_ref)


def _expert_down(hid, w_down, sched, n_steps):
    r, de = hid.shape
    d = w_down.shape[2]
    tm = TM_MOE
    slab = d // 2 // 128
    return pl.pallas_call(
        _expert_down_kernel,
        grid_spec=pltpu.PrefetchScalarGridSpec(
            num_scalar_prefetch=_N_SCHED,
            grid=(n_steps,),
            in_specs=[pl.BlockSpec((tm, de), _sched_map(lambda s, se, sj, sb, *_: (sb[s], 0))),
                      pl.BlockSpec(memory_space=pl.ANY)],
            out_specs=pl.BlockSpec((tm * slab, 128), _sched_map(lambda s, se, sj, sb, oj, ob, *_: (ob[s], 0))),
            scratch_shapes=[pltpu.VMEM((2, de, d), F32), pltpu.SemaphoreType.DMA((1, 2))]),
        out_shape=jax.ShapeDtypeStruct((r * slab, 128), jnp.uint32),
        compiler_params=_params("arbitrary"),
        name="moe_down",
    )(*sched, hid, w_down)


def _combine_kernel(npb, alpha, pos_ref, y_hbm, x1_ref, rt_ref, gt_ref, g_ref, b_ref,
                    op_ref, os_ref, buf_ref, sems):
    i = pl.program_id(0)
    n = pl.num_programs(0)
    tm = x1_ref.shape[0]
    slab = buf_ref.shape[2] // tm

    def rows(wait, blk, par):
        def body(g, carry):
            for u in range(ROW_UNROLL):
                r = g * ROW_UNROLL + u
                for k in range(TOP_K):
                    cp = _slab_copy(y_hbm, buf_ref.at[par, k], pos_ref[(blk * tm + r) * TOP_K + k], r, slab,
                                    sems.at[par])
                    cp.wait() if wait else cp.start(priority=k % 2)
            return carry
        lax.fori_loop(0, tm // ROW_UNROLL, body, 0)

    @pl.when(i == 0)
    def _():
        rows(False, 0, 0)

    @pl.when(i + 1 < n)
    def _():
        _on_parity(i + 1, functools.partial(rows, False))

    _on_parity(i, functools.partial(rows, True))
    rt = rt_ref[...]
    par = lax.rem(i, 2)
    expert_rows = lambda k: _unpack_bf16_pairs(_load_from_slabs(buf_ref.at[par, k], tm))
    f = rt[:, 2:3] * expert_rows(0) + rt[:, 3:4] * expert_rows(1)
    out = _ln_rows(alpha * x1_ref[...] + gt_ref[...] * f) * g_ref[...] + b_ref[...]

    @pl.when(i < npb)
    def _():
        op_ref[...] = out

    @pl.when(i >= npb)
    def _():
        os_ref[...] = out


def _combine(yb, pos, x1, route, gt, ln_g, ln_b, alpha, tp, dec_seq):
    t, d = x1.shape
    tm = TM_ROW
    npb = tp // tm
    cidx = lambda i: jnp.where(i < npb, 0, 1 + (jnp.maximum(i - npb, 0) * tm) // dec_seq)
    vec = pl.BlockSpec((1, d), lambda i, *_: (0, 0))
    return pl.pallas_call(
        functools.partial(_combine_kernel, npb, alpha),
        grid_spec=pltpu.PrefetchScalarGridSpec(
            num_scalar_prefetch=1,
            grid=(t // tm,),
            in_specs=[pl.BlockSpec(memory_space=pl.ANY),
                      pl.BlockSpec((tm, d), lambda i, *_: (i, 0)),
                      pl.BlockSpec((tm, ROUTER_LANES), lambda i, *_: (i, 0)),
                      pl.BlockSpec((None, 1, d), lambda i, *_: (cidx(i), 0, 0)),
                      vec, vec],
            out_specs=[pl.BlockSpec((tm, d), lambda i, *_: (jnp.minimum(i, npb - 1), 0)),
                       pl.BlockSpec((tm, d), lambda i, *_: (jnp.maximum(i - npb, 0), 0))],
            scratch_shapes=[pltpu.VMEM((2, TOP_K, tm * (d // 2 // 128), 128), jnp.uint32),
                            pltpu.SemaphoreType.DMA((2,))]),
        out_shape=[jax.ShapeDtypeStruct((tp, d), F32), jax.ShapeDtypeStruct((t - tp, d), F32)],
        compiler_params=_params("arbitrary"),
        name="moe_combine_ln",
    )(pos, yb, x1, route, gt, ln_g.reshape(1, d), ln_b.reshape(1, d))


def _dispatch_plan(route, counts_f, block_tab, n_tok):
    tm = TM_MOE
    n_blocks = n_tok * TOP_K // tm + N_EXPERTS
    i32 = jnp.int32
    eid = route[:, 0:TOP_K].astype(i32).reshape(-1)
    slot = route[:, 4:4 + TOP_K].astype(i32).reshape(-1)
    counts = counts_f[0, :N_EXPERTS].astype(i32)
    nb = (counts + tm - 1) // tm
    nb_end = jnp.cumsum(nb)
    bs = nb_end - nb
    n_act = nb_end[-1]
    experts = jnp.arange(N_EXPERTS, dtype=i32)
    look = lambda table, e: jnp.sum(jnp.where(e[:, None] == experts[None, :], table[None, :], 0), axis=1)
    tab = block_tab[:, :3, :N_EXPERTS].astype(i32)
    run_dst = (bs * tm)[None, :] + tab[:, 2]
    dispatch_tables = (tab[:, 0].reshape(-1), tab[:, 1].reshape(-1), run_dst.reshape(-1),
                       bs * tm + counts, nb * tm - counts, n_act.reshape(1))

    def schedule(n_inner):
        n_steps = n_inner * n_blocks
        n_live = n_inner * n_act

        def decode(step):
            s = jnp.minimum(step, n_live - 1)
            e = jnp.minimum(jnp.sum((s[:, None] >= n_inner * nb_end[None, :]).astype(i32), axis=1), N_EXPERTS - 1)
            nbe = jnp.maximum(look(nb, e), 1)
            loc = s - n_inner * look(bs, e)
            return e, loc // nbe, loc % nbe, nbe, look(bs, e)

        step = jnp.arange(n_steps, dtype=i32)
        e, sj, bi, nbe, bse = decode(step)
        live = step < n_live
        nxt_step = jnp.minimum(step, n_live - 1) + nbe - bi
        ne, nj, _, _, _ = decode(nxt_step)
        fst = (live & (bi == 0)).astype(i32)
        gp = (jnp.cumsum(fst) - 1) % 2
        nxt = (live & (bi == 0) & (nxt_step < n_live)).astype(i32)
        spare = jnp.maximum(step - n_live, 0)
        oj = jnp.where(live, sj, spare % n_inner)
        ob = jnp.where(live, bse + bi, n_act + spare // n_inner)
        sched = (e, sj, bse + bi, oj, ob, fst, gp, nxt, ne, nj, n_live.reshape(1))
        return tuple(a.astype(i32) for a in sched), n_steps

    pos = look(bs * tm, eid) + slot
    return pos, dispatch_tables, n_blocks, schedule


def kernel(x_prompt, x_sample, state_ret_fwd, state_ret_bwd, c, c_ctx, w_mod, b_mod, w_in, conv_w, conv_b, conv_ln_g, conv_ln_b, ret_decay_fwd, ret_decay_bwd, ret_gn_g, w_out, ln1_g, ln1_b, w_grp, b_grp, w_exp, b_exp, w_gate, w_up, w_down, ln2_g, ln2_b):
    depth = w_mod.shape[0]
    assert depth == 1, "single-layer step"
    bp, sp, d = x_prompt.shape
    bs_, ss, _ = x_sample.shape
    tp, ts = bp * sp, bs_ * ss
    t = tp + ts
    conv_width = conv_w.shape[2]
    ret_width = ret_gn_g.shape[1]
    assert ret_width == RET_HEADS * RET_D and sp % CHUNK == 0 and ss % CHUNK == 0
    alpha = (2.0 * depth) ** 0.25

    xp = x_prompt.reshape(tp, d)
    xs = x_sample.reshape(ts, d)
    cond8 = jnp.zeros((8, d), F32).at[0].set(c_ctx).at[1:1 + bs_].set(c)
    m = _modulation(cond8, w_mod[0], b_mod[0])
    sh1, sc1, gt1, sh2, sc2, gt2 = [m[:, k * d:(k + 1) * d].reshape(8, 1, d) for k in range(6)]

    h = _ln_modulate(xp, xs, sc1, sh1, ss)
    u_glu = _glu_proj(h, w_in[0], conv_width)
    cos_t, sin_t = _rope_tables(ss, RET_D)
    qkvg = _qkvg_proj(h, w_in[0], cos_t, sin_t, 2 * conv_width, 4 * ret_width, tp)

    u = _conv_module(u_glu, conv_w[0], conv_b[0], conv_ln_g[0], conv_ln_b[0], tp, ss)

    lg = jnp.stack([jax.nn.log_sigmoid(ret_decay_fwd[0].astype(F32)),
                    jax.nn.log_sigmoid(ret_decay_bwd[0].astype(F32))])
    gn = ret_gn_g[0].reshape(1, ret_width)
    r_p, new_f, new_b = _retention(qkvg, lg, gn, sp, 0, bp)
    r_s = _retention(qkvg, lg, gn, ss, tp // ss, bs_, state_ret_fwd, state_ret_bwd)
    r = jnp.concatenate([r_p, r_s], axis=0)

    v = _out_proj(u, r, w_out[0], xp, xs, gt1, alpha, ss)

    w_router = jnp.zeros((d, ROUTER_LANES), F32).at[:, :N_GROUPS].set(w_grp[0]).at[:, N_GROUPS:N_GROUPS + N_EXPERTS].set(w_exp[0]).astype(BF16)
    b_router = jnp.zeros((1, ROUTER_LANES), F32).at[0, :N_GROUPS].set(b_grp[0]).at[0, N_GROUPS:N_GROUPS + N_EXPERTS].set(b_exp[0])
    x1, h2, route, counts, block_tab = _post_mix(v, ln1_g[0], ln1_b[0], sc2, sh2, w_router, b_router, tp, ss)

    pos, dispatch_tables, n_blocks, schedule = _dispatch_plan(route, counts, block_tab, t)
    xg = _dispatch_rows(h2, route, dispatch_tables, n_blocks)
    sched_up, n_up = schedule(w_gate.shape[3] // TF_MOE)
    hid = _expert_up(xg, w_gate[0], w_up[0], sched_up, n_up)
    sched_dn, n_dn = schedule(1)
    yb = _expert_down(hid, w_down[0], sched_dn, n_dn)

    out_p, out_s = _combine(yb, pos, x1, route, gt2, ln2_g[0], ln2_b[0], alpha, tp, ss)
    return (out_p.reshape(bp, sp, d), out_s.reshape(bs_, ss, d), new_f, new_b)
```

```python
import functools

import numpy as np
import jax
import jax.numpy as jnp
from jax import lax
from jax.experimental import pallas as pl
from jax.experimental.pallas import tpu as pltpu

F32 = jnp.float32
BF16 = jnp.bfloat16

LN_EPS = 1e-5
CONV_K = 31
CONV_HALO = 16
RET_HEADS = 8
RET_D = 256
HEADS_PER_STEP = 4
CHUNK = 128
GRID_W = 64
ROPE_BASE = 10000.0
N_GROUPS = 4
EXPERTS_PER_GROUP = 8
N_EXPERTS = N_GROUPS * EXPERTS_PER_GROUP
TOP_K = 2
ROUTER_LANES = 128
VMEM_LIMIT = 56 * 1024 * 1024
TM_PROJ = 1024
TM_ROW = 256
TM_MOE = 256
TF_MOE = 512
ROW_UNROLL = 8


def _params(*sem):
    return pltpu.CompilerParams(dimension_semantics=tuple(sem), vmem_limit_bytes=VMEM_LIMIT)


def _sigmoid(x):
    return 1.0 / (1.0 + jnp.exp(-x))


def _ln_rows(x):
    mu = jnp.mean(x, axis=-1, keepdims=True)
    xc = x - mu
    var = jnp.mean(xc * xc, axis=-1, keepdims=True)
    return xc * lax.rsqrt(var + LN_EPS)


def _mod_kernel(c_ref, w_ref, b_ref, o_ref):
    c = c_ref[...]
    s = c * _sigmoid(c)
    o_ref[...] = jnp.dot(s.astype(BF16), w_ref[...].astype(BF16), preferred_element_type=F32) + b_ref[...]


def _modulation(cond8, w_mod, b_mod):
    d, n = w_mod.shape
    tn = 512
    return pl.pallas_call(
        _mod_kernel,
        grid=(n // tn,),
        in_specs=[pl.BlockSpec((8, d), lambda j: (0, 0)),
                  pl.BlockSpec((d, tn), lambda j: (0, j)),
                  pl.BlockSpec((1, tn), lambda j: (0, j))],
        out_specs=pl.BlockSpec((8, tn), lambda j: (0, j)),
        out_shape=jax.ShapeDtypeStruct((8, n), F32),
        compiler_params=_params("arbitrary"),
        name="modulation",
    )(cond8, w_mod, b_mod.reshape(1, n))


def _ln_mod_kernel(npb, xp_ref, xs_ref, sc_ref, sh_ref, o_ref):
    def body(x_ref):
        y = _ln_rows(x_ref[...])
        o_ref[...] = (y * (1.0 + sc_ref[...]) + sh_ref[...]).astype(o_ref.dtype)

    i = pl.program_id(0)
    pl.when(i < npb)(lambda: body(xp_ref))
    pl.when(i >= npb)(lambda: body(xs_ref))


def _ln_modulate(xp, xs, sc, sh, dec_seq):
    (tp, d), ts = xp.shape, xs.shape[0]
    tm = 512
    npb, nsb = tp // tm, ts // tm
    cidx = lambda i: jnp.where(i < npb, 0, 1 + (jnp.maximum(i - npb, 0) * tm) // dec_seq)
    return pl.pallas_call(
        functools.partial(_ln_mod_kernel, npb),
        grid=(npb + nsb,),
        in_specs=[pl.BlockSpec((tm, d), lambda i: (jnp.minimum(i, npb - 1), 0)),
                  pl.BlockSpec((tm, d), lambda i: (jnp.maximum(i - npb, 0), 0)),
                  pl.BlockSpec((None, 1, d), lambda i: (cidx(i), 0, 0)),
                  pl.BlockSpec((None, 1, d), lambda i: (cidx(i), 0, 0))],
        out_specs=pl.BlockSpec((tm, d), lambda i: (i, 0)),
        out_shape=jax.ShapeDtypeStruct((tp + ts, d), BF16),
        compiler_params=_params("arbitrary"),
        name="ln_modulate",
    )(xp, xs, sc, sh)


def _glu_proj_kernel(h_ref, wv_ref, wg_ref, o_ref):
    h = h_ref[...]
    a = jnp.dot(h, wv_ref[...].astype(BF16), preferred_element_type=F32)
    g = jnp.dot(h, wg_ref[...].astype(BF16), preferred_element_type=F32)
    o_ref[...] = a * _sigmoid(g)


def _glu_proj(h, w_in, conv_width):
    t, d = h.shape
    tm, tn = TM_PROJ, 256
    goff = conv_width // tn
    return pl.pallas_call(
        _glu_proj_kernel,
        grid=(t // tm, conv_width // tn),
        in_specs=[pl.BlockSpec((tm, d), lambda i, j: (i, 0)),
                  pl.BlockSpec((d, tn), lambda i, j: (0, j)),
                  pl.BlockSpec((d, tn), lambda i, j: (0, j + goff))],
        out_specs=pl.BlockSpec((tm, tn), lambda i, j: (i, j)),
        out_shape=jax.ShapeDtypeStruct((t, conv_width), F32),
        compiler_params=_params("arbitrary", "arbitrary"),
        name="glu_proj",
    )(h, w_in, w_in)


def _qkvg_proj_kernel(npb, tiles_per_part, k_scale, h_ref, w_ref, cos_ref, sin_ref, o_ref):
    i, j = pl.program_id(0), pl.program_id(1)
    z = jnp.dot(h_ref[...], w_ref[...].astype(BF16), preferred_element_type=F32)
    tpp = tiles_per_part

    @pl.when(j >= 3 * tpp)
    def _():
        o_ref[...] = (z * _sigmoid(z)).astype(o_ref.dtype)

    @pl.when((j >= 2 * tpp) & (j < 3 * tpp))
    def _():
        o_ref[...] = z.astype(o_ref.dtype)

    @pl.when(j < 2 * tpp)
    def _():
        zz = z * jnp.where(j >= tpp, k_scale, 1.0).astype(F32)

        @pl.when(i < npb)
        def _():
            o_ref[...] = zz.astype(o_ref.dtype)

        @pl.when(i >= npb)
        def _():
            for s in range(zz.shape[1] // 128):
                cs = slice(s * 128, (s + 1) * 128)
                ts_ = slice((s * 128) % RET_D, (s * 128) % RET_D + 128)
                zs = zz[:, cs]
                o_ref[:, cs] = (zs * cos_ref[:, ts_] + pltpu.roll(zs, 64, axis=1) * sin_ref[:, ts_]).astype(o_ref.dtype)


def _qkvg_proj(h, w_in, cos_t, sin_t, col0, n_cols, tp):
    t, d = h.shape
    tm, tn = TM_PROJ, 512
    assert cos_t.shape == (tm, RET_D), "a projection row tile is one latent sequence"
    npb = tp // tm
    tpp = (n_cols // 4) // tn
    return pl.pallas_call(
        functools.partial(_qkvg_proj_kernel, npb, tpp, RET_D ** -0.5),
        grid=(t // tm, n_cols // tn),
        in_specs=[pl.BlockSpec((tm, d), lambda i, j: (i, 0)),
                  pl.BlockSpec((d, tn), lambda i, j: (0, j + col0 // tn)),
                  pl.BlockSpec((tm, RET_D), lambda i, j: (0, 0)),
                  pl.BlockSpec((tm, RET_D), lambda i, j: (0, 0))],
        out_specs=pl.BlockSpec((tm, tn), lambda i, j: (i, j)),
        out_shape=jax.ShapeDtypeStruct((t, n_cols), BF16),
        compiler_params=_params("arbitrary", "arbitrary"),
        name="qkvg_proj",
    )(h, w_in, cos_t, sin_t)


def _rope_tables(n_tok, width):
    rows = n_tok // GRID_W
    r_idx = np.repeat(np.arange(rows), GRID_W).astype(np.float64)
    c_idx = np.tile(np.arange(GRID_W), rows).astype(np.float64)
    dq = RET_D // 2
    inv = ROPE_BASE ** (-np.arange(dq // 2, dtype=np.float64) / (dq // 2))
    sign = np.concatenate([-np.ones(dq // 2), np.ones(dq // 2)])
    cos_h, sin_h = [], []
    for idx in (r_idx, c_idx):
        ang = idx[:, None] * inv
        cos_h.append(np.concatenate([np.cos(ang), np.cos(ang)], axis=1))
        sin_h.append(np.concatenate([np.sin(ang), np.sin(ang)], axis=1) * sign)
    cos_h, sin_h = np.concatenate(cos_h, axis=1), np.concatenate(sin_h, axis=1)
    reps = width // RET_D
    return (jnp.asarray(np.tile(cos_h, (1, reps)), F32), jnp.asarray(np.tile(sin_h, (1, reps)), F32))


def _conv_kernel(npb, tiles_per_seq, uc_ref, up_ref, un_ref, w_ref, b_ref, g_ref, bt_ref, o_ref, buf_ref, y_ref):
    r = pl.program_id(0)
    tm = uc_ref.shape[0]
    ncg = buf_ref.shape[0]
    t = lax.rem(jnp.maximum(r - npb, 0), tiles_per_seq)
    is_s = r >= npb
    has_prev = is_s & (t != 0)
    has_next = is_s & (t != tiles_per_seq - 1)
    for cg in range(ncg):
        cs = slice(cg * 128, (cg + 1) * 128)
        buf_ref[cg, 0:CONV_HALO, :] = jnp.where(has_prev, up_ref[:, cs], 0.0)
        buf_ref[cg, CONV_HALO:CONV_HALO + tm, :] = uc_ref[:, cs]
        buf_ref[cg, CONV_HALO + tm:, :] = jnp.where(has_next, un_ref[:, cs], 0.0)

    off = CONV_HALO - CONV_K // 2

    def body(cg, carry):
        acc = jnp.zeros((tm, 128), F32)
        for tap in range(CONV_K):
            acc = acc + buf_ref[cg, off + tap:off + tap + tm, :] * w_ref[cg, tap:tap + 1, :]
        y_ref[cg] = acc + b_ref[cg]
        return carry

    lax.fori_loop(0, ncg, body, 0)

    n_ch = ncg * 128
    tot = y_ref[0]
    for cg in range(1, ncg):
        tot = tot + y_ref[cg]
    mu = jnp.sum(tot, axis=1, keepdims=True) * (1.0 / n_ch)
    sq = jnp.zeros((tm, 128), F32)
    for cg in range(ncg):
        dv = y_ref[cg] - mu
        sq = sq + dv * dv
    var = jnp.sum(sq, axis=1, keepdims=True) * (1.0 / n_ch)
    rstd = lax.rsqrt(var + LN_EPS)
    for cg in range(ncg):
        cs = slice(cg * 128, (cg + 1) * 128)
        v = (y_ref[cg] - mu) * rstd * g_ref[cg] + bt_ref[cg]
        o_ref[:, cs] = (v * _sigmoid(v)).astype(o_ref.dtype)


def _conv_module(u, conv_w, conv_b, ln_g, ln_b, tp, dec_seq):
    t, c = u.shape
    tm = TM_ROW
    ncg = c // 128
    npb = tp // tm
    hb = tm // CONV_HALO
    n_halo_blocks = t // CONV_HALO
    w3 = jnp.zeros((32, c), F32).at[:CONV_K].set(conv_w).reshape(32, ncg, 128).transpose(1, 0, 2)
    vec = lambda a: a.reshape(ncg, 1, 128)
    return pl.pallas_call(
        functools.partial(_conv_kernel, npb, dec_seq // tm),
        grid=(t // tm,),
        in_specs=[pl.BlockSpec((tm, c), lambda r: (r, 0)),
                  pl.BlockSpec((CONV_HALO, c), lambda r: (jnp.maximum(r * hb - 1, 0), 0)),
                  pl.BlockSpec((CONV_HALO, c), lambda r: (jnp.minimum((r + 1) * hb, n_halo_blocks - 1), 0)),
                  pl.BlockSpec((ncg, 32, 128), lambda r: (0, 0, 0)),
                  pl.BlockSpec((ncg, 1, 128), lambda r: (0, 0, 0)),
                  pl.BlockSpec((ncg, 1, 128), lambda r: (0, 0, 0)),
                  pl.BlockSpec((ncg, 1, 128), lambda r: (0, 0, 0))],
        out_specs=pl.BlockSpec((tm, c), lambda r: (r, 0)),
        out_shape=jax.ShapeDtypeStruct((t, c), BF16),
        scratch_shapes=[pltpu.VMEM((ncg, tm + 2 * CONV_HALO, 128), F32),
                        pltpu.VMEM((ncg, tm, 128), F32)],
        compiler_params=_params("arbitrary"),
        name="conv_ln_swish",
    )(u, u, u, w3, vec(conv_b), vec(ln_g), vec(ln_b))


def _retention_kernel(nc, has_init, lg_ref, q_ref, k_ref, v_ref, g_ref, gn_ref, *rest):
    if has_init:
        s0f_ref, s0b_ref, r_ref, o_ref, sf_ref, sb_ref = rest
    else:
        s0f_ref = s0b_ref = None
        r_ref, sf_ref, sb_ref, o_ref = rest
    for hh in range(HEADS_PER_STEP):
        cols = slice(hh * RET_D, (hh + 1) * RET_D)
        _retention_head(nc, lg_ref, pl.program_id(1) * HEADS_PER_STEP + hh,
                        q_ref.at[:, cols], k_ref.at[:, cols], v_ref.at[:, cols], g_ref.at[:, cols],
                        gn_ref.at[:, cols], None if s0f_ref is None else s0f_ref.at[hh],
                        None if s0b_ref is None else s0b_ref.at[hh],
                        r_ref.at[:, cols], sf_ref.at[hh], sb_ref.at[hh], o_ref.at[hh])


def _retention_head(nc, lg_ref, hd, q_ref, k_ref, v_ref, g_ref, gn_ref, s0f_ref, s0b_ref, r_ref, sf_ref, sb_ref, o_ref):
    has_init = s0f_ref is not None
    lgf, lgb = lg_ref[0, hd], lg_ref[1, hd]
    c = CHUNK
    row = lax.broadcasted_iota(jnp.int32, (c, c), 0)
    col = lax.broadcasted_iota(jnp.int32, (c, c), 1)
    diff = (row - col).astype(F32)
    dec = (jnp.where(diff >= 0, jnp.exp(jnp.maximum(diff, 0.0) * lgf), 0.0)
           + jnp.where(diff <= 0, jnp.exp(jnp.maximum(-diff, 0.0) * lgb), 0.0))
    pos = lax.broadcasted_iota(jnp.int32, (c, 1), 0).astype(F32)
    xi_f = jnp.exp((pos + 1.0) * lgf)
    zeta_f = jnp.exp((c - 1.0 - pos) * lgf)
    xi_b = jnp.exp((c - pos) * lgb)
    zeta_b = jnp.exp(pos * lgb)
    gch_f = jnp.exp(jnp.full((1, RET_D), c, F32) * lgf)
    gch_b = jnp.exp(jnp.full((1, RET_D), c, F32) * lgb)

    if has_init:
        sf_ref[...] = s0f_ref[...]
        sb_ref[...] = s0b_ref[...]
    else:
        sf_ref[...] = jnp.zeros_like(sf_ref)
        sb_ref[...] = jnp.zeros_like(sb_ref)

    def chunk(ci):
        sl = slice(ci * c, (ci + 1) * c)
        return q_ref[sl, :], k_ref[sl, :], v_ref[sl, :], sl

    def state_update(s_ref, kc, vc, zeta, gch):
        kz = (kc.astype(F32) * zeta).T.astype(BF16)
        s_ref[...] = gch * s_ref[...] + jnp.dot(kz, vc, preferred_element_type=F32)

    for ci in range(nc):
        qc, kc, vc, sl = chunk(ci)
        s = lax.dot_general(qc, kc, (((1,), (1,)), ((), ())), preferred_element_type=F32)
        p = (s * dec).astype(BF16)
        o = jnp.dot(p, vc, preferred_element_type=F32)
        o = o + jnp.dot(qc, sf_ref[...].astype(BF16), preferred_element_type=F32) * xi_f
        o_ref[sl, :] = o
        state_update(sf_ref, kc, vc, zeta_f, gch_f)

    for ci in reversed(range(nc)):
        qc, kc, vc, sl = chunk(ci)
        o_ref[sl, :] = o_ref[sl, :] + jnp.dot(qc, sb_ref[...].astype(BF16), preferred_element_type=F32) * xi_b
        state_update(sb_ref, kc, vc, zeta_b, gch_b)

    y = _ln_rows(o_ref[...])
    r_ref[...] = (g_ref[...].astype(F32) * (y * gn_ref[...])).astype(r_ref.dtype)


def _retention(qkvg, lg, gn, n, row_blk0, n_seq, s0f=None, s0b=None):
    has_init = s0f is not None
    nh, dd, hps = RET_HEADS, RET_D, HEADS_PER_STEP
    wd = hps * dd
    col = lambda part: (lambda b, h: (b + row_blk0, part * (nh // hps) + h))
    st_spec = pl.BlockSpec((None, None, hps, dd, dd), lambda b, h: (b, 0, h, 0, 0))
    in_specs = [pl.BlockSpec(memory_space=pltpu.SMEM),
                pl.BlockSpec((n, wd), col(0)), pl.BlockSpec((n, wd), col(1)),
                pl.BlockSpec((n, wd), col(2)), pl.BlockSpec((n, wd), col(3)),
                pl.BlockSpec((1, wd), lambda b, h: (0, h))]
    args = [lg, qkvg, qkvg, qkvg, qkvg, gn]
    r_shape = jax.ShapeDtypeStruct((n_seq * n, nh * dd), BF16)
    r_spec = pl.BlockSpec((n, wd), lambda b, h: (b, h))
    st_scratch = pltpu.VMEM((hps, dd, dd), F32)
    if has_init:
        in_specs += [st_spec, st_spec]
        args += [s0f, s0b]
        out_specs, out_shape = r_spec, r_shape
        scratch = [pltpu.VMEM((hps, n, dd), F32), st_scratch, st_scratch]
    else:
        st_shape = jax.ShapeDtypeStruct((n_seq, 1, nh, dd, dd), F32)
        out_specs, out_shape = [r_spec, st_spec, st_spec], [r_shape, st_shape, st_shape]
        scratch = [pltpu.VMEM((hps, n, dd), F32)]
    return pl.pallas_call(
        functools.partial(_retention_kernel, n // CHUNK, has_init),
        grid=(n_seq, nh // hps),
        in_specs=in_specs, out_specs=out_specs, out_shape=out_shape,
        scratch_shapes=scratch,
        compiler_params=_params("arbitrary", "arbitrary"),
        name="retention_latent" if has_init else "retention_context",
    )(*args)


def _out_proj_kernel(npb, alpha, u_ref, r_ref, w1_ref, w2_ref, xp_ref, xs_ref, gta_ref, gtb_ref, o_ref):
    y = (jnp.dot(u_ref[...], w1_ref[...].astype(BF16), preferred_element_type=F32)
         + jnp.dot(r_ref[...], w2_ref[...].astype(BF16), preferred_element_type=F32))
    i = pl.program_id(0)
    half = y.shape[0] // 2

    def write(x_ref):
        o_ref[:half, :] = alpha * x_ref[:half, :] + gta_ref[...] * y[:half]
        o_ref[half:, :] = alpha * x_ref[half:, :] + gtb_ref[...] * y[half:]

    pl.when(i < npb)(lambda: write(xp_ref))
    pl.when(i >= npb)(lambda: write(xs_ref))


def _out_proj(u, r, w_out, xp, xs, gt, alpha, dec_seq):
    t, kh = u.shape
    d = w_out.shape[1]
    tm, tn = 2 * TM_PROJ, 256
    assert (tm // 2) % dec_seq == 0 or dec_seq % (tm // 2) == 0
    npb = xp.shape[0] // tm
    cidx = lambda i, half: jnp.where(i < npb, 0, 1 + (jnp.maximum(i - npb, 0) * tm + half * (tm // 2)) // dec_seq)
    once = pl.Buffered(1)
    return pl.pallas_call(
        functools.partial(_out_proj_kernel, npb, alpha),
        grid=(t // tm, d // tn),
        in_specs=[pl.BlockSpec((tm, kh), lambda i, j: (i, 0), pipeline_mode=once),
                  pl.BlockSpec((tm, kh), lambda i, j: (i, 0), pipeline_mode=once),
                  pl.BlockSpec((kh, tn), lambda i, j: (0, j)),
                  pl.BlockSpec((kh, tn), lambda i, j: (1, j)),
                  pl.BlockSpec((tm, tn), lambda i, j: (jnp.minimum(i, npb - 1), j)),
                  pl.BlockSpec((tm, tn), lambda i, j: (jnp.maximum(i - npb, 0), j)),
                  pl.BlockSpec((None, 1, tn), lambda i, j: (cidx(i, 0), 0, j)),
                  pl.BlockSpec((None, 1, tn), lambda i, j: (cidx(i, 1), 0, j))],
        out_specs=pl.BlockSpec((tm, tn), lambda i, j: (i, j)),
        out_shape=jax.ShapeDtypeStruct((t, d), F32),
        compiler_params=_params("arbitrary", "arbitrary"),
        name="out_proj_residual",
    )(u, r, w_out, w_out, xp, xs, gt, gt)


def _pack_bf16_pairs(x):
    n = x.shape[1] // 2
    lo = lax.bitcast_convert_type(x[:, :n].astype(BF16).astype(F32), jnp.uint32)
    hi = lax.bitcast_convert_type(x[:, n:].astype(BF16).astype(F32), jnp.uint32)
    return hi | (lo >> 16)


def _unpack_bf16_pairs(p):
    lo = lax.bitcast_convert_type(p << 16, F32)
    hi = lax.bitcast_convert_type(p & jnp.uint32(0xFFFF0000), F32)
    return jnp.concatenate([lo, hi], axis=1)


def _store_as_slabs(ref, rows):
    m, n = rows.shape
    slab = n // 128
    for s in range(slab):
        ref[pl.ds(s, m, stride=slab), :] = rows[:, s * 128:(s + 1) * 128]


def _load_from_slabs(ref, m):
    slab = ref.shape[0] // m
    return jnp.concatenate([ref[pl.ds(s, m, stride=slab), :] for s in range(slab)], axis=1)


def _slab_copy(src, dst, src_row, dst_row, slab, sem):
    return pltpu.make_async_copy(src.at[pl.ds(pl.multiple_of(src_row * slab, slab), slab), :],
                                 dst.at[pl.ds(pl.multiple_of(dst_row * slab, slab), slab), :], sem)


def _post_mix_kernel(v_ref, g1_ref, b1_ref, sc_ref, sh_ref, wr_ref, br_ref, x1_ref, h2_ref, rt_ref, cnt_ref, tab_ref):
    @pl.when(pl.program_id(0) == 0)
    def _():
        cnt_ref[...] = jnp.zeros_like(cnt_ref)

    x1 = _ln_rows(v_ref[...]) * g1_ref[...] + b1_ref[...]
    x1_ref[...] = x1
    h2 = (_ln_rows(x1) * (1.0 + sc_ref[...]) + sh_ref[...]).astype(BF16)
    h2_ref[...] = h2
    logits = jnp.dot(h2, wr_ref[...], preferred_element_type=F32) + br_ref[...]
    lane = lax.broadcasted_iota(jnp.int32, logits.shape, 1)
    big = jnp.int32(ROUTER_LANES)
    neg = jnp.float32(-jnp.inf)

    def first_lane_of_max(vals):
        m = jnp.max(vals, axis=1, keepdims=True)
        return m, jnp.min(jnp.where(vals == m, lane, big), axis=1, keepdims=True)

    lgt = jnp.where(lane < N_GROUPS, logits, neg)
    eg = jnp.exp(lgt - jnp.max(lgt, axis=1, keepdims=True))
    pg = eg / jnp.sum(eg, axis=1, keepdims=True)
    grp_prob, grp = first_lane_of_max(jnp.where(lane < N_GROUPS, pg, -1.0))
    lo = N_GROUPS + grp * EXPERTS_PER_GROUP
    in_grp = (lane >= lo) & (lane < lo + EXPERTS_PER_GROUP)
    let = jnp.where(in_grp, logits, neg)
    ee = jnp.exp(let - jnp.max(let, axis=1, keepdims=True))
    pe = jnp.where(in_grp, ee / jnp.sum(ee, axis=1, keepdims=True), -1.0)
    p1, l1 = first_lane_of_max(pe)
    p2, l2 = first_lane_of_max(jnp.where(lane == l1, -1.0, pe))
    den = p1 + p2
    gate1, gate2 = grp_prob * p1 / den, grp_prob * p2 / den
    e1, e2 = l1 - N_GROUPS, l2 - N_GROUPS
    tm = logits.shape[0]
    hot1 = (lane == e1).astype(F32)
    hot2 = (lane == e2).astype(F32)
    earlier = (lax.broadcasted_iota(jnp.int32, (tm, tm), 0) > lax.broadcasted_iota(jnp.int32, (tm, tm), 1)).astype(BF16)
    before1 = jnp.dot(earlier, hot1.astype(BF16), preferred_element_type=F32)
    n1 = jnp.sum(hot1, axis=0, keepdims=True)
    before2 = jnp.dot(earlier, hot2.astype(BF16), preferred_element_type=F32) + n1
    cnt = cnt_ref[...]
    slot1 = jnp.sum((before1 + cnt) * hot1, axis=1, keepdims=True)
    slot2 = jnp.sum((before2 + cnt) * hot2, axis=1, keepdims=True)
    n_blk = n1 + jnp.sum(hot2, axis=0, keepdims=True)
    lower = (lax.broadcasted_iota(jnp.int32, (ROUTER_LANES, ROUTER_LANES), 0)
             < lax.broadcasted_iota(jnp.int32, (ROUTER_LANES, ROUTER_LANES), 1)).astype(BF16)
    start = jnp.dot(jnp.broadcast_to(n_blk, (8, ROUTER_LANES)).astype(BF16), lower, preferred_element_type=F32)[0:1]
    local1 = jnp.sum((before1 + start) * hot1, axis=1, keepdims=True)
    local2 = jnp.sum((before2 + start) * hot2, axis=1, keepdims=True)
    cnt_ref[...] = cnt + n_blk
    vals = (e1.astype(F32), e2.astype(F32), gate1, gate2, slot1, slot2, local1, local2)
    out = jnp.zeros_like(logits)
    for k, val in enumerate(vals):
        out = jnp.where(lane == k, val, out)
    rt_ref[...] = out
    sub = lax.broadcasted_iota(jnp.int32, (8, ROUTER_LANES), 0)
    tab_ref[...] = jnp.where(sub == 0, n_blk, jnp.where(sub == 1, start, jnp.where(sub == 2, cnt, 0.0)))


def _post_mix(v, ln_g, ln_b, sc, sh, w_router, b_router, tp, dec_seq):
    t, d = v.shape
    tm = TM_ROW
    npb = tp // tm
    cidx = lambda i: jnp.where(i < npb, 0, 1 + (jnp.maximum(i - npb, 0) * tm) // dec_seq)
    row = pl.BlockSpec((tm, d), lambda i: (i, 0))
    vec = pl.BlockSpec((1, d), lambda i: (0, 0))
    cvec = pl.BlockSpec((None, 1, d), lambda i: (cidx(i), 0, 0))
    return pl.pallas_call(
        _post_mix_kernel,
        grid=(t // tm,),
        in_specs=[row, vec, vec, cvec, cvec,
                  pl.BlockSpec((d, ROUTER_LANES), lambda i: (0, 0)),
                  pl.BlockSpec((1, ROUTER_LANES), lambda i: (0, 0))],
        out_specs=[row, row,
                   pl.BlockSpec((tm, ROUTER_LANES), lambda i: (i, 0)),
                   pl.BlockSpec((1, ROUTER_LANES), lambda i: (0, 0)),
                   pl.BlockSpec((None, 8, ROUTER_LANES), lambda i: (i, 0, 0))],
        out_shape=[jax.ShapeDtypeStruct((t, d), F32), jax.ShapeDtypeStruct((t, d), BF16),
                   jax.ShapeDtypeStruct((t, ROUTER_LANES), F32), jax.ShapeDtypeStruct((1, ROUTER_LANES), F32),
                   jax.ShapeDtypeStruct((t // tm, 8, ROUTER_LANES), F32)],
        compiler_params=_params("arbitrary"),
        name="ln_ln_router",
    )(v, ln_g.reshape(1, d), ln_b.reshape(1, d), sc, sh, w_router, b_router)


def _on_parity(blk, fn):
    for par in range(2):
        pl.when(lax.rem(blk, 2) == par)(functools.partial(fn, blk, par))


def _dispatch_kernel(run_n_ref, run_src_ref, run_dst_ref, pad_row_ref, pad_n_ref, nact_ref,
                     h_ref, rt_ref, o_hbm, stage_ref, zero_ref, sems):
    tb = pl.program_id(0)
    n_tb = pl.num_programs(0)
    tm, d = h_ref.shape
    slab = d // 2 // 128
    n_rows_out = o_hbm.shape[0] // slab
    fill_sem = sems.at[2]

    def pieces(wait, src_ref, src_row, dst_row, n, max_size, sem):
        done = jnp.int32(0)
        size = max_size
        while size >= 1:
            take = (n & size) != 0
            src0 = 0 if src_row is None else pl.multiple_of((src_row + done) * slab, slab)
            cp = pltpu.make_async_copy(
                src_ref.at[pl.ds(src0, size * slab), :],
                o_hbm.at[pl.ds(pl.multiple_of((dst_row + done) * slab, slab), size * slab), :], sem)
            pl.when(take)(cp.wait if wait else cp.start)
            done = done + jnp.where(take, size, 0)
            size //= 2

    def runs(wait, blk, par):
        def body(e, carry):
            k = blk * N_EXPERTS + e
            pieces(wait, stage_ref.at[par], run_src_ref[k], run_dst_ref[k], run_n_ref[k], tm, sems.at[par])
            return carry
        lax.fori_loop(0, N_EXPERTS, body, 0)

    def fill(wait):
        def tail(e, carry):
            pieces(wait, zero_ref, None, pad_row_ref[e], pad_n_ref[e], tm // 2, fill_sem)
            return carry

        def spare(b, carry):
            pieces(wait, zero_ref, None, b * tm, jnp.int32(tm), tm, fill_sem)
            return carry

        lax.fori_loop(0, N_EXPERTS, tail, 0)
        lax.fori_loop(nact_ref[0], n_rows_out // tm, spare, 0)

    @pl.when(tb == 0)
    def _():
        zero_ref[...] = jnp.zeros_like(zero_ref)
        fill(False)

    @pl.when(tb >= 2)
    def _():
        _on_parity(tb, lambda blk, par: runs(True, blk - 2, par))

    rt = rt_ref[...]
    local1 = rt[:, 6:7].astype(jnp.int32)
    local2 = rt[:, 7:8].astype(jnp.int32)
    place = lax.broadcasted_iota(jnp.int32, (tm, TOP_K * tm), 1)
    onehot = ((place == local1) | (place == local2)).astype(F32)
    xp = jnp.dot(onehot.T.astype(BF16), h_ref[...], preferred_element_type=F32)
    packed = (lax.bitcast_convert_type(xp[:, d // 2:], jnp.uint32)
              | (lax.bitcast_convert_type(xp[:, :d // 2], jnp.uint32) >> 16))
    _store_as_slabs(stage_ref.at[lax.rem(tb, 2)], packed)
    _on_parity(tb, functools.partial(runs, False))

    @pl.when(tb == n_tb - 1)
    def _():
        _on_parity(tb - 1, functools.partial(runs, True))
        _on_parity(tb, functools.partial(runs, True))
        fill(True)


def _dispatch_rows(h2, route, tables, n_blocks):
    t, d = h2.shape
    tm = TM_ROW
    slab = d // 2 // 128
    assert tm == TM_MOE and t // tm >= 2
    return pl.pallas_call(
        _dispatch_kernel,
        grid_spec=pltpu.PrefetchScalarGridSpec(
            num_scalar_prefetch=len(tables),
            grid=(t // tm,),
            in_specs=[pl.BlockSpec((tm, d), lambda i, *_: (i, 0)),
                      pl.BlockSpec((tm, ROUTER_LANES), lambda i, *_: (i, 0))],
            out_specs=pl.BlockSpec(memory_space=pl.ANY),
            scratch_shapes=[pltpu.VMEM((2, TOP_K * tm * slab, 128), jnp.uint32),
                            pltpu.VMEM((tm * slab, 128), jnp.uint32), pltpu.SemaphoreType.DMA((3,))]),
        out_shape=jax.ShapeDtypeStruct((n_blocks * TM_MOE * slab, 128), jnp.uint32),
        compiler_params=_params("arbitrary"),
        name="moe_dispatch",
    )(*tables, h2, route)


def _expert_up_kernel(se_ref, sj_ref, sb_ref, oj_ref, ob_ref, fst_ref, gp_ref, nxt_ref, ne_ref, nj_ref, ns_ref,
                      x_ref, wg_hbm, wu_hbm, o_ref, wg_st, wu_st, sems):
    live = pl.program_id(0) < ns_ref[0]
    _stream_group_weights(se_ref, sj_ref, fst_ref, gp_ref, nxt_ref, ne_ref, nj_ref, live,
                          ((wg_hbm, wg_st), (wu_hbm, wu_st)), sems)

    @pl.when(live)
    def _():
        x = _unpack_bf16_pairs(_load_from_slabs(x_ref, TM_MOE))
        slot = gp_ref[pl.program_id(0)]
        a = jnp.dot(x, wg_st[slot], preferred_element_type=F32)
        u = jnp.dot(x, wu_st[slot], preferred_element_type=F32)
        o_ref[...] = ((a * _sigmoid(a)) * u).astype(o_ref.dtype)

    @pl.when(jnp.logical_not(live))
    def _():
        o_ref[...] = jnp.zeros_like(o_ref)


def _stream_group_weights(se_ref, sj_ref, fst_ref, gp_ref, nxt_ref, ne_ref, nj_ref, live, weights, sems):
    s = pl.program_id(0)

    def copies(e, j, slot):
        out = []
        for k, (w_hbm, w_st) in enumerate(weights):
            tn = w_st.shape[2]
            src = w_hbm.at[e, :, pl.ds(pl.multiple_of(j * tn, tn), tn)]
            out.append(pltpu.make_async_copy(src, w_st.at[slot], sems.at[k, slot]))
        return out

    def first_step(slot):
        cur = copies(se_ref[s], sj_ref[s], slot)

        @pl.when(s == 0)
        def _():
            for cp in cur:
                cp.start(priority=1)

        @pl.when(nxt_ref[s] == 1)
        def _():
            for cp in copies(ne_ref[s], nj_ref[s], 1 - slot):
                cp.start(priority=1)

        for cp in cur:
            cp.wait()

    for slot in range(2):
        pl.when(live & (fst_ref[s] == 1) & (gp_ref[s] == slot))(functools.partial(first_step, slot))


_N_SCHED = 11


def _sched_map(fn):
    return lambda s, *refs: fn(s, *refs[:_N_SCHED])


def _expert_up(xs, w_gate, w_up, sched, n_steps):
    d, de = w_gate.shape[1], w_gate.shape[2]
    tm, tf = TM_MOE, TF_MOE
    slab = d // 2 // 128
    r = xs.shape[0] // slab
    return pl.pallas_call(
        _expert_up_kernel,
        grid_spec=pltpu.PrefetchScalarGridSpec(
            num_scalar_prefetch=_N_SCHED,
            grid=(n_steps,),
            in_specs=[pl.BlockSpec((tm * slab, 128), _sched_map(lambda s, se, sj, sb, *_: (sb[s], 0))),
                      pl.BlockSpec(memory_space=pl.ANY), pl.BlockSpec(memory_space=pl.ANY)],
            out_specs=pl.BlockSpec((tm, tf), _sched_map(lambda s, se, sj, sb, oj, ob, *_: (ob[s], oj[s]))),
            scratch_shapes=[pltpu.VMEM((2, d, tf), F32), pltpu.VMEM((2, d, tf), F32),
                            pltpu.SemaphoreType.DMA((2, 2))]),
        out_shape=jax.ShapeDtypeStruct((r, de), BF16),
        compiler_params=_params("arbitrary"),
        name="moe_gate_up",
    )(*sched, xs, w_gate, w_up)


def _expert_down_kernel(se_ref, sj_ref, sb_ref, oj_ref, ob_ref, fst_ref, gp_ref, nxt_ref, ne_ref, nj_ref, ns_ref,
                        h_ref, wd_hbm, o_ref, wd_st, sems):
    live = pl.program_id(0) < ns_ref[0]
    _stream_group_weights(se_ref, sj_ref, fst_ref, gp_ref, nxt_ref, ne_ref, nj_ref, live, ((wd_hbm, wd_st),), sems)

    @pl.when(live)
    def _():
        y = jnp.dot(h_ref[...].astype(F32), wd_st[gp_ref[pl.program_id(0)]], preferred_element_type=F32)
        _store_as_slabs(o_ref, _pack_bf16_pairs(y))

    @pl.when(jnp.logical_not(live))
    def _():
        o_ref[...] = jnp.zeros_like(o_ref)


def _expert_down(hid, w_down, sched, n_steps):
    r, de = hid.shape
    d = w_down.shape[2]
    tm = TM_MOE
    slab = d // 2 // 128
    return pl.pallas_call(
        _expert_down_kernel,
        grid_spec=pltpu.PrefetchScalarGridSpec(
            num_scalar_prefetch=_N_SCHED,
            grid=(n_steps,),
            in_specs=[pl.BlockSpec((tm, de), _sched_map(lambda s, se, sj, sb, *_: (sb[s], 0))),
                      pl.BlockSpec(memory_space=pl.ANY)],
            out_specs=pl.BlockSpec((tm * slab, 128), _sched_map(lambda s, se, sj, sb, oj, ob, *_: (ob[s], 0))),
            scratch_shapes=[pltpu.VMEM((2, de, d), F32), pltpu.SemaphoreType.DMA((1, 2))]),
        out_shape=jax.ShapeDtypeStruct((r * slab, 128), jnp.uint32),
        compiler_params=_params("arbitrary"),
        name="moe_down",
    )(*sched, hid, w_down)


def _combine_kernel(npb, alpha, pos_ref, y_hbm, x1_ref, rt_ref, gt_ref, g_ref, b_ref,
                    op_ref, os_ref, buf_ref, sems):
    i = pl.program_id(0)
    n = pl.num_programs(0)
    tm = x1_ref.shape[0]
    slab = buf_ref.shape[2] // tm

    def rows(wait, blk, par):
        def body(g, carry):
            for u in range(ROW_UNROLL):
                r = g * ROW_UNROLL + u
                for k in range(TOP_K):
                    cp = _slab_copy(y_hbm, buf_ref.at[par, k], pos_ref[(blk * tm + r) * TOP_K + k], r, slab,
                                    sems.at[par])
                    cp.wait() if wait else cp.start(priority=k % 2)
            return carry
        lax.fori_loop(0, tm // ROW_UNROLL, body, 0)

    @pl.when(i == 0)
    def _():
        rows(False, 0, 0)

    @pl.when(i + 1 < n)
    def _():
        _on_parity(i + 1, functools.partial(rows, False))

    _on_parity(i, functools.partial(rows, True))
    rt = rt_ref[...]
    par = lax.rem(i, 2)
    expert_rows = lambda k: _unpack_bf16_pairs(_load_from_slabs(buf_ref.at[par, k], tm))
    f = rt[:, 2:3] * expert_rows(0) + rt[:, 3:4] * expert_rows(1)
    out = _ln_rows(alpha * x1_ref[...] + gt_ref[...] * f) * g_ref[...] + b_ref[...]

    @pl.when(i < npb)
    def _():
        op_ref[...] = out

    @pl.when(i >= npb)
    def _():
        os_ref[...] = out


def _combine(yb, pos, x1, route, gt, ln_g, ln_b, alpha, tp, dec_seq):
    t, d = x1.shape
    tm = TM_ROW
    npb = tp // tm
    cidx = lambda i: jnp.where(i < npb, 0, 1 + (jnp.maximum(i - npb, 0) * tm) // dec_seq)
    vec = pl.BlockSpec((1, d), lambda i, *_: (0, 0))
    return pl.pallas_call(
        functools.partial(_combine_kernel, npb, alpha),
        grid_spec=pltpu.PrefetchScalarGridSpec(
            num_scalar_prefetch=1,
            grid=(t // tm,),
            in_specs=[pl.BlockSpec(memory_space=pl.ANY),
                      pl.BlockSpec((tm, d), lambda i, *_: (i, 0)),
                      pl.BlockSpec((tm, ROUTER_LANES), lambda i, *_: (i, 0)),
                      pl.BlockSpec((None, 1, d), lambda i, *_: (cidx(i), 0, 0)),
                      vec, vec],
            out_specs=[pl.BlockSpec((tm, d), lambda i, *_: (jnp.minimum(i, npb - 1), 0)),
                       pl.BlockSpec((tm, d), lambda i, *_: (jnp.maximum(i - npb, 0), 0))],
            scratch_shapes=[pltpu.VMEM((2, TOP_K, tm * (d // 2 // 128), 128), jnp.uint32),
                            pltpu.SemaphoreType.DMA((2,))]),
        out_shape=[jax.ShapeDtypeStruct((tp, d), F32), jax.ShapeDtypeStruct((t - tp, d), F32)],
        compiler_params=_params("arbitrary"),
        name="moe_combine_ln",
    )(pos, yb, x1, route, gt, ln_g.reshape(1, d), ln_b.reshape(1, d))


def _dispatch_plan(route, counts_f, block_tab, n_tok):
    tm = TM_MOE
    n_blocks = n_tok * TOP_K // tm + N_EXPERTS
    i32 = jnp.int32
    eid = route[:, 0:TOP_K].astype(i32).reshape(-1)
    slot = route[:, 4:4 + TOP_K].astype(i32).reshape(-1)
    counts = counts_f[0, :N_EXPERTS].astype(i32)
    nb = (counts + tm - 1) // tm

    def cumsum(x):
        idx = jnp.arange(x.shape[0], dtype=i32)
        return jnp.sum(jnp.where(idx[None, :] <= idx[:, None], x[None, :], 0), axis=1)

    nb_end = cumsum(nb)
    bs = nb_end - nb
    n_act = nb_end[-1]
    experts = jnp.arange(N_EXPERTS, dtype=i32)
    look = lambda table, e: jnp.sum(jnp.where(e[:, None] == experts[None, :], table[None, :], 0), axis=1)
    tab = block_tab[:, :3, :N_EXPERTS].astype(i32)
    run_dst = (bs * tm)[None, :] + tab[:, 2]
    dispatch_tables = (tab[:, 0].reshape(-1), tab[:, 1].reshape(-1), run_dst.reshape(-1),
                       bs * tm + counts, nb * tm - counts, n_act.reshape(1))

    def schedule(n_inner):
        n_steps = n_inner * n_blocks
        n_live = n_inner * n_act

        def decode(step):
            s = jnp.minimum(step, n_live - 1)
            e = jnp.minimum(jnp.sum((s[:, None] >= n_inner * nb_end[None, :]).astype(i32), axis=1), N_EXPERTS - 1)
            nbe = jnp.maximum(look(nb, e), 1)
            loc = s - n_inner * look(bs, e)
            return e, loc // nbe, loc % nbe, nbe, look(bs, e)

        step = jnp.arange(n_steps, dtype=i32)
        e, sj, bi, nbe, bse = decode(step)
        live = step < n_live
        nxt_step = jnp.minimum(step, n_live - 1) + nbe - bi
        ne, nj, _, _, _ = decode(nxt_step)
        fst = (live & (bi == 0)).astype(i32)
        gp = (cumsum(fst) - 1) % 2
        nxt = (live & (bi == 0) & (nxt_step < n_live)).astype(i32)
        spare = jnp.maximum(step - n_live, 0)
        oj = jnp.where(live, sj, spare % n_inner)
        ob = jnp.where(live, bse + bi, n_act + spare // n_inner)
        sched = (e, sj, bse + bi, oj, ob, fst, gp, nxt, ne, nj, n_live.reshape(1))
        return tuple(a.astype(i32) for a in sched), n_steps

    pos = look(bs * tm, eid) + slot
    return pos, dispatch_tables, n_blocks, schedule


def kernel(x_prompt, x_sample, state_ret_fwd, state_ret_bwd, c, c_ctx, w_mod, b_mod, w_in, conv_w, conv_b, conv_ln_g, conv_ln_b, ret_decay_fwd, ret_decay_bwd, ret_gn_g, w_out, ln1_g, ln1_b, w_grp, b_grp, w_exp, b_exp, w_gate, w_up, w_down, ln2_g, ln2_b):
    depth = w_mod.shape[0]
    assert depth == 1, "single-layer step"
    bp, sp, d = x_prompt.shape
    bs_, ss, _ = x_sample.shape
    tp, ts = bp * sp, bs_ * ss
    t = tp + ts
    conv_width = conv_w.shape[2]
    ret_width = ret_gn_g.shape[1]
    assert ret_width == RET_HEADS * RET_D and sp % CHUNK == 0 and ss % CHUNK == 0
    alpha = (2.0 * depth) ** 0.25

    xp = x_prompt.reshape(tp, d)
    xs = x_sample.reshape(ts, d)
    cond8 = jnp.zeros((8, d), F32).at[0].set(c_ctx).at[1:1 + bs_].set(c)
    m = _modulation(cond8, w_mod[0], b_mod[0])
    sh1, sc1, gt1, sh2, sc2, gt2 = [m[:, k * d:(k + 1) * d].reshape(8, 1, d) for k in range(6)]

    h = _ln_modulate(xp, xs, sc1, sh1, ss)
    u_glu = _glu_proj(h, w_in[0], conv_width)
    cos_t, sin_t = _rope_tables(ss, RET_D)
    qkvg = _qkvg_proj(h, w_in[0], cos_t, sin_t, 2 * conv_width, 4 * ret_width, tp)

    u = _conv_module(u_glu, conv_w[0], conv_b[0], conv_ln_g[0], conv_ln_b[0], tp, ss)

    lg = jnp.stack([jax.nn.log_sigmoid(ret_decay_fwd[0].astype(F32)),
                    jax.nn.log_sigmoid(ret_decay_bwd[0].astype(F32))])
    gn = ret_gn_g[0].reshape(1, ret_width)
    r_p, new_f, new_b = _retention(qkvg, lg, gn, sp, 0, bp)
    r_s = _retention(qkvg, lg, gn, ss, tp // ss, bs_, state_ret_fwd, state_ret_bwd)
    r = jnp.concatenate([r_p, r_s], axis=0)

    v = _out_proj(u, r, w_out[0], xp, xs, gt1, alpha, ss)

    w_router = jnp.zeros((d, ROUTER_LANES), F32).at[:, :N_GROUPS].set(w_grp[0]).at[:, N_GROUPS:N_GROUPS + N_EXPERTS].set(w_exp[0]).astype(BF16)
    b_router = jnp.zeros((1, ROUTER_LANES), F32).at[0, :N_GROUPS].set(b_grp[0]).at[0, N_GROUPS:N_GROUPS + N_EXPERTS].set(b_exp[0])
    x1, h2, route, counts, block_tab = _post_mix(v, ln1_g[0], ln1_b[0], sc2, sh2, w_router, b_router, tp, ss)

    pos, dispatch_tables, n_blocks, schedule = _dispatch_plan(route, counts, block_tab, t)
    xg = _dispatch_rows(h2, route, dispatch_tables, n_blocks)
    sched_up, n_up = schedule(w_gate.shape[3] // TF_MOE)
    hid = _expert_up(xg, w_gate[0], w_up[0], sched_up, n_up)
    sched_dn, n_dn = schedule(1)
    yb = _expert_down(hid, w_down[0], sched_dn, n_dn)

    out_p, out_s = _combine(yb, pos, x1, route, gt2, ln2_g[0], ln2_b[0], alpha, tp, ss)
    return (out_p.reshape(bp, sp, d), out_s.reshape(bs_, ss, d), new_f, new_b)
```

```python
import functools

import numpy as np
import jax
import jax.numpy as jnp
from jax import lax
from jax.experimental import pallas as pl
from jax.experimental.pallas import tpu as pltpu

F32 = jnp.float32
BF16 = jnp.bfloat16

LN_EPS = 1e-5
CONV_K = 31
CONV_HALO = 16
RET_HEADS = 8
RET_D = 256
HEADS_PER_STEP = 4
CHUNK = 128
GRID_W = 64
ROPE_BASE = 10000.0
N_GROUPS = 4
EXPERTS_PER_GROUP = 8
N_EXPERTS = N_GROUPS * EXPERTS_PER_GROUP
TOP_K = 2
ROUTER_LANES = 128
VMEM_LIMIT = 56 * 1024 * 1024
TM_PROJ = 1024
TM_ROW = 256
TM_MOE = 256
TF_MOE = 512
ROW_UNROLL = 8
RARE_RUN = 64

def _params(*sem):
    return pltpu.CompilerParams(dimension_semantics=tuple(sem), vmem_limit_bytes=VMEM_LIMIT)


def _sigmoid(x):
    return 1.0 / (1.0 + jnp.exp(-x))


def _ln_rows(x):
    mu = jnp.mean(x, axis=-1, keepdims=True)
    xc = x - mu
    var = jnp.mean(xc * xc, axis=-1, keepdims=True)
    return xc * lax.rsqrt(var + LN_EPS)


def _mod_kernel(c_ref, w_ref, b_ref, o_ref):
    c = c_ref[...]
    s = c * _sigmoid(c)
    o_ref[...] = jnp.dot(s.astype(BF16), w_ref[...].astype(BF16), preferred_element_type=F32) + b_ref[...]


def _modulation(cond8, w_mod, b_mod):
    d, n = w_mod.shape
    tn = 512
    return pl.pallas_call(
        _mod_kernel,
        grid=(n // tn,),
        in_specs=[pl.BlockSpec((8, d), lambda j: (0, 0)),
                  pl.BlockSpec((d, tn), lambda j: (0, j)),
                  pl.BlockSpec((1, tn), lambda j: (0, j))],
        out_specs=pl.BlockSpec((8, tn), lambda j: (0, j)),
        out_shape=jax.ShapeDtypeStruct((8, n), F32),
        compiler_params=_params("arbitrary"),
        name="modulation",
    )(cond8, w_mod, b_mod.reshape(1, n))


def _ln_mod_kernel(npb, xp_ref, xs_ref, sc_ref, sh_ref, o_ref):
    def body(x_ref):
        y = _ln_rows(x_ref[...])
        o_ref[...] = (y * (1.0 + sc_ref[...]) + sh_ref[...]).astype(o_ref.dtype)

    i = pl.program_id(0)
    pl.when(i < npb)(lambda: body(xp_ref))
    pl.when(i >= npb)(lambda: body(xs_ref))


def _ln_modulate(xp, xs, sc, sh, dec_seq):
    (tp, d), ts = xp.shape, xs.shape[0]
    tm = 512
    npb, nsb = tp // tm, ts // tm
    cidx = lambda i: jnp.where(i < npb, 0, 1 + (jnp.maximum(i - npb, 0) * tm) // dec_seq)
    return pl.pallas_call(
        functools.partial(_ln_mod_kernel, npb),
        grid=(npb + nsb,),
        in_specs=[pl.BlockSpec((tm, d), lambda i: (jnp.minimum(i, npb - 1), 0)),
                  pl.BlockSpec((tm, d), lambda i: (jnp.maximum(i - npb, 0), 0)),
                  pl.BlockSpec((None, 1, d), lambda i: (cidx(i), 0, 0)),
                  pl.BlockSpec((None, 1, d), lambda i: (cidx(i), 0, 0))],
        out_specs=pl.BlockSpec((tm, d), lambda i: (i, 0)),
        out_shape=jax.ShapeDtypeStruct((tp + ts, d), BF16),
        compiler_params=_params("arbitrary"),
        name="ln_modulate",
    )(xp, xs, sc, sh)


def _glu_proj_kernel(h_ref, wv_ref, wg_ref, o_ref):
    h = h_ref[...]
    a = jnp.dot(h, wv_ref[...].astype(BF16), preferred_element_type=F32)
    g = jnp.dot(h, wg_ref[...].astype(BF16), preferred_element_type=F32)
    o_ref[...] = a * _sigmoid(g)


def _glu_proj(h, w_in, conv_width):
    t, d = h.shape
    tm, tn = TM_PROJ, 256
    goff = conv_width // tn
    return pl.pallas_call(
        _glu_proj_kernel,
        grid=(t // tm, conv_width // tn),
        in_specs=[pl.BlockSpec((tm, d), lambda i, j: (i, 0)),
                  pl.BlockSpec((d, tn), lambda i, j: (0, j)),
                  pl.BlockSpec((d, tn), lambda i, j: (0, j + goff))],
        out_specs=pl.BlockSpec((tm, tn), lambda i, j: (i, j)),
        out_shape=jax.ShapeDtypeStruct((t, conv_width), F32),
        compiler_params=_params("arbitrary", "arbitrary"),
        name="glu_proj",
    )(h, w_in, w_in)


def _qkvg_proj_kernel(npb, tiles_per_part, k_scale, h_ref, w_ref, cos_ref, sin_ref, o_ref):
    i, j = pl.program_id(0), pl.program_id(1)
    z = jnp.dot(h_ref[...], w_ref[...].astype(BF16), preferred_element_type=F32)
    tpp = tiles_per_part

    @pl.when(j >= 3 * tpp)
    def _():
        o_ref[...] = (z * _sigmoid(z)).astype(o_ref.dtype)

    @pl.when((j >= 2 * tpp) & (j < 3 * tpp))
    def _():
        o_ref[...] = z.astype(o_ref.dtype)

    @pl.when(j < 2 * tpp)
    def _():
        zz = z * jnp.where(j >= tpp, k_scale, 1.0).astype(F32)

        @pl.when(i < npb)
        def _():
            o_ref[...] = zz.astype(o_ref.dtype)

        @pl.when(i >= npb)
        def _():
            for s in range(zz.shape[1] // 128):
                cs = slice(s * 128, (s + 1) * 128)
                ts_ = slice((s * 128) % RET_D, (s * 128) % RET_D + 128)
                zs = zz[:, cs]
                o_ref[:, cs] = (zs * cos_ref[:, ts_] + pltpu.roll(zs, 64, axis=1) * sin_ref[:, ts_]).astype(o_ref.dtype)


def _qkvg_proj(h, w_in, cos_t, sin_t, col0, n_cols, tp):
    t, d = h.shape
    tm, tn = TM_PROJ, 512
    assert cos_t.shape == (tm, RET_D), "a projection row tile is one latent sequence"
    npb = tp // tm
    tpp = (n_cols // 4) // tn
    return pl.pallas_call(
        functools.partial(_qkvg_proj_kernel, npb, tpp, RET_D ** -0.5),
        grid=(t // tm, n_cols // tn),
        in_specs=[pl.BlockSpec((tm, d), lambda i, j: (i, 0)),
                  pl.BlockSpec((d, tn), lambda i, j: (0, j + col0 // tn)),
                  pl.BlockSpec((tm, RET_D), lambda i, j: (0, 0)),
                  pl.BlockSpec((tm, RET_D), lambda i, j: (0, 0))],
        out_specs=pl.BlockSpec((tm, tn), lambda i, j: (i, j)),
        out_shape=jax.ShapeDtypeStruct((t, n_cols), BF16),
        compiler_params=_params("arbitrary", "arbitrary"),
        name="qkvg_proj",
    )(h, w_in, cos_t, sin_t)


def _rope_tables(n_tok, width):
    rows = n_tok // GRID_W
    r_idx = np.repeat(np.arange(rows), GRID_W).astype(np.float64)
    c_idx = np.tile(np.arange(GRID_W), rows).astype(np.float64)
    dq = RET_D // 2
    inv = ROPE_BASE ** (-np.arange(dq // 2, dtype=np.float64) / (dq // 2))
    sign = np.concatenate([-np.ones(dq // 2), np.ones(dq // 2)])
    cos_h, sin_h = [], []
    for idx in (r_idx, c_idx):
        ang = idx[:, None] * inv
        cos_h.append(np.concatenate([np.cos(ang), np.cos(ang)], axis=1))
        sin_h.append(np.concatenate([np.sin(ang), np.sin(ang)], axis=1) * sign)
    cos_h, sin_h = np.concatenate(cos_h, axis=1), np.concatenate(sin_h, axis=1)
    reps = width // RET_D
    return (jnp.asarray(np.tile(cos_h, (1, reps)), F32), jnp.asarray(np.tile(sin_h, (1, reps)), F32))


def _conv_kernel(npb, tiles_per_seq, uc_ref, up_ref, un_ref, w_ref, b_ref, g_ref, bt_ref, o_ref, buf_ref, y_ref):
    r = pl.program_id(0)
    tm = uc_ref.shape[0]
    ncg = buf_ref.shape[0]
    t = lax.rem(jnp.maximum(r - npb, 0), tiles_per_seq)
    is_s = r >= npb
    has_prev = is_s & (t != 0)
    has_next = is_s & (t != tiles_per_seq - 1)
    for cg in range(ncg):
        cs = slice(cg * 128, (cg + 1) * 128)
        buf_ref[cg, 0:CONV_HALO, :] = jnp.where(has_prev, up_ref[:, cs], 0.0)
        buf_ref[cg, CONV_HALO:CONV_HALO + tm, :] = uc_ref[:, cs]
        buf_ref[cg, CONV_HALO + tm:, :] = jnp.where(has_next, un_ref[:, cs], 0.0)

    off = CONV_HALO - CONV_K // 2

    def body(cg, carry):
        acc = jnp.zeros((tm, 128), F32)
        for tap in range(CONV_K):
            acc = acc + buf_ref[cg, off + tap:off + tap + tm, :] * w_ref[cg, tap:tap + 1, :]
        y_ref[cg] = acc + b_ref[cg]
        return carry

    lax.fori_loop(0, ncg, body, 0)

    n_ch = ncg * 128
    tot = y_ref[0]
    for cg in range(1, ncg):
        tot = tot + y_ref[cg]
    mu = jnp.sum(tot, axis=1, keepdims=True) * (1.0 / n_ch)
    sq = jnp.zeros((tm, 128), F32)
    for cg in range(ncg):
        dv = y_ref[cg] - mu
        sq = sq + dv * dv
    var = jnp.sum(sq, axis=1, keepdims=True) * (1.0 / n_ch)
    rstd = lax.rsqrt(var + LN_EPS)
    for cg in range(ncg):
        cs = slice(cg * 128, (cg + 1) * 128)
        v = (y_ref[cg] - mu) * rstd * g_ref[cg] + bt_ref[cg]
        o_ref[:, cs] = (v * _sigmoid(v)).astype(o_ref.dtype)


def _conv_module(u, conv_w, conv_b, ln_g, ln_b, tp, dec_seq):
    t, c = u.shape
    tm = TM_ROW
    ncg = c // 128
    npb = tp // tm
    hb = tm // CONV_HALO
    n_halo_blocks = t // CONV_HALO
    w3 = jnp.zeros((32, c), F32).at[:CONV_K].set(conv_w).reshape(32, ncg, 128).transpose(1, 0, 2)
    vec = lambda a: a.reshape(ncg, 1, 128)
    return pl.pallas_call(
        functools.partial(_conv_kernel, npb, dec_seq // tm),
        grid=(t // tm,),
        in_specs=[pl.BlockSpec((tm, c), lambda r: (r, 0)),
                  pl.BlockSpec((CONV_HALO, c), lambda r: (jnp.maximum(r * hb - 1, 0), 0)),
                  pl.BlockSpec((CONV_HALO, c), lambda r: (jnp.minimum((r + 1) * hb, n_halo_blocks - 1), 0)),
                  pl.BlockSpec((ncg, 32, 128), lambda r: (0, 0, 0)),
                  pl.BlockSpec((ncg, 1, 128), lambda r: (0, 0, 0)),
                  pl.BlockSpec((ncg, 1, 128), lambda r: (0, 0, 0)),
                  pl.BlockSpec((ncg, 1, 128), lambda r: (0, 0, 0))],
        out_specs=pl.BlockSpec((tm, c), lambda r: (r, 0)),
        out_shape=jax.ShapeDtypeStruct((t, c), BF16),
        scratch_shapes=[pltpu.VMEM((ncg, tm + 2 * CONV_HALO, 128), F32),
                        pltpu.VMEM((ncg, tm, 128), F32)],
        compiler_params=_params("arbitrary"),
        name="conv_ln_swish",
    )(u, u, u, w3, vec(conv_b), vec(ln_g), vec(ln_b))


def _retention_kernel(nc, has_init, lg_ref, q_ref, k_ref, v_ref, g_ref, gn_ref, *rest):
    if has_init:
        s0f_ref, s0b_ref, r_ref, o_ref, sf_ref, sb_ref = rest
    else:
        s0f_ref = s0b_ref = None
        r_ref, sf_ref, sb_ref, o_ref = rest
    for hh in range(HEADS_PER_STEP):
        cols = slice(hh * RET_D, (hh + 1) * RET_D)
        _retention_head(nc, lg_ref, pl.program_id(1) * HEADS_PER_STEP + hh,
                        q_ref.at[:, cols], k_ref.at[:, cols], v_ref.at[:, cols], g_ref.at[:, cols],
                        gn_ref.at[:, cols], None if s0f_ref is None else s0f_ref.at[hh],
                        None if s0b_ref is None else s0b_ref.at[hh],
                        r_ref.at[:, cols], sf_ref.at[hh], sb_ref.at[hh], o_ref.at[hh])


def _retention_head(nc, lg_ref, hd, q_ref, k_ref, v_ref, g_ref, gn_ref, s0f_ref, s0b_ref, r_ref, sf_ref, sb_ref, o_ref):
    has_init = s0f_ref is not None
    lgf, lgb = lg_ref[0, hd], lg_ref[1, hd]
    c = CHUNK
    row = lax.broadcasted_iota(jnp.int32, (c, c), 0)
    col = lax.broadcasted_iota(jnp.int32, (c, c), 1)
    diff = (row - col).astype(F32)
    dec = (jnp.where(diff >= 0, jnp.exp(jnp.maximum(diff, 0.0) * lgf), 0.0)
           + jnp.where(diff <= 0, jnp.exp(jnp.maximum(-diff, 0.0) * lgb), 0.0))
    pos = lax.broadcasted_iota(jnp.int32, (c, 1), 0).astype(F32)
    xi_f = jnp.exp((pos + 1.0) * lgf)
    zeta_f = jnp.exp((c - 1.0 - pos) * lgf)
    xi_b = jnp.exp((c - pos) * lgb)
    zeta_b = jnp.exp(pos * lgb)
    gch_f = jnp.exp(jnp.full((1, RET_D), c, F32) * lgf)
    gch_b = jnp.exp(jnp.full((1, RET_D), c, F32) * lgb)

    if has_init:
        sf_ref[...] = s0f_ref[...]
        sb_ref[...] = s0b_ref[...]
    else:
        sf_ref[...] = jnp.zeros_like(sf_ref)
        sb_ref[...] = jnp.zeros_like(sb_ref)

    def chunk(ci):
        sl = slice(ci * c, (ci + 1) * c)
        return q_ref[sl, :], k_ref[sl, :], v_ref[sl, :], sl

    def state_update(s_ref, kc, vc, zeta, gch):
        kz = (kc.astype(F32) * zeta).T.astype(BF16)
        s_ref[...] = gch * s_ref[...] + jnp.dot(kz, vc, preferred_element_type=F32)

    for ci in range(nc):
        qc, kc, vc, sl = chunk(ci)
        s = lax.dot_general(qc, kc, (((1,), (1,)), ((), ())), preferred_element_type=F32)
        p = (s * dec).astype(BF16)
        o = jnp.dot(p, vc, preferred_element_type=F32)
        o = o + jnp.dot(qc, sf_ref[...].astype(BF16), preferred_element_type=F32) * xi_f
        o_ref[sl, :] = o
        state_update(sf_ref, kc, vc, zeta_f, gch_f)

    for ci in reversed(range(nc)):
        qc, kc, vc, sl = chunk(ci)
        o_ref[sl, :] = o_ref[sl, :] + jnp.dot(qc, sb_ref[...].astype(BF16), preferred_element_type=F32) * xi_b
        state_update(sb_ref, kc, vc, zeta_b, gch_b)

    y = _ln_rows(o_ref[...])
    r_ref[...] = (g_ref[...].astype(F32) * (y * gn_ref[...])).astype(r_ref.dtype)


def _retention(qkvg, lg, gn, n, row_blk0, n_seq, s0f=None, s0b=None):
    has_init = s0f is not None
    nh, dd, hps = RET_HEADS, RET_D, HEADS_PER_STEP
    wd = hps * dd
    col = lambda part: (lambda b, h: (b + row_blk0, part * (nh // hps) + h))
    st_spec = pl.BlockSpec((None, None, hps, dd, dd), lambda b, h: (b, 0, h, 0, 0))
    in_specs = [pl.BlockSpec(memory_space=pltpu.SMEM),
                pl.BlockSpec((n, wd), col(0)), pl.BlockSpec((n, wd), col(1)),
                pl.BlockSpec((n, wd), col(2)), pl.BlockSpec((n, wd), col(3)),
                pl.BlockSpec((1, wd), lambda b, h: (0, h))]
    args = [lg, qkvg, qkvg, qkvg, qkvg, gn]
    r_shape = jax.ShapeDtypeStruct((n_seq * n, nh * dd), BF16)
    r_spec = pl.BlockSpec((n, wd), lambda b, h: (b, h))
    st_scratch = pltpu.VMEM((hps, dd, dd), F32)
    if has_init:
        in_specs += [st_spec, st_spec]
        args += [s0f, s0b]
        out_specs, out_shape = r_spec, r_shape
        scratch = [pltpu.VMEM((hps, n, dd), F32), st_scratch, st_scratch]
    else:
        st_shape = jax.ShapeDtypeStruct((n_seq, 1, nh, dd, dd), F32)
        out_specs, out_shape = [r_spec, st_spec, st_spec], [r_shape, st_shape, st_shape]
        scratch = [pltpu.VMEM((hps, n, dd), F32)]
    return pl.pallas_call(
        functools.partial(_retention_kernel, n // CHUNK, has_init),
        grid=(n_seq, nh // hps),
        in_specs=in_specs, out_specs=out_specs, out_shape=out_shape,
        scratch_shapes=scratch,
        compiler_params=_params("arbitrary", "arbitrary"),
        name="retention_latent" if has_init else "retention_context",
    )(*args)


def _out_proj_kernel(npb, alpha, u_ref, r_ref, w1_ref, w2_ref, xp_ref, xs_ref, gt_ref, o_ref):
    i = pl.program_id(0)
    y = (jnp.dot(u_ref[...], w1_ref[...].astype(BF16), preferred_element_type=F32)
         + jnp.dot(r_ref[...], w2_ref[...].astype(BF16), preferred_element_type=F32))

    @pl.when(i < npb)
    def _():
        o_ref[...] = alpha * xp_ref[...] + gt_ref[...] * y

    @pl.when(i >= npb)
    def _():
        o_ref[...] = alpha * xs_ref[...] + gt_ref[...] * y


def _out_proj(u, r, w_out, xp, xs, gt, alpha, dec_seq):
    t, kh = u.shape
    d = w_out.shape[1]
    tm, tn = TM_PROJ, 512
    npb = xp.shape[0] // tm
    cidx = lambda i: jnp.where(i < npb, 0, 1 + (jnp.maximum(i - npb, 0) * tm) // dec_seq)
    return pl.pallas_call(
        functools.partial(_out_proj_kernel, npb, alpha),
        grid=(t // tm, d // tn),
        in_specs=[pl.BlockSpec((tm, kh), lambda i, j: (i, 0)),
                  pl.BlockSpec((tm, kh), lambda i, j: (i, 0)),
                  pl.BlockSpec((kh, tn), lambda i, j: (0, j)),
                  pl.BlockSpec((kh, tn), lambda i, j: (1, j)),
                  pl.BlockSpec((tm, tn), lambda i, j: (jnp.minimum(i, npb - 1), j)),
                  pl.BlockSpec((tm, tn), lambda i, j: (jnp.maximum(i - npb, 0), j)),
                  pl.BlockSpec((None, 1, tn), lambda i, j: (cidx(i), 0, j))],
        out_specs=pl.BlockSpec((tm, tn), lambda i, j: (i, j)),
        out_shape=jax.ShapeDtypeStruct((t, d), F32),
        compiler_params=_params("arbitrary", "arbitrary"),
        name="out_proj_residual",
    )(u, r, w_out, w_out, xp, xs, gt)


def _pack_bf16_pairs(x):
    n = x.shape[1] // 2
    lo = lax.bitcast_convert_type(x[:, :n].astype(BF16).astype(F32), jnp.uint32)
    hi = lax.bitcast_convert_type(x[:, n:].astype(BF16).astype(F32), jnp.uint32)
    return hi | (lo >> 16)


def _unpack_bf16_pairs(p):
    lo = lax.bitcast_convert_type(p << 16, F32)
    hi = lax.bitcast_convert_type(p & jnp.uint32(0xFFFF0000), F32)
    return jnp.concatenate([lo, hi], axis=1)


def _store_as_slabs(ref, rows):
    m, n = rows.shape
    slab = n // 128
    for s in range(slab):
        ref[pl.ds(s, m, stride=slab), :] = rows[:, s * 128:(s + 1) * 128]


def _load_from_slabs(ref, m):
    slab = ref.shape[0] // m
    return jnp.concatenate([ref[pl.ds(s, m, stride=slab), :] for s in range(slab)], axis=1)


def _slab_copy(src, dst, src_row, dst_row, slab, sem):
    return pltpu.make_async_copy(src.at[pl.ds(pl.multiple_of(src_row * slab, slab), slab), :],
                                 dst.at[pl.ds(pl.multiple_of(dst_row * slab, slab), slab), :], sem)


def _post_mix_kernel(v_ref, g1_ref, b1_ref, sc_ref, sh_ref, wr_ref, br_ref, x1_ref, h2_ref, rt_ref, cnt_ref, tab_ref):
    @pl.when(pl.program_id(0) == 0)
    def _():
        cnt_ref[...] = jnp.zeros_like(cnt_ref)

    x1 = _ln_rows(v_ref[...]) * g1_ref[...] + b1_ref[...]
    x1_ref[...] = x1
    h2 = (_ln_rows(x1) * (1.0 + sc_ref[...]) + sh_ref[...]).astype(BF16)
    h2_ref[...] = h2
    logits = jnp.dot(h2, wr_ref[...], preferred_element_type=F32) + br_ref[...]
    lane = lax.broadcasted_iota(jnp.int32, logits.shape, 1)
    big = jnp.int32(ROUTER_LANES)
    neg = jnp.float32(-jnp.inf)

    def first_lane_of_max(vals):
        m = jnp.max(vals, axis=1, keepdims=True)
        return m, jnp.min(jnp.where(vals == m, lane, big), axis=1, keepdims=True)

    lgt = jnp.where(lane < N_GROUPS, logits, neg)
    eg = jnp.exp(lgt - jnp.max(lgt, axis=1, keepdims=True))
    pg = eg / jnp.sum(eg, axis=1, keepdims=True)
    grp_prob, grp = first_lane_of_max(jnp.where(lane < N_GROUPS, pg, -1.0))
    lo = N_GROUPS + grp * EXPERTS_PER_GROUP
    in_grp = (lane >= lo) & (lane < lo + EXPERTS_PER_GROUP)
    let = jnp.where(in_grp, logits, neg)
    ee = jnp.exp(let - jnp.max(let, axis=1, keepdims=True))
    pe = jnp.where(in_grp, ee / jnp.sum(ee, axis=1, keepdims=True), -1.0)
    p1, l1 = first_lane_of_max(pe)
    p2, l2 = first_lane_of_max(jnp.where(lane == l1, -1.0, pe))
    den = p1 + p2
    gate1, gate2 = grp_prob * p1 / den, grp_prob * p2 / den
    e1, e2 = l1 - N_GROUPS, l2 - N_GROUPS
    tm = logits.shape[0]
    hot1 = (lane == e1).astype(F32)
    hot2 = (lane == e2).astype(F32)
    earlier = (lax.broadcasted_iota(jnp.int32, (tm, tm), 0) > lax.broadcasted_iota(jnp.int32, (tm, tm), 1)).astype(BF16)
    before1 = jnp.dot(earlier, hot1.astype(BF16), preferred_element_type=F32)
    n1 = jnp.sum(hot1, axis=0, keepdims=True)
    before2 = jnp.dot(earlier, hot2.astype(BF16), preferred_element_type=F32) + n1
    cnt = cnt_ref[...]
    slot1 = jnp.sum((before1 + cnt) * hot1, axis=1, keepdims=True)
    slot2 = jnp.sum((before2 + cnt) * hot2, axis=1, keepdims=True)
    n_blk = n1 + jnp.sum(hot2, axis=0, keepdims=True)
    lower = (lax.broadcasted_iota(jnp.int32, (ROUTER_LANES, ROUTER_LANES), 0)
             < lax.broadcasted_iota(jnp.int32, (ROUTER_LANES, ROUTER_LANES), 1)).astype(BF16)
    start = jnp.dot(jnp.broadcast_to(n_blk, (8, ROUTER_LANES)).astype(BF16), lower, preferred_element_type=F32)[0:1]
    local1 = jnp.sum((before1 + start) * hot1, axis=1, keepdims=True)
    local2 = jnp.sum((before2 + start) * hot2, axis=1, keepdims=True)
    cnt_ref[...] = cnt + n_blk
    vals = (e1.astype(F32), e2.astype(F32), gate1, gate2, slot1, slot2, local1, local2)
    out = jnp.zeros_like(logits)
    for k, val in enumerate(vals):
        out = jnp.where(lane == k, val, out)
    rt_ref[...] = out
    sub = lax.broadcasted_iota(jnp.int32, (8, ROUTER_LANES), 0)
    tab_ref[...] = jnp.where(sub == 0, n_blk, jnp.where(sub == 1, start, jnp.where(sub == 2, cnt, 0.0)))


def _post_mix(v, ln_g, ln_b, sc, sh, w_router, b_router, tp, dec_seq):
    t, d = v.shape
    tm = TM_ROW
    npb = tp // tm
    cidx = lambda i: jnp.where(i < npb, 0, 1 + (jnp.maximum(i - npb, 0) * tm) // dec_seq)
    row = pl.BlockSpec((tm, d), lambda i: (i, 0))
    vec = pl.BlockSpec((1, d), lambda i: (0, 0))
    cvec = pl.BlockSpec((None, 1, d), lambda i: (cidx(i), 0, 0))
    return pl.pallas_call(
        _post_mix_kernel,
        grid=(t // tm,),
        in_specs=[row, vec, vec, cvec, cvec,
                  pl.BlockSpec((d, ROUTER_LANES), lambda i: (0, 0)),
                  pl.BlockSpec((1, ROUTER_LANES), lambda i: (0, 0))],
        out_specs=[row, row,
                   pl.BlockSpec((tm, ROUTER_LANES), lambda i: (i, 0)),
                   pl.BlockSpec((1, ROUTER_LANES), lambda i: (0, 0)),
                   pl.BlockSpec((None, 8, ROUTER_LANES), lambda i: (i, 0, 0))],
        out_shape=[jax.ShapeDtypeStruct((t, d), F32), jax.ShapeDtypeStruct((t, d), BF16),
                   jax.ShapeDtypeStruct((t, ROUTER_LANES), F32), jax.ShapeDtypeStruct((1, ROUTER_LANES), F32),
                   jax.ShapeDtypeStruct((t // tm, 8, ROUTER_LANES), F32)],
        compiler_params=_params("arbitrary"),
        name="ln_ln_router",
    )(v, ln_g.reshape(1, d), ln_b.reshape(1, d), sc, sh, w_router, b_router)


def _on_parity(blk, fn):
    for par in range(2):
        pl.when(lax.rem(blk, 2) == par)(functools.partial(fn, blk, par))


def _copy_run(wait, src_ref, dst_ref, src_row, dst_row, n, max_size, slab, sem):
    def piece(size):
        done = n & jnp.int32(-2 * size)
        src0 = 0 if src_row is None else pl.multiple_of((src_row + done) * slab, slab)
        cp = pltpu.make_async_copy(
            src_ref.at[pl.ds(src0, size * slab), :],
            dst_ref.at[pl.ds(pl.multiple_of((dst_row + done) * slab, slab), size * slab), :], sem)
        pl.when((n & size) != 0)(cp.wait if wait else cp.start)

    sizes = [max_size >> k for k in range(max_size.bit_length())]
    rare = [s for s in sizes if s >= RARE_RUN]
    if rare:
        @pl.when(n >= RARE_RUN)
        def _():
            for s in rare:
                piece(s)
    for s in sizes:
        if s < RARE_RUN:
            piece(s)


def _dispatch_kernel(run_n_ref, run_src_ref, run_dst_ref, pad_row_ref, pad_n_ref, nact_ref,
                     h_ref, rt_ref, o_hbm, stage_ref, zero_ref, sems):
    tb = pl.program_id(0)
    n_tb = pl.num_programs(0)
    tm, d = h_ref.shape
    slab = d // 2 // 128
    n_rows_out = o_hbm.shape[0] // slab
    fill_sem = sems.at[2]

    def pieces(wait, src_ref, src_row, dst_row, n, max_size, sem):
        _copy_run(wait, src_ref, o_hbm, src_row, dst_row, n, max_size, slab, sem)

    def runs(wait, blk, par):
        def body(e, carry):
            k = blk * N_EXPERTS + e
            pieces(wait, stage_ref.at[par], run_src_ref[k], run_dst_ref[k], run_n_ref[k], tm, sems.at[par])
            return carry
        lax.fori_loop(0, N_EXPERTS, body, 0)

    def fill(wait):
        def tail(e, carry):
            pieces(wait, zero_ref, None, pad_row_ref[e], pad_n_ref[e], tm // 2, fill_sem)
            return carry

        def spare(b, carry):
            pieces(wait, zero_ref, None, b * tm, jnp.int32(tm), tm, fill_sem)
            return carry

        lax.fori_loop(0, N_EXPERTS, tail, 0)
        lax.fori_loop(nact_ref[0], n_rows_out // tm, spare, 0)

    @pl.when(tb == 0)
    def _():
        zero_ref[...] = jnp.zeros_like(zero_ref)
        fill(False)

    @pl.when(tb >= 2)
    def _():
        _on_parity(tb, lambda blk, par: runs(True, blk - 2, par))

    rt = rt_ref[...]
    local1 = rt[:, 6:7].astype(jnp.int32)
    local2 = rt[:, 7:8].astype(jnp.int32)
    place = lax.broadcasted_iota(jnp.int32, (tm, TOP_K * tm), 1)
    onehot = ((place == local1) | (place == local2)).astype(F32)
    xp = jnp.dot(onehot.T.astype(BF16), h_ref[...], preferred_element_type=F32)
    packed = (lax.bitcast_convert_type(xp[:, d // 2:], jnp.uint32)
              | (lax.bitcast_convert_type(xp[:, :d // 2], jnp.uint32) >> 16))
    _store_as_slabs(stage_ref.at[lax.rem(tb, 2)], packed)
    _on_parity(tb, functools.partial(runs, False))

    @pl.when(tb == n_tb - 1)
    def _():
        _on_parity(tb - 1, functools.partial(runs, True))
        _on_parity(tb, functools.partial(runs, True))
        fill(True)


def _dispatch_rows(h2, route, tables, n_blocks):
    t, d = h2.shape
    tm = TM_ROW
    slab = d // 2 // 128
    assert tm == TM_MOE and t // tm >= 2
    return pl.pallas_call(
        _dispatch_kernel,
        grid_spec=pltpu.PrefetchScalarGridSpec(
            num_scalar_prefetch=len(tables),
            grid=(t // tm,),
            in_specs=[pl.BlockSpec((tm, d), lambda i, *_: (i, 0)),
                      pl.BlockSpec((tm, ROUTER_LANES), lambda i, *_: (i, 0))],
            out_specs=pl.BlockSpec(memory_space=pl.ANY),
            scratch_shapes=[pltpu.VMEM((2, TOP_K * tm * slab, 128), jnp.uint32),
                            pltpu.VMEM((tm * slab, 128), jnp.uint32), pltpu.SemaphoreType.DMA((3,))]),
        out_shape=jax.ShapeDtypeStruct((n_blocks * TM_MOE * slab, 128), jnp.uint32),
        compiler_params=_params("arbitrary"),
        name="moe_dispatch",
    )(*tables, h2, route)


def _expert_up_kernel(se_ref, sj_ref, sb_ref, oj_ref, ob_ref, fst_ref, gp_ref, nxt_ref, ne_ref, nj_ref, ns_ref,
                      x_ref, wg_hbm, wu_hbm, o_ref, wg_st, wu_st, sems):
    live = pl.program_id(0) < ns_ref[0]
    _stream_group_weights(se_ref, sj_ref, fst_ref, gp_ref, nxt_ref, ne_ref, nj_ref, live,
                          ((wg_hbm, wg_st), (wu_hbm, wu_st)), sems)

    @pl.when(live)
    def _():
        x = _unpack_bf16_pairs(_load_from_slabs(x_ref, TM_MOE))
        slot = gp_ref[pl.program_id(0)]
        a = jnp.dot(x, wg_st[slot], preferred_element_type=F32)
        u = jnp.dot(x, wu_st[slot], preferred_element_type=F32)
        o_ref[...] = ((a * _sigmoid(a)) * u).astype(o_ref.dtype)

    @pl.when(jnp.logical_not(live))
    def _():
        o_ref[...] = jnp.zeros_like(o_ref)


def _stream_group_weights(se_ref, sj_ref, fst_ref, gp_ref, nxt_ref, ne_ref, nj_ref, live, weights, sems):
    s = pl.program_id(0)

    def copies(e, j, slot):
        out = []
        for k, (w_hbm, w_st) in enumerate(weights):
            tn = w_st.shape[2]
            src = w_hbm.at[e, :, pl.ds(pl.multiple_of(j * tn, tn), tn)]
            out.append(pltpu.make_async_copy(src, w_st.at[slot], sems.at[k, slot]))
        return out

    def first_step(slot):
        cur = copies(se_ref[s], sj_ref[s], slot)

        @pl.when(s == 0)
        def _():
            for cp in cur:
                cp.start(priority=1)

        @pl.when(nxt_ref[s] == 1)
        def _():
            for cp in copies(ne_ref[s], nj_ref[s], 1 - slot):
                cp.start(priority=1)

        for cp in cur:
            cp.wait()

    for slot in range(2):
        pl.when(live & (fst_ref[s] == 1) & (gp_ref[s] == slot))(functools.partial(first_step, slot))


_N_SCHED = 11


def _sched_map(fn):
    return lambda s, *refs: fn(s, *refs[:_N_SCHED])


def _expert_up(xs, w_gate, w_up, sched, n_steps):
    d, de = w_gate.shape[1], w_gate.shape[2]
    tm, tf = TM_MOE, TF_MOE
    slab = d // 2 // 128
    r = xs.shape[0] // slab
    return pl.pallas_call(
        _expert_up_kernel,
        grid_spec=pltpu.PrefetchScalarGridSpec(
            num_scalar_prefetch=_N_SCHED,
            grid=(n_steps,),
            in_specs=[pl.BlockSpec((tm * slab, 128), _sched_map(lambda s, se, sj, sb, *_: (sb[s], 0))),
                      pl.BlockSpec(memory_space=pl.ANY), pl.BlockSpec(memory_space=pl.ANY)],
            out_specs=pl.BlockSpec((tm, tf), _sched_map(lambda s, se, sj, sb, oj, ob, *_: (ob[s], oj[s]))),
            scratch_shapes=[pltpu.VMEM((2, d, tf), F32), pltpu.VMEM((2, d, tf), F32),
                            pltpu.SemaphoreType.DMA((2, 2))]),
        out_shape=jax.ShapeDtypeStruct((r, de), BF16),
        compiler_params=_params("arbitrary"),
        name="moe_gate_up",
    )(*sched, xs, w_gate, w_up)


def _expert_down_kernel(se_ref, sj_ref, sb_ref, oj_ref, ob_ref, fst_ref, gp_ref, nxt_ref, ne_ref, nj_ref, ns_ref,
                        h_ref, wd_hbm, o_ref, wd_st, sems):
    live = pl.program_id(0) < ns_ref[0]
    _stream_group_weights(se_ref, sj_ref, fst_ref, gp_ref, nxt_ref, ne_ref, nj_ref, live, ((wd_hbm, wd_st),), sems)

    @pl.when(live)
    def _():
        y = jnp.dot(h_ref[...].astype(F32), wd_st[gp_ref[pl.program_id(0)]], preferred_element_type=F32)
        _store_as_slabs(o_ref, _pack_bf16_pairs(y))

    @pl.when(jnp.logical_not(live))
    def _():
        o_ref[...] = jnp.zeros_like(o_ref)


def _expert_down(hid, w_down, sched, n_steps):
    r, de = hid.shape
    d = w_down.shape[2]
    tm = TM_MOE
    slab = d // 2 // 128
    return pl.pallas_call(
        _expert_down_kernel,
        grid_spec=pltpu.PrefetchScalarGridSpec(
            num_scalar_prefetch=_N_SCHED,
            grid=(n_steps,),
            in_specs=[pl.BlockSpec((tm, de), _sched_map(lambda s, se, sj, sb, *_: (sb[s], 0))),
                      pl.BlockSpec(memory_space=pl.ANY)],
            out_specs=pl.BlockSpec((tm * slab, 128), _sched_map(lambda s, se, sj, sb, oj, ob, *_: (ob[s], 0))),
            scratch_shapes=[pltpu.VMEM((2, de, d), F32), pltpu.SemaphoreType.DMA((1, 2))]),
        out_shape=jax.ShapeDtypeStruct((r * slab, 128), jnp.uint32),
        compiler_params=_params("arbitrary"),
        name="moe_down",
    )(*sched, hid, w_down)


def _combine_kernel(npb, alpha, pos_ref, y_hbm, x1_ref, rt_ref, gt_ref, g_ref, b_ref,
                    op_ref, os_ref, buf_ref, sems):
    i = pl.program_id(0)
    n = pl.num_programs(0)
    tm = x1_ref.shape[0]
    slab = buf_ref.shape[2] // tm

    def rows(wait, blk, par):
        def body(g, carry):
            for u in range(ROW_UNROLL):
                r = g * ROW_UNROLL + u
                for k in range(TOP_K):
                    cp = _slab_copy(y_hbm, buf_ref.at[par, k], pos_ref[(blk * tm + r) * TOP_K + k], r, slab,
                                    sems.at[par])
                    cp.wait() if wait else cp.start(priority=k % 2)
            return carry
        lax.fori_loop(0, tm // ROW_UNROLL, body, 0)

    @pl.when(i == 0)
    def _():
        rows(False, 0, 0)

    @pl.when(i + 1 < n)
    def _():
        _on_parity(i + 1, functools.partial(rows, False))

    _on_parity(i, functools.partial(rows, True))
    rt = rt_ref[...]
    par = lax.rem(i, 2)
    expert_rows = lambda k: _unpack_bf16_pairs(_load_from_slabs(buf_ref.at[par, k], tm))
    f = rt[:, 2:3] * expert_rows(0) + rt[:, 3:4] * expert_rows(1)
    out = _ln_rows(alpha * x1_ref[...] + gt_ref[...] * f) * g_ref[...] + b_ref[...]

    @pl.when(i < npb)
    def _():
        op_ref[...] = out

    @pl.when(i >= npb)
    def _():
        os_ref[...] = out


def _combine(yb, pos, x1, route, gt, ln_g, ln_b, alpha, tp, dec_seq):
    t, d = x1.shape
    tm = TM_ROW
    npb = tp // tm
    cidx = lambda i: jnp.where(i < npb, 0, 1 + (jnp.maximum(i - npb, 0) * tm) // dec_seq)
    vec = pl.BlockSpec((1, d), lambda i, *_: (0, 0))
    return pl.pallas_call(
        functools.partial(_combine_kernel, npb, alpha),
        grid_spec=pltpu.PrefetchScalarGridSpec(
            num_scalar_prefetch=1,
            grid=(t // tm,),
            in_specs=[pl.BlockSpec(memory_space=pl.ANY),
                      pl.BlockSpec((tm, d), lambda i, *_: (i, 0)),
                      pl.BlockSpec((tm, ROUTER_LANES), lambda i, *_: (i, 0)),
                      pl.BlockSpec((None, 1, d), lambda i, *_: (cidx(i), 0, 0)),
                      vec, vec],
            out_specs=[pl.BlockSpec((tm, d), lambda i, *_: (jnp.minimum(i, npb - 1), 0)),
                       pl.BlockSpec((tm, d), lambda i, *_: (jnp.maximum(i - npb, 0), 0))],
            scratch_shapes=[pltpu.VMEM((2, TOP_K, tm * (d // 2 // 128), 128), jnp.uint32),
                            pltpu.SemaphoreType.DMA((2,))]),
        out_shape=[jax.ShapeDtypeStruct((tp, d), F32), jax.ShapeDtypeStruct((t - tp, d), F32)],
        compiler_params=_params("arbitrary"),
        name="moe_combine_ln",
    )(pos, yb, x1, route, gt, ln_g.reshape(1, d), ln_b.reshape(1, d))


def _dispatch_plan(route, counts_f, block_tab, n_tok):
    tm = TM_MOE
    n_blocks = n_tok * TOP_K // tm + N_EXPERTS
    i32 = jnp.int32
    eid = route[:, 0:TOP_K].astype(i32).reshape(-1)
    slot = route[:, 4:4 + TOP_K].astype(i32).reshape(-1)
    counts = counts_f[0, :N_EXPERTS].astype(i32)
    nb = (counts + tm - 1) // tm

    def cumsum(x):
        idx = jnp.arange(x.shape[0], dtype=i32)
        return jnp.sum(jnp.where(idx[None, :] <= idx[:, None], x[None, :], 0), axis=1)

    nb_end = cumsum(nb)
    bs = nb_end - nb
    n_act = nb_end[-1]
    experts = jnp.arange(N_EXPERTS, dtype=i32)
    look = lambda table, e: jnp.sum(jnp.where(e[:, None] == experts[None, :], table[None, :], 0), axis=1)
    tab = block_tab[:, :3, :N_EXPERTS].astype(i32)
    run_dst = (bs * tm)[None, :] + tab[:, 2]
    dispatch_tables = (tab[:, 0].reshape(-1), tab[:, 1].reshape(-1), run_dst.reshape(-1),
                       bs * tm + counts, nb * tm - counts, n_act.reshape(1))

    def schedule(n_inner):
        n_steps = n_inner * n_blocks
        n_live = n_inner * n_act

        def decode(step):
            s = jnp.minimum(step, n_live - 1)
            e = jnp.minimum(jnp.sum((s[:, None] >= n_inner * nb_end[None, :]).astype(i32), axis=1), N_EXPERTS - 1)
            nbe = jnp.maximum(look(nb, e), 1)
            loc = s - n_inner * look(bs, e)
            return e, loc // nbe, loc % nbe, nbe, look(bs, e)

        step = jnp.arange(n_steps, dtype=i32)
        e, sj, bi, nbe, bse = decode(step)
        live = step < n_live
        nxt_step = jnp.minimum(step, n_live - 1) + nbe - bi
        ne, nj, _, _, _ = decode(nxt_step)
        fst = (live & (bi == 0)).astype(i32)
        gp = (cumsum(fst) - 1) % 2
        nxt = (live & (bi == 0) & (nxt_step < n_live)).astype(i32)
        spare = jnp.maximum(step - n_live, 0)
        oj = jnp.where(live, sj, spare % n_inner)
        ob = jnp.where(live, bse + bi, n_act + spare // n_inner)
        sched = (e, sj, bse + bi, oj, ob, fst, gp, nxt, ne, nj, n_live.reshape(1))
        return tuple(a.astype(i32) for a in sched), n_steps

    pos = look(bs * tm, eid) + slot
    return pos, dispatch_tables, n_blocks, schedule


def kernel(x_prompt, x_sample, state_ret_fwd, state_ret_bwd, c, c_ctx, w_mod, b_mod, w_in, conv_w, conv_b, conv_ln_g, conv_ln_b, ret_decay_fwd, ret_decay_bwd, ret_gn_g, w_out, ln1_g, ln1_b, w_grp, b_grp, w_exp, b_exp, w_gate, w_up, w_down, ln2_g, ln2_b):
    depth = w_mod.shape[0]
    assert depth == 1, "single-layer step"
    bp, sp, d = x_prompt.shape
    bs_, ss, _ = x_sample.shape
    tp, ts = bp * sp, bs_ * ss
    t = tp + ts
    conv_width = conv_w.shape[2]
    ret_width = ret_gn_g.shape[1]
    assert ret_width == RET_HEADS * RET_D and sp % CHUNK == 0 and ss % CHUNK == 0
    alpha = (2.0 * depth) ** 0.25

    xp = x_prompt.reshape(tp, d)
    xs = x_sample.reshape(ts, d)
    cond8 = jnp.zeros((8, d), F32).at[0].set(c_ctx).at[1:1 + bs_].set(c)
    m = _modulation(cond8, w_mod[0], b_mod[0])
    sh1, sc1, gt1, sh2, sc2, gt2 = [m[:, k * d:(k + 1) * d].reshape(8, 1, d) for k in range(6)]

    h = _ln_modulate(xp, xs, sc1, sh1, ss)
    u_glu = _glu_proj(h, w_in[0], conv_width)
    cos_t, sin_t = _rope_tables(ss, RET_D)
    qkvg = _qkvg_proj(h, w_in[0], cos_t, sin_t, 2 * conv_width, 4 * ret_width, tp)

    u = _conv_module(u_glu, conv_w[0], conv_b[0], conv_ln_g[0], conv_ln_b[0], tp, ss)

    lg = jnp.stack([jax.nn.log_sigmoid(ret_decay_fwd[0].astype(F32)),
                    jax.nn.log_sigmoid(ret_decay_bwd[0].astype(F32))])
    gn = ret_gn_g[0].reshape(1, ret_width)
    r_p, new_f, new_b = _retention(qkvg, lg, gn, sp, 0, bp)
    r_s = _retention(qkvg, lg, gn, ss, tp // ss, bs_, state_ret_fwd, state_ret_bwd)
    v = _out_proj(u, jnp.concatenate([r_p, r_s], axis=0), w_out[0], xp, xs, gt1, alpha, ss)

    w_router = jnp.zeros((d, ROUTER_LANES), F32).at[:, :N_GROUPS].set(w_grp[0]).at[:, N_GROUPS:N_GROUPS + N_EXPERTS].set(w_exp[0]).astype(BF16)
    b_router = jnp.zeros((1, ROUTER_LANES), F32).at[0, :N_GROUPS].set(b_grp[0]).at[0, N_GROUPS:N_GROUPS + N_EXPERTS].set(b_exp[0])
    x1, h2, route, counts, block_tab = _post_mix(v, ln1_g[0], ln1_b[0], sc2, sh2, w_router, b_router, tp, ss)

    pos, dispatch_tables, n_blocks, schedule = _dispatch_plan(route, counts, block_tab, t)
    xg = _dispatch_rows(h2, route, dispatch_tables, n_blocks)
    sched_up, n_up = schedule(w_gate.shape[3] // TF_MOE)
    hid = _expert_up(xg, w_gate[0], w_up[0], sched_up, n_up)
    sched_dn, n_dn = schedule(1)
    yb = _expert_down(hid, w_down[0], sched_dn, n_dn)

    out_p, out_s = _combine(yb, pos, x1, route, gt2, ln2_g[0], ln2_b[0], alpha, tp, ss)
    return (out_p.reshape(bp, sp, d), out_s.reshape(bs_, ss, d), new_f, new_b)
```

```python
import functools

import numpy as np
import jax
import jax.numpy as jnp
from jax import lax
from jax.experimental import pallas as pl
from jax.experimental.pallas import tpu as pltpu

F32 = jnp.float32
BF16 = jnp.bfloat16

LN_EPS = 1e-5
CONV_K = 31
CONV_HALO = 16
RET_HEADS = 8
RET_D = 256
HEADS_PER_STEP = 4
CHUNK = 128
GRID_W = 64
ROPE_BASE = 10000.0
N_GROUPS = 4
EXPERTS_PER_GROUP = 8
N_EXPERTS = N_GROUPS * EXPERTS_PER_GROUP
TOP_K = 2
ROUTER_LANES = 128
VMEM_LIMIT = 56 * 1024 * 1024
TM_PROJ = 1024
TM_ROW = 256
TM_MOE = 256
TF_MOE = 512
ROW_UNROLL = 8


def _params(*sem):
    return pltpu.CompilerParams(dimension_semantics=tuple(sem), vmem_limit_bytes=VMEM_LIMIT)


def _sigmoid(x):
    return 1.0 / (1.0 + jnp.exp(-x))


def _ln_rows(x):
    mu = jnp.mean(x, axis=-1, keepdims=True)
    xc = x - mu
    var = jnp.mean(xc * xc, axis=-1, keepdims=True)
    return xc * lax.rsqrt(var + LN_EPS)


def _mod_kernel(c_ref, w_ref, b_ref, o_ref):
    c = c_ref[...]
    s = c * _sigmoid(c)
    o_ref[...] = jnp.dot(s.astype(BF16), w_ref[...].astype(BF16), preferred_element_type=F32) + b_ref[...]


def _modulation(cond8, w_mod, b_mod):
    d, n = w_mod.shape
    tn = 1024
    return pl.pallas_call(
        _mod_kernel,
        grid=(n // tn,),
        in_specs=[pl.BlockSpec((8, d), lambda j: (0, 0)),
                  pl.BlockSpec((d, tn), lambda j: (0, j)),
                  pl.BlockSpec((1, tn), lambda j: (0, j))],
        out_specs=pl.BlockSpec((8, tn), lambda j: (0, j)),
        out_shape=jax.ShapeDtypeStruct((8, n), F32),
        compiler_params=_params("arbitrary"),
        name="modulation",
    )(cond8, w_mod, b_mod.reshape(1, n))


def _ln_mod_kernel(npb, xp_ref, xs_ref, sc_ref, sh_ref, o_ref):
    def body(x_ref):
        y = _ln_rows(x_ref[...])
        o_ref[...] = (y * (1.0 + sc_ref[...]) + sh_ref[...]).astype(o_ref.dtype)

    i = pl.program_id(0)
    pl.when(i < npb)(lambda: body(xp_ref))
    pl.when(i >= npb)(lambda: body(xs_ref))


def _ln_modulate(xp, xs, sc, sh, dec_seq):
    (tp, d), ts = xp.shape, xs.shape[0]
    tm = 512
    npb, nsb = tp // tm, ts // tm
    cidx = lambda i: jnp.where(i < npb, 0, 1 + (jnp.maximum(i - npb, 0) * tm) // dec_seq)
    return pl.pallas_call(
        functools.partial(_ln_mod_kernel, npb),
        grid=(npb + nsb,),
        in_specs=[pl.BlockSpec((tm, d), lambda i: (jnp.minimum(i, npb - 1), 0)),
                  pl.BlockSpec((tm, d), lambda i: (jnp.maximum(i - npb, 0), 0)),
                  pl.BlockSpec((None, 1, d), lambda i: (cidx(i), 0, 0)),
                  pl.BlockSpec((None, 1, d), lambda i: (cidx(i), 0, 0))],
        out_specs=pl.BlockSpec((tm, d), lambda i: (i, 0)),
        out_shape=jax.ShapeDtypeStruct((tp + ts, d), BF16),
        compiler_params=_params("arbitrary"),
        name="ln_modulate",
    )(xp, xs, sc, sh)


def _glu_proj_kernel(h_ref, wv_ref, wg_ref, o_ref):
    h = h_ref[...]
    a = jnp.dot(h, wv_ref[...].astype(BF16), preferred_element_type=F32)
    g = jnp.dot(h, wg_ref[...].astype(BF16), preferred_element_type=F32)
    o_ref[...] = a * _sigmoid(g)


def _glu_proj(h, w_in, conv_width):
    t, d = h.shape
    tm, tn = TM_PROJ, 256
    goff = conv_width // tn
    return pl.pallas_call(
        _glu_proj_kernel,
        grid=(t // tm, conv_width // tn),
        in_specs=[pl.BlockSpec((tm, d), lambda i, j: (i, 0)),
                  pl.BlockSpec((d, tn), lambda i, j: (0, j)),
                  pl.BlockSpec((d, tn), lambda i, j: (0, j + goff))],
        out_specs=pl.BlockSpec((tm, tn), lambda i, j: (i, j)),
        out_shape=jax.ShapeDtypeStruct((t, conv_width), F32),
        compiler_params=_params("arbitrary", "arbitrary"),
        name="glu_proj",
    )(h, w_in, w_in)


def _qkvg_proj_kernel(npb, tiles_per_part, k_scale, h_ref, w_ref, cos_ref, sin_ref, o_ref):
    i, j = pl.program_id(0), pl.program_id(1)
    z = jnp.dot(h_ref[...], w_ref[...].astype(BF16), preferred_element_type=F32)
    tpp = tiles_per_part

    @pl.when(j >= 3 * tpp)
    def _():
        o_ref[...] = (z * _sigmoid(z)).astype(o_ref.dtype)

    @pl.when((j >= 2 * tpp) & (j < 3 * tpp))
    def _():
        o_ref[...] = z.astype(o_ref.dtype)

    @pl.when(j < 2 * tpp)
    def _():
        zz = z * jnp.where(j >= tpp, k_scale, 1.0).astype(F32)

        @pl.when(i < npb)
        def _():
            o_ref[...] = zz.astype(o_ref.dtype)

        @pl.when(i >= npb)
        def _():
            for s in range(zz.shape[1] // 128):
                cs = slice(s * 128, (s + 1) * 128)
                ts_ = slice((s * 128) % RET_D, (s * 128) % RET_D + 128)
                zs = zz[:, cs]
                o_ref[:, cs] = (zs * cos_ref[:, ts_] + pltpu.roll(zs, 64, axis=1) * sin_ref[:, ts_]).astype(o_ref.dtype)


def _qkvg_proj(h, w_in, cos_t, sin_t, col0, n_cols, tp):
    t, d = h.shape
    tm, tn = TM_PROJ, 512
    assert cos_t.shape == (tm, RET_D), "a projection row tile is one latent sequence"
    npb = tp // tm
    tpp = (n_cols // 4) // tn
    return pl.pallas_call(
        functools.partial(_qkvg_proj_kernel, npb, tpp, RET_D ** -0.5),
        grid=(t // tm, n_cols // tn),
        in_specs=[pl.BlockSpec((tm, d), lambda i, j: (i, 0)),
                  pl.BlockSpec((d, tn), lambda i, j: (0, j + col0 // tn)),
                  pl.BlockSpec((tm, RET_D), lambda i, j: (0, 0)),
                  pl.BlockSpec((tm, RET_D), lambda i, j: (0, 0))],
        out_specs=pl.BlockSpec((tm, tn), lambda i, j: (i, j)),
        out_shape=jax.ShapeDtypeStruct((t, n_cols), BF16),
        compiler_params=_params("arbitrary", "arbitrary"),
        name="qkvg_proj",
    )(h, w_in, cos_t, sin_t)


def _rope_tables(n_tok, width):
    rows = n_tok // GRID_W
    r_idx = np.repeat(np.arange(rows), GRID_W).astype(np.float64)
    c_idx = np.tile(np.arange(GRID_W), rows).astype(np.float64)
    dq = RET_D // 2
    inv = ROPE_BASE ** (-np.arange(dq // 2, dtype=np.float64) / (dq // 2))
    sign = np.concatenate([-np.ones(dq // 2), np.ones(dq // 2)])
    cos_h, sin_h = [], []
    for idx in (r_idx, c_idx):
        ang = idx[:, None] * inv
        cos_h.append(np.concatenate([np.cos(ang), np.cos(ang)], axis=1))
        sin_h.append(np.concatenate([np.sin(ang), np.sin(ang)], axis=1) * sign)
    cos_h, sin_h = np.concatenate(cos_h, axis=1), np.concatenate(sin_h, axis=1)
    reps = width // RET_D
    return (jnp.asarray(np.tile(cos_h, (1, reps)), F32), jnp.asarray(np.tile(sin_h, (1, reps)), F32))


def _conv_kernel(npb, tiles_per_seq, uc_ref, up_ref, un_ref, w_ref, b_ref, g_ref, bt_ref, o_ref, buf_ref, y_ref):
    r = pl.program_id(0)
    tm = uc_ref.shape[0]
    ncg = buf_ref.shape[0]
    t = lax.rem(jnp.maximum(r - npb, 0), tiles_per_seq)
    is_s = r >= npb
    has_prev = is_s & (t != 0)
    has_next = is_s & (t != tiles_per_seq - 1)
    for cg in range(ncg):
        cs = slice(cg * 128, (cg + 1) * 128)
        buf_ref[cg, 0:CONV_HALO, :] = jnp.where(has_prev, up_ref[:, cs], 0.0)
        buf_ref[cg, CONV_HALO:CONV_HALO + tm, :] = uc_ref[:, cs]
        buf_ref[cg, CONV_HALO + tm:, :] = jnp.where(has_next, un_ref[:, cs], 0.0)

    off = CONV_HALO - CONV_K // 2

    def body(cg, carry):
        acc = jnp.zeros((tm, 128), F32)
        for tap in range(CONV_K):
            acc = acc + buf_ref[cg, off + tap:off + tap + tm, :] * w_ref[cg, tap:tap + 1, :]
        y_ref[cg] = acc + b_ref[cg]
        return carry

    lax.fori_loop(0, ncg, body, 0)

    n_ch = ncg * 128
    tot = y_ref[0]
    for cg in range(1, ncg):
        tot = tot + y_ref[cg]
    mu = jnp.sum(tot, axis=1, keepdims=True) * (1.0 / n_ch)
    sq = jnp.zeros((tm, 128), F32)
    for cg in range(ncg):
        dv = y_ref[cg] - mu
        sq = sq + dv * dv
    var = jnp.sum(sq, axis=1, keepdims=True) * (1.0 / n_ch)
    rstd = lax.rsqrt(var + LN_EPS)
    for cg in range(ncg):
        cs = slice(cg * 128, (cg + 1) * 128)
        v = (y_ref[cg] - mu) * rstd * g_ref[cg] + bt_ref[cg]
        o_ref[:, cs] = (v * _sigmoid(v)).astype(o_ref.dtype)


def _conv_module(u, conv_w, conv_b, ln_g, ln_b, tp, dec_seq):
    t, c = u.shape
    tm = TM_ROW
    ncg = c // 128
    npb = tp // tm
    hb = tm // CONV_HALO
    n_halo_blocks = t // CONV_HALO
    w3 = jnp.zeros((32, c), F32).at[:CONV_K].set(conv_w).reshape(32, ncg, 128).transpose(1, 0, 2)
    vec = lambda a: a.reshape(ncg, 1, 128)
    return pl.pallas_call(
        functools.partial(_conv_kernel, npb, dec_seq // tm),
        grid=(t // tm,),
        in_specs=[pl.BlockSpec((tm, c), lambda r: (r, 0)),
                  pl.BlockSpec((CONV_HALO, c), lambda r: (jnp.maximum(r * hb - 1, 0), 0)),
                  pl.BlockSpec((CONV_HALO, c), lambda r: (jnp.minimum((r + 1) * hb, n_halo_blocks - 1), 0)),
                  pl.BlockSpec((ncg, 32, 128), lambda r: (0, 0, 0)),
                  pl.BlockSpec((ncg, 1, 128), lambda r: (0, 0, 0)),
                  pl.BlockSpec((ncg, 1, 128), lambda r: (0, 0, 0)),
                  pl.BlockSpec((ncg, 1, 128), lambda r: (0, 0, 0))],
        out_specs=pl.BlockSpec((tm, c), lambda r: (r, 0)),
        out_shape=jax.ShapeDtypeStruct((t, c), BF16),
        scratch_shapes=[pltpu.VMEM((ncg, tm + 2 * CONV_HALO, 128), F32),
                        pltpu.VMEM((ncg, tm, 128), F32)],
        compiler_params=_params("arbitrary"),
        name="conv_ln_swish",
    )(u, u, u, w3, vec(conv_b), vec(ln_g), vec(ln_b))


def _retention_kernel(nc, has_init, lg_ref, q_ref, k_ref, v_ref, g_ref, gn_ref, *rest):
    if has_init:
        s0f_ref, s0b_ref, r_ref, o_ref, sf_ref, sb_ref = rest
    else:
        s0f_ref = s0b_ref = None
        r_ref, sf_ref, sb_ref, o_ref = rest
    for hh in range(HEADS_PER_STEP):
        cols = slice(hh * RET_D, (hh + 1) * RET_D)
        _retention_head(nc, lg_ref, pl.program_id(1) * HEADS_PER_STEP + hh,
                        q_ref.at[:, cols], k_ref.at[:, cols], v_ref.at[:, cols], g_ref.at[:, cols],
                        gn_ref.at[:, cols], None if s0f_ref is None else s0f_ref.at[hh],
                        None if s0b_ref is None else s0b_ref.at[hh],
                        r_ref.at[:, cols], sf_ref.at[hh], sb_ref.at[hh], o_ref.at[hh])


def _retention_head(nc, lg_ref, hd, q_ref, k_ref, v_ref, g_ref, gn_ref, s0f_ref, s0b_ref, r_ref, sf_ref, sb_ref, o_ref):
    has_init = s0f_ref is not None
    lgf, lgb = lg_ref[0, hd], lg_ref[1, hd]
    c = CHUNK
    row = lax.broadcasted_iota(jnp.int32, (c, c), 0)
    col = lax.broadcasted_iota(jnp.int32, (c, c), 1)
    diff = (row - col).astype(F32)
    dec = (jnp.where(diff >= 0, jnp.exp(jnp.maximum(diff, 0.0) * lgf), 0.0)
           + jnp.where(diff <= 0, jnp.exp(jnp.maximum(-diff, 0.0) * lgb), 0.0))
    pos = lax.broadcasted_iota(jnp.int32, (c, 1), 0).astype(F32)
    xi_f = jnp.exp((pos + 1.0) * lgf)
    zeta_f = jnp.exp((c - 1.0 - pos) * lgf)
    xi_b = jnp.exp((c - pos) * lgb)
    zeta_b = jnp.exp(pos * lgb)
    gch_f = jnp.exp(jnp.full((1, RET_D), c, F32) * lgf)
    gch_b = jnp.exp(jnp.full((1, RET_D), c, F32) * lgb)

    if has_init:
        sf_ref[...] = s0f_ref[...]
        sb_ref[...] = s0b_ref[...]
    else:
        sf_ref[...] = jnp.zeros_like(sf_ref)
        sb_ref[...] = jnp.zeros_like(sb_ref)

    def chunk(ci):
        sl = slice(ci * c, (ci + 1) * c)
        return q_ref[sl, :], k_ref[sl, :], v_ref[sl, :], sl

    def state_update(s_ref, kc, vc, zeta, gch):
        kz = (kc.astype(F32) * zeta).T.astype(BF16)
        s_ref[...] = gch * s_ref[...] + jnp.dot(kz, vc, preferred_element_type=F32)

    for ci in range(nc):
        qc, kc, vc, sl = chunk(ci)
        s = lax.dot_general(qc, kc, (((1,), (1,)), ((), ())), preferred_element_type=F32)
        p = (s * dec).astype(BF16)
        o = jnp.dot(p, vc, preferred_element_type=F32)
        o = o + jnp.dot(qc, sf_ref[...].astype(BF16), preferred_element_type=F32) * xi_f
        o_ref[sl, :] = o
        state_update(sf_ref, kc, vc, zeta_f, gch_f)

    for ci in reversed(range(nc)):
        qc, kc, vc, sl = chunk(ci)
        o_ref[sl, :] = o_ref[sl, :] + jnp.dot(qc, sb_ref[...].astype(BF16), preferred_element_type=F32) * xi_b
        state_update(sb_ref, kc, vc, zeta_b, gch_b)

    y = _ln_rows(o_ref[...])
    r_ref[...] = (g_ref[...].astype(F32) * (y * gn_ref[...])).astype(r_ref.dtype)


def _retention(qkvg, lg, gn, n, row_blk0, n_seq, s0f=None, s0b=None):
    has_init = s0f is not None
    nh, dd, hps = RET_HEADS, RET_D, HEADS_PER_STEP
    wd = hps * dd
    col = lambda part: (lambda b, h: (b + row_blk0, part * (nh // hps) + h))
    st_spec = pl.BlockSpec((None, None, hps, dd, dd), lambda b, h: (b, 0, h, 0, 0))
    in_specs = [pl.BlockSpec(memory_space=pltpu.SMEM),
                pl.BlockSpec((n, wd), col(0)), pl.BlockSpec((n, wd), col(1)),
                pl.BlockSpec((n, wd), col(2)), pl.BlockSpec((n, wd), col(3)),
                pl.BlockSpec((1, wd), lambda b, h: (0, h))]
    args = [lg, qkvg, qkvg, qkvg, qkvg, gn]
    r_shape = jax.ShapeDtypeStruct((n_seq * n, nh * dd), BF16)
    r_spec = pl.BlockSpec((n, wd), lambda b, h: (b, h))
    st_scratch = pltpu.VMEM((hps, dd, dd), F32)
    if has_init:
        in_specs += [st_spec, st_spec]
        args += [s0f, s0b]
        out_specs, out_shape = r_spec, r_shape
        scratch = [pltpu.VMEM((hps, n, dd), F32), st_scratch, st_scratch]
    else:
        st_shape = jax.ShapeDtypeStruct((n_seq, 1, nh, dd, dd), F32)
        out_specs, out_shape = [r_spec, st_spec, st_spec], [r_shape, st_shape, st_shape]
        scratch = [pltpu.VMEM((hps, n, dd), F32)]
    return pl.pallas_call(
        functools.partial(_retention_kernel, n // CHUNK, has_init),
        grid=(n_seq, nh // hps),
        in_specs=in_specs, out_specs=out_specs, out_shape=out_shape,
        scratch_shapes=scratch,
        compiler_params=_params("arbitrary", "arbitrary"),
        name="retention_latent" if has_init else "retention_context",
    )(*args)


def _out_proj_kernel(npb, alpha, u_ref, r_ref, w1_ref, w2_ref, xp_ref, xs_ref, gta_ref, gtb_ref, o_ref):
    y = (jnp.dot(u_ref[...], w1_ref[...].astype(BF16), preferred_element_type=F32)
         + jnp.dot(r_ref[...], w2_ref[...].astype(BF16), preferred_element_type=F32))
    i = pl.program_id(0)
    half = y.shape[0] // 2

    def write(x_ref):
        o_ref[:half, :] = alpha * x_ref[:half, :] + gta_ref[...] * y[:half]
        o_ref[half:, :] = alpha * x_ref[half:, :] + gtb_ref[...] * y[half:]

    pl.when(i < npb)(lambda: write(xp_ref))
    pl.when(i >= npb)(lambda: write(xs_ref))


def _out_proj(u, r, w_out, xp, xs, gt, alpha, dec_seq):
    t, kh = u.shape
    d = w_out.shape[1]
    tm, tn = 2 * TM_PROJ, 256
    assert (tm // 2) % dec_seq == 0 or dec_seq % (tm // 2) == 0
    npb = xp.shape[0] // tm
    cidx = lambda i, half: jnp.where(i < npb, 0, 1 + (jnp.maximum(i - npb, 0) * tm + half * (tm // 2)) // dec_seq)
    once = pl.Buffered(1)
    return pl.pallas_call(
        functools.partial(_out_proj_kernel, npb, alpha),
        grid=(t // tm, d // tn),
        in_specs=[pl.BlockSpec((tm, kh), lambda i, j: (i, 0), pipeline_mode=once),
                  pl.BlockSpec((tm, kh), lambda i, j: (i, 0), pipeline_mode=once),
                  pl.BlockSpec((kh, tn), lambda i, j: (0, j)),
                  pl.BlockSpec((kh, tn), lambda i, j: (1, j)),
                  pl.BlockSpec((tm, tn), lambda i, j: (jnp.minimum(i, npb - 1), j)),
                  pl.BlockSpec((tm, tn), lambda i, j: (jnp.maximum(i - npb, 0), j)),
                  pl.BlockSpec((None, 1, tn), lambda i, j: (cidx(i, 0), 0, j)),
                  pl.BlockSpec((None, 1, tn), lambda i, j: (cidx(i, 1), 0, j))],
        out_specs=pl.BlockSpec((tm, tn), lambda i, j: (i, j)),
        out_shape=jax.ShapeDtypeStruct((t, d), F32),
        compiler_params=_params("arbitrary", "arbitrary"),
        name="out_proj_residual",
    )(u, r, w_out, w_out, xp, xs, gt, gt)


def _pack_bf16_pairs(x):
    n = x.shape[1] // 2
    lo = lax.bitcast_convert_type(x[:, :n].astype(BF16).astype(F32), jnp.uint32)
    hi = lax.bitcast_convert_type(x[:, n:].astype(BF16).astype(F32), jnp.uint32)
    return hi | (lo >> 16)


def _unpack_bf16_pairs(p):
    lo = lax.bitcast_convert_type(p << 16, F32)
    hi = lax.bitcast_convert_type(p & jnp.uint32(0xFFFF0000), F32)
    return jnp.concatenate([lo, hi], axis=1)


def _store_as_slabs(ref, rows):
    m, n = rows.shape
    slab = n // 128
    for s in range(slab):
        ref[pl.ds(s, m, stride=slab), :] = rows[:, s * 128:(s + 1) * 128]


def _load_from_slabs(ref, m):
    slab = ref.shape[0] // m
    return jnp.concatenate([ref[pl.ds(s, m, stride=slab), :] for s in range(slab)], axis=1)


def _slab_copy(src, dst, src_row, dst_row, slab, sem):
    return pltpu.make_async_copy(src.at[pl.ds(pl.multiple_of(src_row * slab, slab), slab), :],
                                 dst.at[pl.ds(pl.multiple_of(dst_row * slab, slab), slab), :], sem)


def _post_mix_kernel(v_ref, g1_ref, b1_ref, sc_ref, sh_ref, wr_ref, br_ref, x1_ref, h2_ref, rt_ref, cnt_ref, tab_ref):
    @pl.when(pl.program_id(0) == 0)
    def _():
        cnt_ref[...] = jnp.zeros_like(cnt_ref)

    x1 = _ln_rows(v_ref[...]) * g1_ref[...] + b1_ref[...]
    x1_ref[...] = x1
    h2 = (_ln_rows(x1) * (1.0 + sc_ref[...]) + sh_ref[...]).astype(BF16)
    h2_ref[...] = h2
    logits = jnp.dot(h2, wr_ref[...], preferred_element_type=F32) + br_ref[...]
    lane = lax.broadcasted_iota(jnp.int32, logits.shape, 1)
    big = jnp.int32(ROUTER_LANES)
    neg = jnp.float32(-jnp.inf)

    def first_lane_of_max(vals):
        m = jnp.max(vals, axis=1, keepdims=True)
        return m, jnp.min(jnp.where(vals == m, lane, big), axis=1, keepdims=True)

    lgt = jnp.where(lane < N_GROUPS, logits, neg)
    eg = jnp.exp(lgt - jnp.max(lgt, axis=1, keepdims=True))
    pg = eg / jnp.sum(eg, axis=1, keepdims=True)
    grp_prob, grp = first_lane_of_max(jnp.where(lane < N_GROUPS, pg, -1.0))
    lo = N_GROUPS + grp * EXPERTS_PER_GROUP
    in_grp = (lane >= lo) & (lane < lo + EXPERTS_PER_GROUP)
    let = jnp.where(in_grp, logits, neg)
    ee = jnp.exp(let - jnp.max(let, axis=1, keepdims=True))
    pe = jnp.where(in_grp, ee / jnp.sum(ee, axis=1, keepdims=True), -1.0)
    p1, l1 = first_lane_of_max(pe)
    p2, l2 = first_lane_of_max(jnp.where(lane == l1, -1.0, pe))
    den = p1 + p2
    gate1, gate2 = grp_prob * p1 / den, grp_prob * p2 / den
    e1, e2 = l1 - N_GROUPS, l2 - N_GROUPS
    tm = logits.shape[0]
    hot1 = (lane == e1).astype(F32)
    hot2 = (lane == e2).astype(F32)
    earlier = (lax.broadcasted_iota(jnp.int32, (tm, tm), 0) > lax.broadcasted_iota(jnp.int32, (tm, tm), 1)).astype(BF16)
    before1 = jnp.dot(earlier, hot1.astype(BF16), preferred_element_type=F32)
    n1 = jnp.sum(hot1, axis=0, keepdims=True)
    before2 = jnp.dot(earlier, hot2.astype(BF16), preferred_element_type=F32) + n1
    cnt = cnt_ref[...]
    slot1 = jnp.sum((before1 + cnt) * hot1, axis=1, keepdims=True)
    slot2 = jnp.sum((before2 + cnt) * hot2, axis=1, keepdims=True)
    n_blk = n1 + jnp.sum(hot2, axis=0, keepdims=True)
    lower = (lax.broadcasted_iota(jnp.int32, (ROUTER_LANES, ROUTER_LANES), 0)
             < lax.broadcasted_iota(jnp.int32, (ROUTER_LANES, ROUTER_LANES), 1)).astype(BF16)
    start = jnp.dot(jnp.broadcast_to(n_blk, (8, ROUTER_LANES)).astype(BF16), lower, preferred_element_type=F32)[0:1]
    local1 = jnp.sum((before1 + start) * hot1, axis=1, keepdims=True)
    local2 = jnp.sum((before2 + start) * hot2, axis=1, keepdims=True)
    cnt_ref[...] = cnt + n_blk
    vals = (e1.astype(F32), e2.astype(F32), gate1, gate2, slot1, slot2, local1, local2)
    out = jnp.zeros_like(logits)
    for k, val in enumerate(vals):
        out = jnp.where(lane == k, val, out)
    rt_ref[...] = out
    sub = lax.broadcasted_iota(jnp.int32, (8, ROUTER_LANES), 0)
    tab_ref[...] = jnp.where(sub == 0, n_blk, jnp.where(sub == 1, start, jnp.where(sub == 2, cnt, 0.0)))


def _post_mix(v, ln_g, ln_b, sc, sh, w_router, b_router, tp, dec_seq):
    t, d = v.shape
    tm = TM_ROW
    npb = tp // tm
    cidx = lambda i: jnp.where(i < npb, 0, 1 + (jnp.maximum(i - npb, 0) * tm) // dec_seq)
    row = pl.BlockSpec((tm, d), lambda i: (i, 0))
    vec = pl.BlockSpec((1, d), lambda i: (0, 0))
    cvec = pl.BlockSpec((None, 1, d), lambda i: (cidx(i), 0, 0))
    return pl.pallas_call(
        _post_mix_kernel,
        grid=(t // tm,),
        in_specs=[row, vec, vec, cvec, cvec,
                  pl.BlockSpec((d, ROUTER_LANES), lambda i: (0, 0)),
                  pl.BlockSpec((1, ROUTER_LANES), lambda i: (0, 0))],
        out_specs=[row, row,
                   pl.BlockSpec((tm, ROUTER_LANES), lambda i: (i, 0)),
                   pl.BlockSpec((1, ROUTER_LANES), lambda i: (0, 0)),
                   pl.BlockSpec((None, 8, ROUTER_LANES), lambda i: (i, 0, 0))],
        out_shape=[jax.ShapeDtypeStruct((t, d), F32), jax.ShapeDtypeStruct((t, d), BF16),
                   jax.ShapeDtypeStruct((t, ROUTER_LANES), F32), jax.ShapeDtypeStruct((1, ROUTER_LANES), F32),
                   jax.ShapeDtypeStruct((t // tm, 8, ROUTER_LANES), F32)],
        compiler_params=_params("arbitrary"),
        name="ln_ln_router",
    )(v, ln_g.reshape(1, d), ln_b.reshape(1, d), sc, sh, w_router, b_router)


def _on_parity(blk, fn):
    for par in range(2):
        pl.when(lax.rem(blk, 2) == par)(functools.partial(fn, blk, par))


def _dispatch_kernel(run_n_ref, run_src_ref, run_dst_ref, pad_row_ref, pad_n_ref, nact_ref,
                     h_ref, rt_ref, o_hbm, stage_ref, zero_ref, sems):
    tb = pl.program_id(0)
    n_tb = pl.num_programs(0)
    tm, d = h_ref.shape
    slab = d // 2 // 128
    n_rows_out = o_hbm.shape[0] // slab
    fill_sem = sems.at[2]

    def pieces(wait, src_ref, src_row, dst_row, n, max_size, sem):
        done = jnp.int32(0)
        size = max_size
        while size >= 1:
            take = (n & size) != 0
            src0 = 0 if src_row is None else pl.multiple_of((src_row + done) * slab, slab)
            cp = pltpu.make_async_copy(
                src_ref.at[pl.ds(src0, size * slab), :],
                o_hbm.at[pl.ds(pl.multiple_of((dst_row + done) * slab, slab), size * slab), :], sem)
            pl.when(take)(cp.wait if wait else functools.partial(cp.start, 1))
            done = done + jnp.where(take, size, 0)
            size //= 2

    def runs(wait, blk, par):
        def body(e, carry):
            k = blk * N_EXPERTS + e
            pieces(wait, stage_ref.at[par], run_src_ref[k], run_dst_ref[k], run_n_ref[k], tm, sems.at[par])
            return carry
        lax.fori_loop(0, N_EXPERTS, body, 0)

    def fill(wait):
        def tail(e, carry):
            pieces(wait, zero_ref, None, pad_row_ref[e], pad_n_ref[e], tm // 2, fill_sem)
            return carry

        def spare(b, carry):
            pieces(wait, zero_ref, None, b * tm, jnp.int32(tm), tm, fill_sem)
            return carry

        lax.fori_loop(0, N_EXPERTS, tail, 0)
        lax.fori_loop(nact_ref[0], n_rows_out // tm, spare, 0)

    @pl.when(tb == 0)
    def _():
        zero_ref[...] = jnp.zeros_like(zero_ref)
        fill(False)

    @pl.when(tb >= 2)
    def _():
        _on_parity(tb, lambda blk, par: runs(True, blk - 2, par))

    rt = rt_ref[...]
    local1 = rt[:, 6:7].astype(jnp.int32)
    local2 = rt[:, 7:8].astype(jnp.int32)
    place = lax.broadcasted_iota(jnp.int32, (tm, TOP_K * tm), 1)
    onehot = ((place == local1) | (place == local2)).astype(F32)
    xp = jnp.dot(onehot.T.astype(BF16), h_ref[...], preferred_element_type=F32)
    packed = (lax.bitcast_convert_type(xp[:, d // 2:], jnp.uint32)
              | (lax.bitcast_convert_type(xp[:, :d // 2], jnp.uint32) >> 16))
    _store_as_slabs(stage_ref.at[lax.rem(tb, 2)], packed)
    _on_parity(tb, functools.partial(runs, False))

    @pl.when(tb == n_tb - 1)
    def _():
        _on_parity(tb - 1, functools.partial(runs, True))
        _on_parity(tb, functools.partial(runs, True))
        fill(True)


def _dispatch_rows(h2, route, tables, n_blocks):
    t, d = h2.shape
    tm = TM_ROW
    slab = d // 2 // 128
    assert tm == TM_MOE and t // tm >= 2
    return pl.pallas_call(
        _dispatch_kernel,
        grid_spec=pltpu.PrefetchScalarGridSpec(
            num_scalar_prefetch=len(tables),
            grid=(t // tm,),
            in_specs=[pl.BlockSpec((tm, d), lambda i, *_: (i, 0)),
                      pl.BlockSpec((tm, ROUTER_LANES), lambda i, *_: (i, 0))],
            out_specs=pl.BlockSpec(memory_space=pl.ANY),
            scratch_shapes=[pltpu.VMEM((2, TOP_K * tm * slab, 128), jnp.uint32),
                            pltpu.VMEM((tm * slab, 128), jnp.uint32), pltpu.SemaphoreType.DMA((3,))]),
        out_shape=jax.ShapeDtypeStruct((n_blocks * TM_MOE * slab, 128), jnp.uint32),
        compiler_params=_params("arbitrary"),
        name="moe_dispatch",
    )(*tables, h2, route)


def _expert_up_kernel(se_ref, sj_ref, sb_ref, oj_ref, ob_ref, fst_ref, gp_ref, nxt_ref, ne_ref, nj_ref, ns_ref,
                      x_ref, wg_hbm, wu_hbm, o_ref, wg_st, wu_st, sems):
    live = pl.program_id(0) < ns_ref[0]
    _stream_group_weights(se_ref, sj_ref, fst_ref, gp_ref, nxt_ref, ne_ref, nj_ref, live,
                          ((wg_hbm, wg_st), (wu_hbm, wu_st)), sems)

    @pl.when(live)
    def _():
        x = _unpack_bf16_pairs(_load_from_slabs(x_ref, TM_MOE))
        slot = gp_ref[pl.program_id(0)]
        a = jnp.dot(x, wg_st[slot], preferred_element_type=F32)
        u = jnp.dot(x, wu_st[slot], preferred_element_type=F32)
        o_ref[...] = ((a * _sigmoid(a)) * u).astype(o_ref.dtype)

    @pl.when(jnp.logical_not(live))
    def _():
        o_ref[...] = jnp.zeros_like(o_ref)


def _stream_group_weights(se_ref, sj_ref, fst_ref, gp_ref, nxt_ref, ne_ref, nj_ref, live, weights, sems):
    s = pl.program_id(0)

    def copies(e, j, slot):
        out = []
        for k, (w_hbm, w_st) in enumerate(weights):
            tn = w_st.shape[2]
            src = w_hbm.at[e, :, pl.ds(pl.multiple_of(j * tn, tn), tn)]
            out.append(pltpu.make_async_copy(src, w_st.at[slot], sems.at[k, slot]))
        return out

    def first_step(slot):
        cur = copies(se_ref[s], sj_ref[s], slot)

        @pl.when(s == 0)
        def _():
            for cp in cur:
                cp.start(priority=1)

        @pl.when(nxt_ref[s] == 1)
        def _():
            for cp in copies(ne_ref[s], nj_ref[s], 1 - slot):
                cp.start(priority=1)

        for cp in cur:
            cp.wait()

    for slot in range(2):
        pl.when(live & (fst_ref[s] == 1) & (gp_ref[s] == slot))(functools.partial(first_step, slot))


_N_SCHED = 11


def _sched_map(fn):
    return lambda s, *refs: fn(s, *refs[:_N_SCHED])


def _expert_up(xs, w_gate, w_up, sched, n_steps):
    d, de = w_gate.shape[1], w_gate.shape[2]
    tm, tf = TM_MOE, TF_MOE
    slab = d // 2 // 128
    r = xs.shape[0] // slab
    return pl.pallas_call(
        _expert_up_kernel,
        grid_spec=pltpu.PrefetchScalarGridSpec(
            num_scalar_prefetch=_N_SCHED,
            grid=(n_steps,),
            in_specs=[pl.BlockSpec((tm * slab, 128), _sched_map(lambda s, se, sj, sb, *_: (sb[s], 0))),
                      pl.BlockSpec(memory_space=pl.ANY), pl.BlockSpec(memory_space=pl.ANY)],
            out_specs=pl.BlockSpec((tm, tf), _sched_map(lambda s, se, sj, sb, oj, ob, *_: (ob[s], oj[s]))),
            scratch_shapes=[pltpu.VMEM((2, d, tf), F32), pltpu.VMEM((2, d, tf), F32),
                            pltpu.SemaphoreType.DMA((2, 2))]),
        out_shape=jax.ShapeDtypeStruct((r, de), BF16),
        compiler_params=_params("arbitrary"),
        name="moe_gate_up",
    )(*sched, xs, w_gate, w_up)


def _expert_down_kernel(se_ref, sj_ref, sb_ref, oj_ref, ob_ref, fst_ref, gp_ref, nxt_ref, ne_ref, nj_ref, ns_ref,
                        h_ref, wd_hbm, o_ref, wd_st, sems):
    live = pl.program_id(0) < ns_ref[0]
    _stream_group_weights(se_ref, sj_ref, fst_ref, gp_ref, nxt_ref, ne_ref, nj_ref, live, ((wd_hbm, wd_st),), sems)

    @pl.when(live)
    def _():
        y = jnp.dot(h_ref[...].astype(F32), wd_st[gp_ref[pl.program_id(0)]], preferred_element_type=F32)
        _store_as_slabs(o_ref, _pack_bf16_pairs(y))

    @pl.when(jnp.logical_not(live))
    def _():
        o_ref[...] = jnp.zeros_like(o_ref)


def _expert_down(hid, w_down, sched, n_steps):
    r, de = hid.shape
    d = w_down.shape[2]
    tm = TM_MOE
    slab = d // 2 // 128
    return pl.pallas_call(
        _expert_down_kernel,
        grid_spec=pltpu.PrefetchScalarGridSpec(
            num_scalar_prefetch=_N_SCHED,
            grid=(n_steps,),
            in_specs=[pl.BlockSpec((tm, de), _sched_map(lambda s, se, sj, sb, *_: (sb[s], 0))),
                      pl.BlockSpec(memory_space=pl.ANY)],
            out_specs=pl.BlockSpec((tm * slab, 128), _sched_map(lambda s, se, sj, sb, oj, ob, *_: (ob[s], 0))),
            scratch_shapes=[pltpu.VMEM((2, de, d), F32), pltpu.SemaphoreType.DMA((1, 2))]),
        out_shape=jax.ShapeDtypeStruct((r * slab, 128), jnp.uint32),
        compiler_params=_params("arbitrary"),
        name="moe_down",
    )(*sched, hid, w_down)


def _combine_kernel(npb, alpha, pos_ref, y_hbm, x1_ref, rt_ref, gt_ref, g_ref, b_ref,
                    op_ref, os_ref, buf_ref, sems):
    i = pl.program_id(0)
    n = pl.num_programs(0)
    tm = x1_ref.shape[0]
    slab = buf_ref.shape[2] // tm

    def rows(wait, blk, par):
        def body(g, carry):
            for u in range(ROW_UNROLL):
                r = g * ROW_UNROLL + u
                for k in range(TOP_K):
                    cp = _slab_copy(y_hbm, buf_ref.at[par, k], pos_ref[(blk * tm + r) * TOP_K + k], r, slab,
                                    sems.at[par])
                    cp.wait() if wait else cp.start(priority=1)
            return carry
        lax.fori_loop(0, tm // ROW_UNROLL, body, 0)

    @pl.when(i == 0)
    def _():
        rows(False, 0, 0)

    @pl.when(i + 1 < n)
    def _():
        _on_parity(i + 1, functools.partial(rows, False))

    _on_parity(i, functools.partial(rows, True))
    rt = rt_ref[...]
    par = lax.rem(i, 2)
    expert_rows = lambda k: _unpack_bf16_pairs(_load_from_slabs(buf_ref.at[par, k], tm))
    f = rt[:, 2:3] * expert_rows(0) + rt[:, 3:4] * expert_rows(1)
    out = _ln_rows(alpha * x1_ref[...] + gt_ref[...] * f) * g_ref[...] + b_ref[...]

    @pl.when(i < npb)
    def _():
        op_ref[...] = out

    @pl.when(i >= npb)
    def _():
        os_ref[...] = out


def _combine(yb, pos, x1, route, gt, ln_g, ln_b, alpha, tp, dec_seq):
    t, d = x1.shape
    tm = TM_ROW
    npb = tp // tm
    cidx = lambda i: jnp.where(i < npb, 0, 1 + (jnp.maximum(i - npb, 0) * tm) // dec_seq)
    vec = pl.BlockSpec((1, d), lambda i, *_: (0, 0))
    return pl.pallas_call(
        functools.partial(_combine_kernel, npb, alpha),
        grid_spec=pltpu.PrefetchScalarGridSpec(
            num_scalar_prefetch=1,
            grid=(t // tm,),
            in_specs=[pl.BlockSpec(memory_space=pl.ANY),
                      pl.BlockSpec((tm, d), lambda i, *_: (i, 0)),
                      pl.BlockSpec((tm, ROUTER_LANES), lambda i, *_: (i, 0)),
                      pl.BlockSpec((None, 1, d), lambda i, *_: (cidx(i), 0, 0)),
                      vec, vec],
            out_specs=[pl.BlockSpec((tm, d), lambda i, *_: (jnp.minimum(i, npb - 1), 0)),
                       pl.BlockSpec((tm, d), lambda i, *_: (jnp.maximum(i - npb, 0), 0))],
            scratch_shapes=[pltpu.VMEM((2, TOP_K, tm * (d // 2 // 128), 128), jnp.uint32),
                            pltpu.SemaphoreType.DMA((2,))]),
        out_shape=[jax.ShapeDtypeStruct((tp, d), F32), jax.ShapeDtypeStruct((t - tp, d), F32)],
        compiler_params=_params("arbitrary"),
        name="moe_combine_ln",
    )(pos, yb, x1, route, gt, ln_g.reshape(1, d), ln_b.reshape(1, d))


def _dispatch_plan(route, counts_f, block_tab, n_tok):
    tm = TM_MOE
    n_blocks = n_tok * TOP_K // tm + N_EXPERTS
    i32 = jnp.int32
    eid = route[:, 0:TOP_K].astype(i32).reshape(-1)
    slot = route[:, 4:4 + TOP_K].astype(i32).reshape(-1)
    counts = counts_f[0, :N_EXPERTS].astype(i32)
    nb = (counts + tm - 1) // tm

    def cumsum(x):
        idx = jnp.arange(x.shape[0], dtype=i32)
        return jnp.sum(jnp.where(idx[None, :] <= idx[:, None], x[None, :], 0), axis=1)

    nb_end = cumsum(nb)
    bs = nb_end - nb
    n_act = nb_end[-1]
    experts = jnp.arange(N_EXPERTS, dtype=i32)
    look = lambda table, e: jnp.sum(jnp.where(e[:, None] == experts[None, :], table[None, :], 0), axis=1)
    tab = block_tab[:, :3, :N_EXPERTS].astype(i32)
    run_dst = (bs * tm)[None, :] + tab[:, 2]
    dispatch_tables = (tab[:, 0].reshape(-1), tab[:, 1].reshape(-1), run_dst.reshape(-1),
                       bs * tm + counts, nb * tm - counts, n_act.reshape(1))

    def schedule(n_inner):
        n_steps = n_inner * n_blocks
        n_live = n_inner * n_act

        def decode(step):
            s = jnp.minimum(step, n_live - 1)
            e = jnp.minimum(jnp.sum((s[:, None] >= n_inner * nb_end[None, :]).astype(i32), axis=1), N_EXPERTS - 1)
            nbe = jnp.maximum(look(nb, e), 1)
            loc = s - n_inner * look(bs, e)
            return e, loc // nbe, loc % nbe, nbe, look(bs, e)

        step = jnp.arange(n_steps, dtype=i32)
        e, sj, bi, nbe, bse = decode(step)
        live = step < n_live
        nxt_step = jnp.minimum(step, n_live - 1) + nbe - bi
        ne, nj, _, _, _ = decode(nxt_step)
        fst = (live & (bi == 0)).astype(i32)
        gp = (cumsum(fst) - 1) % 2
        nxt = (live & (bi == 0) & (nxt_step < n_live)).astype(i32)
        spare = jnp.maximum(step - n_live, 0)
        oj = jnp.where(live, sj, spare % n_inner)
        ob = jnp.where(live, bse + bi, n_act + spare // n_inner)
        sched = (e, sj, bse + bi, oj, ob, fst, gp, nxt, ne, nj, n_live.reshape(1))
        return tuple(a.astype(i32) for a in sched), n_steps

    pos = look(bs * tm, eid) + slot
    return pos, dispatch_tables, n_blocks, schedule


def kernel(x_prompt, x_sample, state_ret_fwd, state_ret_bwd, c, c_ctx, w_mod, b_mod, w_in, conv_w, conv_b, conv_ln_g, conv_ln_b, ret_decay_fwd, ret_decay_bwd, ret_gn_g, w_out, ln1_g, ln1_b, w_grp, b_grp, w_exp, b_exp, w_gate, w_up, w_down, ln2_g, ln2_b):
    depth = w_mod.shape[0]
    assert depth == 1, "single-layer step"
    bp, sp, d = x_prompt.shape
    bs_, ss, _ = x_sample.shape
    tp, ts = bp * sp, bs_ * ss
    t = tp + ts
    conv_width = conv_w.shape[2]
    ret_width = ret_gn_g.shape[1]
    assert ret_width == RET_HEADS * RET_D and sp % CHUNK == 0 and ss % CHUNK == 0
    alpha = (2.0 * depth) ** 0.25

    xp = x_prompt.reshape(tp, d)
    xs = x_sample.reshape(ts, d)
    cond8 = jnp.zeros((8, d), F32).at[0].set(c_ctx).at[1:1 + bs_].set(c)
    m = _modulation(cond8, w_mod[0], b_mod[0])
    sh1, sc1, gt1, sh2, sc2, gt2 = [m[:, k * d:(k + 1) * d].reshape(8, 1, d) for k in range(6)]

    h = _ln_modulate(xp, xs, sc1, sh1, ss)
    u_glu = _glu_proj(h, w_in[0], conv_width)
    cos_t, sin_t = _rope_tables(ss, RET_D)
    qkvg = _qkvg_proj(h, w_in[0], cos_t, sin_t, 2 * conv_width, 4 * ret_width, tp)

    u = _conv_module(u_glu, conv_w[0], conv_b[0], conv_ln_g[0], conv_ln_b[0], tp, ss)

    lg = jnp.stack([jax.nn.log_sigmoid(ret_decay_fwd[0].astype(F32)),
                    jax.nn.log_sigmoid(ret_decay_bwd[0].astype(F32))])
    gn = ret_gn_g[0].reshape(1, ret_width)
    r_p, new_f, new_b = _retention(qkvg, lg, gn, sp, 0, bp)
    r_s = _retention(qkvg, lg, gn, ss, tp // ss, bs_, state_ret_fwd, state_ret_bwd)
    r = jnp.concatenate([r_p, r_s], axis=0)

    v = _out_proj(u, r, w_out[0], xp, xs, gt1, alpha, ss)

    w_router = jnp.zeros((d, ROUTER_LANES), F32).at[:, :N_GROUPS].set(w_grp[0]).at[:, N_GROUPS:N_GROUPS + N_EXPERTS].set(w_exp[0]).astype(BF16)
    b_router = jnp.zeros((1, ROUTER_LANES), F32).at[0, :N_GROUPS].set(b_grp[0]).at[0, N_GROUPS:N_GROUPS + N_EXPERTS].set(b_exp[0])
    x1, h2, route, counts, block_tab = _post_mix(v, ln1_g[0], ln1_b[0], sc2, sh2, w_router, b_router, tp, ss)

    pos, dispatch_tables, n_blocks, schedule = _dispatch_plan(route, counts, block_tab, t)
    xg = _dispatch_rows(h2, route, dispatch_tables, n_blocks)
    sched_up, n_up = schedule(w_gate.shape[3] // TF_MOE)
    hid = _expert_up(xg, w_gate[0], w_up[0], sched_up, n_up)
    sched_dn, n_dn = schedule(1)
    yb = _expert_down(hid, w_down[0], sched_dn, n_dn)

    out_p, out_s = _combine(yb, pos, x1, route, gt2, ln2_g[0], ln2_b[0], alpha, tp, ss)
    return (out_p.reshape(bp, sp, d), out_s.reshape(bs_, ss, d), new_f, new_b)
```

```python
import functools

import numpy as np
import jax
import jax.numpy as jnp
from jax import lax
from jax.experimental import pallas as pl
from jax.experimental.pallas import tpu as pltpu

F32 = jnp.float32
BF16 = jnp.bfloat16

LN_EPS = 1e-5
CONV_K = 31
CONV_HALO = 16
RET_HEADS = 8
RET_D = 256
HEADS_PER_STEP = 4
CHUNK = 128
GRID_W = 64
ROPE_BASE = 10000.0
N_GROUPS = 4
EXPERTS_PER_GROUP = 8
N_EXPERTS = N_GROUPS * EXPERTS_PER_GROUP
TOP_K = 2
ROUTER_LANES = 128
VMEM_LIMIT = 56 * 1024 * 1024
TM_PROJ = 1024
TM_ROW = 256
TM_MOE = 256
TF_MOE = 512
ROW_UNROLL = 8


def _params(*sem):
    return pltpu.CompilerParams(dimension_semantics=tuple(sem), vmem_limit_bytes=VMEM_LIMIT)


def _sigmoid(x):
    return 1.0 / (1.0 + jnp.exp(-x))


def _ln_rows(x):
    mu = jnp.mean(x, axis=-1, keepdims=True)
    xc = x - mu
    var = jnp.mean(xc * xc, axis=-1, keepdims=True)
    return xc * lax.rsqrt(var + LN_EPS)


def _mod_kernel(c_ref, w_ref, b_ref, o_ref):
    c = c_ref[...]
    s = c * _sigmoid(c)
    o_ref[...] = jnp.dot(s.astype(BF16), w_ref[...].astype(BF16), preferred_element_type=F32) + b_ref[...]


def _modulation(cond8, w_mod, b_mod):
    d, n = w_mod.shape
    tn = 512
    return pl.pallas_call(
        _mod_kernel,
        grid=(n // tn,),
        in_specs=[pl.BlockSpec((8, d), lambda j: (0, 0)),
                  pl.BlockSpec((d, tn), lambda j: (0, j)),
                  pl.BlockSpec((1, tn), lambda j: (0, j))],
        out_specs=pl.BlockSpec((8, tn), lambda j: (0, j)),
        out_shape=jax.ShapeDtypeStruct((8, n), F32),
        compiler_params=_params("arbitrary"),
        name="modulation",
    )(cond8, w_mod, b_mod.reshape(1, n))


def _ln_mod_kernel(npb, xp_ref, xs_ref, sc_ref, sh_ref, o_ref):
    def body(x_ref):
        y = _ln_rows(x_ref[...])
        o_ref[...] = (y * (1.0 + sc_ref[...]) + sh_ref[...]).astype(o_ref.dtype)

    i = pl.program_id(0)
    pl.when(i < npb)(lambda: body(xp_ref))
    pl.when(i >= npb)(lambda: body(xs_ref))


def _ln_modulate(xp, xs, sc, sh, dec_seq):
    (tp, d), ts = xp.shape, xs.shape[0]
    tm = 512
    npb, nsb = tp // tm, ts // tm
    cidx = lambda i: jnp.where(i < npb, 0, 1 + (jnp.maximum(i - npb, 0) * tm) // dec_seq)
    return pl.pallas_call(
        functools.partial(_ln_mod_kernel, npb),
        grid=(npb + nsb,),
        in_specs=[pl.BlockSpec((tm, d), lambda i: (jnp.minimum(i, npb - 1), 0)),
                  pl.BlockSpec((tm, d), lambda i: (jnp.maximum(i - npb, 0), 0)),
                  pl.BlockSpec((None, 1, d), lambda i: (cidx(i), 0, 0)),
                  pl.BlockSpec((None, 1, d), lambda i: (cidx(i), 0, 0))],
        out_specs=pl.BlockSpec((tm, d), lambda i: (i, 0)),
        out_shape=jax.ShapeDtypeStruct((tp + ts, d), BF16),
        compiler_params=_params("arbitrary"),
        name="ln_modulate",
    )(xp, xs, sc, sh)


def _glu_proj_kernel(h_ref, wv_ref, wg_ref, o_ref):
    h = h_ref[...]
    a = jnp.dot(h, wv_ref[...].astype(BF16), preferred_element_type=F32)
    g = jnp.dot(h, wg_ref[...].astype(BF16), preferred_element_type=F32)
    o_ref[...] = a * _sigmoid(g)


def _glu_proj(h, w_in, conv_width):
    t, d = h.shape
    tm, tn = TM_PROJ, 256
    goff = conv_width // tn
    return pl.pallas_call(
        _glu_proj_kernel,
        grid=(t // tm, conv_width // tn),
        in_specs=[pl.BlockSpec((tm, d), lambda i, j: (i, 0)),
                  pl.BlockSpec((d, tn), lambda i, j: (0, j)),
                  pl.BlockSpec((d, tn), lambda i, j: (0, j + goff))],
        out_specs=pl.BlockSpec((tm, tn), lambda i, j: (i, j)),
        out_shape=jax.ShapeDtypeStruct((t, conv_width), F32),
        compiler_params=_params("arbitrary", "arbitrary"),
        name="glu_proj",
    )(h, w_in, w_in)


def _qkvg_proj_kernel(npb, tiles_per_part, k_scale, h_ref, w_ref, cos_ref, sin_ref, o_ref):
    i, j = pl.program_id(0), pl.program_id(1)
    z = jnp.dot(h_ref[...], w_ref[...].astype(BF16), preferred_element_type=F32)
    tpp = tiles_per_part

    @pl.when(j >= 3 * tpp)
    def _():
        o_ref[...] = (z * _sigmoid(z)).astype(o_ref.dtype)

    @pl.when((j >= 2 * tpp) & (j < 3 * tpp))
    def _():
        o_ref[...] = z.astype(o_ref.dtype)

    @pl.when(j < 2 * tpp)
    def _():
        zz = z * jnp.where(j >= tpp, k_scale, 1.0).astype(F32)

        @pl.when(i < npb)
        def _():
            o_ref[...] = zz.astype(o_ref.dtype)

        @pl.when(i >= npb)
        def _():
            for s in range(zz.shape[1] // 128):
                cs = slice(s * 128, (s + 1) * 128)
                ts_ = slice((s * 128) % RET_D, (s * 128) % RET_D + 128)
                zs = zz[:, cs]
                o_ref[:, cs] = (zs * cos_ref[:, ts_] + pltpu.roll(zs, 64, axis=1) * sin_ref[:, ts_]).astype(o_ref.dtype)


def _qkvg_proj(h, w_in, cos_t, sin_t, col0, n_cols, tp):
    t, d = h.shape
    tm, tn = TM_PROJ, 512
    assert cos_t.shape == (tm, RET_D), "a projection row tile is one latent sequence"
    npb = tp // tm
    tpp = (n_cols // 4) // tn
    return pl.pallas_call(
        functools.partial(_qkvg_proj_kernel, npb, tpp, RET_D ** -0.5),
        grid=(t // tm, n_cols // tn),
        in_specs=[pl.BlockSpec((tm, d), lambda i, j: (i, 0)),
                  pl.BlockSpec((d, tn), lambda i, j: (0, j + col0 // tn)),
                  pl.BlockSpec((tm, RET_D), lambda i, j: (0, 0)),
                  pl.BlockSpec((tm, RET_D), lambda i, j: (0, 0))],
        out_specs=pl.BlockSpec((tm, tn), lambda i, j: (i, j)),
        out_shape=jax.ShapeDtypeStruct((t, n_cols), BF16),
        compiler_params=_params("arbitrary", "arbitrary"),
        name="qkvg_proj",
    )(h, w_in, cos_t, sin_t)


def _rope_tables(n_tok, width):
    rows = n_tok // GRID_W
    r_idx = np.repeat(np.arange(rows), GRID_W).astype(np.float64)
    c_idx = np.tile(np.arange(GRID_W), rows).astype(np.float64)
    dq = RET_D // 2
    inv = ROPE_BASE ** (-np.arange(dq // 2, dtype=np.float64) / (dq // 2))
    sign = np.concatenate([-np.ones(dq // 2), np.ones(dq // 2)])
    cos_h, sin_h = [], []
    for idx in (r_idx, c_idx):
        ang = idx[:, None] * inv
        cos_h.append(np.concatenate([np.cos(ang), np.cos(ang)], axis=1))
        sin_h.append(np.concatenate([np.sin(ang), np.sin(ang)], axis=1) * sign)
    cos_h, sin_h = np.concatenate(cos_h, axis=1), np.concatenate(sin_h, axis=1)
    reps = width // RET_D
    return (jnp.asarray(np.tile(cos_h, (1, reps)), F32), jnp.asarray(np.tile(sin_h, (1, reps)), F32))


def _conv_kernel(npb, tiles_per_seq, uc_ref, up_ref, un_ref, w_ref, b_ref, g_ref, bt_ref, o_ref, buf_ref, y_ref):
    r = pl.program_id(0)
    tm = uc_ref.shape[0]
    ncg = buf_ref.shape[0]
    t = lax.rem(jnp.maximum(r - npb, 0), tiles_per_seq)
    is_s = r >= npb
    has_prev = is_s & (t != 0)
    has_next = is_s & (t != tiles_per_seq - 1)
    for cg in range(ncg):
        cs = slice(cg * 128, (cg + 1) * 128)
        buf_ref[cg, 0:CONV_HALO, :] = jnp.where(has_prev, up_ref[:, cs], 0.0)
        buf_ref[cg, CONV_HALO:CONV_HALO + tm, :] = uc_ref[:, cs]
        buf_ref[cg, CONV_HALO + tm:, :] = jnp.where(has_next, un_ref[:, cs], 0.0)

    off = CONV_HALO - CONV_K // 2

    def body(cg, carry):
        acc = jnp.zeros((tm, 128), F32)
        for tap in range(CONV_K):
            acc = acc + buf_ref[cg, off + tap:off + tap + tm, :] * w_ref[cg, tap:tap + 1, :]
        y_ref[cg] = acc + b_ref[cg]
        return carry

    lax.fori_loop(0, ncg, body, 0)

    n_ch = ncg * 128
    tot = y_ref[0]
    for cg in range(1, ncg):
        tot = tot + y_ref[cg]
    mu = jnp.sum(tot, axis=1, keepdims=True) * (1.0 / n_ch)
    sq = jnp.zeros((tm, 128), F32)
    for cg in range(ncg):
        dv = y_ref[cg] - mu
        sq = sq + dv * dv
    var = jnp.sum(sq, axis=1, keepdims=True) * (1.0 / n_ch)
    rstd = lax.rsqrt(var + LN_EPS)
    for cg in range(ncg):
        cs = slice(cg * 128, (cg + 1) * 128)
        v = (y_ref[cg] - mu) * rstd * g_ref[cg] + bt_ref[cg]
        o_ref[:, cs] = (v * _sigmoid(v)).astype(o_ref.dtype)


def _conv_module(u, conv_w, conv_b, ln_g, ln_b, tp, dec_seq):
    t, c = u.shape
    tm = TM_ROW
    ncg = c // 128
    npb = tp // tm
    hb = tm // CONV_HALO
    n_halo_blocks = t // CONV_HALO
    w3 = jnp.zeros((32, c), F32).at[:CONV_K].set(conv_w).reshape(32, ncg, 128).transpose(1, 0, 2)
    vec = lambda a: a.reshape(ncg, 1, 128)
    return pl.pallas_call(
        functools.partial(_conv_kernel, npb, dec_seq // tm),
        grid=(t // tm,),
        in_specs=[pl.BlockSpec((tm, c), lambda r: (r, 0)),
                  pl.BlockSpec((CONV_HALO, c), lambda r: (jnp.maximum(r * hb - 1, 0), 0)),
                  pl.BlockSpec((CONV_HALO, c), lambda r: (jnp.minimum((r + 1) * hb, n_halo_blocks - 1), 0)),
                  pl.BlockSpec((ncg, 32, 128), lambda r: (0, 0, 0)),
                  pl.BlockSpec((ncg, 1, 128), lambda r: (0, 0, 0)),
                  pl.BlockSpec((ncg, 1, 128), lambda r: (0, 0, 0)),
                  pl.BlockSpec((ncg, 1, 128), lambda r: (0, 0, 0))],
        out_specs=pl.BlockSpec((tm, c), lambda r: (r, 0)),
        out_shape=jax.ShapeDtypeStruct((t, c), BF16),
        scratch_shapes=[pltpu.VMEM((ncg, tm + 2 * CONV_HALO, 128), F32),
                        pltpu.VMEM((ncg, tm, 128), F32)],
        compiler_params=_params("arbitrary"),
        name="conv_ln_swish",
    )(u, u, u, w3, vec(conv_b), vec(ln_g), vec(ln_b))


def _retention_kernel(nc, has_init, lg_ref, q_ref, k_ref, v_ref, g_ref, gn_ref, *rest):
    if has_init:
        s0f_ref, s0b_ref, r_ref, o_ref, sf_ref, sb_ref = rest
    else:
        s0f_ref = s0b_ref = None
        r_ref, sf_ref, sb_ref, o_ref = rest
    for hh in range(HEADS_PER_STEP):
        cols = slice(hh * RET_D, (hh + 1) * RET_D)
        _retention_head(nc, lg_ref, pl.program_id(1) * HEADS_PER_STEP + hh,
                        q_ref.at[:, cols], k_ref.at[:, cols], v_ref.at[:, cols], g_ref.at[:, cols],
                        gn_ref.at[:, cols], None if s0f_ref is None else s0f_ref.at[hh],
                        None if s0b_ref is None else s0b_ref.at[hh],
                        r_ref.at[:, cols], sf_ref.at[hh], sb_ref.at[hh], o_ref.at[hh])


def _retention_head(nc, lg_ref, hd, q_ref, k_ref, v_ref, g_ref, gn_ref, s0f_ref, s0b_ref, r_ref, sf_ref, sb_ref, o_ref):
    has_init = s0f_ref is not None
    lgf, lgb = lg_ref[0, hd], lg_ref[1, hd]
    c = CHUNK
    row = lax.broadcasted_iota(jnp.int32, (c, c), 0)
    col = lax.broadcasted_iota(jnp.int32, (c, c), 1)
    diff = (row - col).astype(F32)
    dec = (jnp.where(diff >= 0, jnp.exp(jnp.maximum(diff, 0.0) * lgf), 0.0)
           + jnp.where(diff <= 0, jnp.exp(jnp.maximum(-diff, 0.0) * lgb), 0.0))
    pos = lax.broadcasted_iota(jnp.int32, (c, 1), 0).astype(F32)
    xi_f = jnp.exp((pos + 1.0) * lgf)
    zeta_f = jnp.exp((c - 1.0 - pos) * lgf)
    xi_b = jnp.exp((c - pos) * lgb)
    zeta_b = jnp.exp(pos * lgb)
    gch_f = jnp.exp(jnp.full((1, RET_D), c, F32) * lgf)
    gch_b = jnp.exp(jnp.full((1, RET_D), c, F32) * lgb)

    if has_init:
        sf_ref[...] = s0f_ref[...]
        sb_ref[...] = s0b_ref[...]
    else:
        sf_ref[...] = jnp.zeros_like(sf_ref)
        sb_ref[...] = jnp.zeros_like(sb_ref)

    def chunk(ci):
        sl = slice(ci * c, (ci + 1) * c)
        return q_ref[sl, :], k_ref[sl, :], v_ref[sl, :], sl

    def state_update(s_ref, kc, vc, zeta, gch):
        kz = (kc.astype(F32) * zeta).T.astype(BF16)
        s_ref[...] = gch * s_ref[...] + jnp.dot(kz, vc, preferred_element_type=F32)

    for ci in range(nc):
        qc, kc, vc, sl = chunk(ci)
        s = lax.dot_general(qc, kc, (((1,), (1,)), ((), ())), preferred_element_type=F32)
        p = (s * dec).astype(BF16)
        o = jnp.dot(p, vc, preferred_element_type=F32)
        o = o + jnp.dot(qc, sf_ref[...].astype(BF16), preferred_element_type=F32) * xi_f
        o_ref[sl, :] = o
        state_update(sf_ref, kc, vc, zeta_f, gch_f)

    for ci in reversed(range(nc)):
        qc, kc, vc, sl = chunk(ci)
        o_ref[sl, :] = o_ref[sl, :] + jnp.dot(qc, sb_ref[...].astype(BF16), preferred_element_type=F32) * xi_b
        state_update(sb_ref, kc, vc, zeta_b, gch_b)

    y = _ln_rows(o_ref[...])
    r_ref[...] = (g_ref[...].astype(F32) * (y * gn_ref[...])).astype(r_ref.dtype)


def _retention(qkvg, lg, gn, n, row_blk0, n_seq, s0f=None, s0b=None):
    has_init = s0f is not None
    nh, dd, hps = RET_HEADS, RET_D, HEADS_PER_STEP
    wd = hps * dd
    col = lambda part: (lambda b, h: (b + row_blk0, part * (nh // hps) + h))
    st_spec = pl.BlockSpec((None, None, hps, dd, dd), lambda b, h: (b, 0, h, 0, 0))
    in_specs = [pl.BlockSpec(memory_space=pltpu.SMEM),
                pl.BlockSpec((n, wd), col(0)), pl.BlockSpec((n, wd), col(1)),
                pl.BlockSpec((n, wd), col(2)), pl.BlockSpec((n, wd), col(3)),
                pl.BlockSpec((1, wd), lambda b, h: (0, h))]
    args = [lg, qkvg, qkvg, qkvg, qkvg, gn]
    r_shape = jax.ShapeDtypeStruct((n_seq * n, nh * dd), BF16)
    r_spec = pl.BlockSpec((n, wd), lambda b, h: (b, h))
    st_scratch = pltpu.VMEM((hps, dd, dd), F32)
    if has_init:
        in_specs += [st_spec, st_spec]
        args += [s0f, s0b]
        out_specs, out_shape = r_spec, r_shape
        scratch = [pltpu.VMEM((hps, n, dd), F32), st_scratch, st_scratch]
    else:
        st_shape = jax.ShapeDtypeStruct((n_seq, 1, nh, dd, dd), F32)
        out_specs, out_shape = [r_spec, st_spec, st_spec], [r_shape, st_shape, st_shape]
        scratch = [pltpu.VMEM((hps, n, dd), F32)]
    return pl.pallas_call(
        functools.partial(_retention_kernel, n // CHUNK, has_init),
        grid=(n_seq, nh // hps),
        in_specs=in_specs, out_specs=out_specs, out_shape=out_shape,
        scratch_shapes=scratch,
        compiler_params=_params("arbitrary", "arbitrary"),
        name="retention_latent" if has_init else "retention_context",
    )(*args)


def _out_proj_kernel(npb, alpha, u_ref, r_ref, w1_ref, w2_ref, xp_ref, xs_ref, gt_ref, o_ref):
    i = pl.program_id(0)
    y = (jnp.dot(u_ref[...], w1_ref[...].astype(BF16), preferred_element_type=F32)
         + jnp.dot(r_ref[...], w2_ref[...].astype(BF16), preferred_element_type=F32))

    @pl.when(i < npb)
    def _():
        o_ref[...] = alpha * xp_ref[...] + gt_ref[...] * y

    @pl.when(i >= npb)
    def _():
        o_ref[...] = alpha * xs_ref[...] + gt_ref[...] * y


def _out_proj(u, r, w_out, xp, xs, gt, alpha, dec_seq):
    t, kh = u.shape
    d = w_out.shape[1]
    tm, tn = TM_PROJ, 512
    npb = xp.shape[0] // tm
    cidx = lambda i: jnp.where(i < npb, 0, 1 + (jnp.maximum(i - npb, 0) * tm) // dec_seq)
    return pl.pallas_call(
        functools.partial(_out_proj_kernel, npb, alpha),
        grid=(t // tm, d // tn),
        in_specs=[pl.BlockSpec((tm, kh), lambda i, j: (i, 0)),
                  pl.BlockSpec((tm, kh), lambda i, j: (i, 0)),
                  pl.BlockSpec((kh, tn), lambda i, j: (0, j)),
                  pl.BlockSpec((kh, tn), lambda i, j: (1, j)),
                  pl.BlockSpec((tm, tn), lambda i, j: (jnp.minimum(i, npb - 1), j)),
                  pl.BlockSpec((tm, tn), lambda i, j: (jnp.maximum(i - npb, 0), j)),
                  pl.BlockSpec((None, 1, tn), lambda i, j: (cidx(i), 0, j))],
        out_specs=pl.BlockSpec((tm, tn), lambda i, j: (i, j)),
        out_shape=jax.ShapeDtypeStruct((t, d), F32),
        compiler_params=_params("arbitrary", "arbitrary"),
        name="out_proj_residual",
    )(u, r, w_out, w_out, xp, xs, gt)


def _pack_bf16_pairs(x):
    n = x.shape[1] // 2
    lo = lax.bitcast_convert_type(x[:, :n].astype(BF16).astype(F32), jnp.uint32)
    hi = lax.bitcast_convert_type(x[:, n:].astype(BF16).astype(F32), jnp.uint32)
    return hi | (lo >> 16)


def _unpack_bf16_pairs(p):
    lo = lax.bitcast_convert_type(p << 16, F32)
    hi = lax.bitcast_convert_type(p & jnp.uint32(0xFFFF0000), F32)
    return jnp.concatenate([lo, hi], axis=1)


def _store_as_slabs(ref, rows):
    m, n = rows.shape
    slab = n // 128
    for s in range(slab):
        ref[pl.ds(s, m, stride=slab), :] = rows[:, s * 128:(s + 1) * 128]


def _load_from_slabs(ref, m):
    slab = ref.shape[0] // m
    return jnp.concatenate([ref[pl.ds(s, m, stride=slab), :] for s in range(slab)], axis=1)


def _slab_copy(src, dst, src_row, dst_row, slab, sem):
    return pltpu.make_async_copy(src.at[pl.ds(pl.multiple_of(src_row * slab, slab), slab), :],
                                 dst.at[pl.ds(pl.multiple_of(dst_row * slab, slab), slab), :], sem)


def _post_mix_kernel(v_ref, g1_ref, b1_ref, sc_ref, sh_ref, wr_ref, br_ref, x1_ref, h2_ref, rt_ref, cnt_ref, tab_ref):
    @pl.when(pl.program_id(0) == 0)
    def _():
        cnt_ref[...] = jnp.zeros_like(cnt_ref)

    x1 = _ln_rows(v_ref[...]) * g1_ref[...] + b1_ref[...]
    x1_ref[...] = x1
    h2 = (_ln_rows(x1) * (1.0 + sc_ref[...]) + sh_ref[...]).astype(BF16)
    h2_ref[...] = h2
    logits = jnp.dot(h2, wr_ref[...], preferred_element_type=F32) + br_ref[...]
    lane = lax.broadcasted_iota(jnp.int32, logits.shape, 1)
    big = jnp.int32(ROUTER_LANES)
    neg = jnp.float32(-jnp.inf)

    def first_lane_of_max(vals):
        m = jnp.max(vals, axis=1, keepdims=True)
        return m, jnp.min(jnp.where(vals == m, lane, big), axis=1, keepdims=True)

    lgt = jnp.where(lane < N_GROUPS, logits, neg)
    eg = jnp.exp(lgt - jnp.max(lgt, axis=1, keepdims=True))
    pg = eg / jnp.sum(eg, axis=1, keepdims=True)
    grp_prob, grp = first_lane_of_max(jnp.where(lane < N_GROUPS, pg, -1.0))
    lo = N_GROUPS + grp * EXPERTS_PER_GROUP
    in_grp = (lane >= lo) & (lane < lo + EXPERTS_PER_GROUP)
    let = jnp.where(in_grp, logits, neg)
    ee = jnp.exp(let - jnp.max(let, axis=1, keepdims=True))
    pe = jnp.where(in_grp, ee / jnp.sum(ee, axis=1, keepdims=True), -1.0)
    p1, l1 = first_lane_of_max(pe)
    p2, l2 = first_lane_of_max(jnp.where(lane == l1, -1.0, pe))
    den = p1 + p2
    gate1, gate2 = grp_prob * p1 / den, grp_prob * p2 / den
    e1, e2 = l1 - N_GROUPS, l2 - N_GROUPS
    tm = logits.shape[0]
    hot1 = (lane == e1).astype(F32)
    hot2 = (lane == e2).astype(F32)
    earlier = (lax.broadcasted_iota(jnp.int32, (tm, tm), 0) > lax.broadcasted_iota(jnp.int32, (tm, tm), 1)).astype(BF16)
    before1 = jnp.dot(earlier, hot1.astype(BF16), preferred_element_type=F32)
    n1 = jnp.sum(hot1, axis=0, keepdims=True)
    before2 = jnp.dot(earlier, hot2.astype(BF16), preferred_element_type=F32) + n1
    cnt = cnt_ref[...]
    slot1 = jnp.sum((before1 + cnt) * hot1, axis=1, keepdims=True)
    slot2 = jnp.sum((before2 + cnt) * hot2, axis=1, keepdims=True)
    n_blk = n1 + jnp.sum(hot2, axis=0, keepdims=True)
    lower = (lax.broadcasted_iota(jnp.int32, (ROUTER_LANES, ROUTER_LANES), 0)
             < lax.broadcasted_iota(jnp.int32, (ROUTER_LANES, ROUTER_LANES), 1)).astype(BF16)
    start = jnp.dot(jnp.broadcast_to(n_blk, (8, ROUTER_LANES)).astype(BF16), lower, preferred_element_type=F32)[0:1]
    local1 = jnp.sum((before1 + start) * hot1, axis=1, keepdims=True)
    local2 = jnp.sum((before2 + start) * hot2, axis=1, keepdims=True)
    cnt_ref[...] = cnt + n_blk
    vals = (e1.astype(F32), e2.astype(F32), gate1, gate2, slot1, slot2, local1, local2)
    out = jnp.zeros_like(logits)
    for k, val in enumerate(vals):
        out = jnp.where(lane == k, val, out)
    rt_ref[...] = out
    sub = lax.broadcasted_iota(jnp.int32, (8, ROUTER_LANES), 0)
    tab_ref[...] = jnp.where(sub == 0, n_blk, jnp.where(sub == 1, start, jnp.where(sub == 2, cnt, 0.0)))


def _post_mix(v, ln_g, ln_b, sc, sh, w_router, b_router, tp, dec_seq):
    t, d = v.shape
    tm = TM_ROW
    npb = tp // tm
    cidx = lambda i: jnp.where(i < npb, 0, 1 + (jnp.maximum(i - npb, 0) * tm) // dec_seq)
    row = pl.BlockSpec((tm, d), lambda i: (i, 0))
    vec = pl.BlockSpec((1, d), lambda i: (0, 0))
    cvec = pl.BlockSpec((None, 1, d), lambda i: (cidx(i), 0, 0))
    return pl.pallas_call(
        _post_mix_kernel,
        grid=(t // tm,),
        in_specs=[row, vec, vec, cvec, cvec,
                  pl.BlockSpec((d, ROUTER_LANES), lambda i: (0, 0)),
                  pl.BlockSpec((1, ROUTER_LANES), lambda i: (0, 0))],
        out_specs=[row, row,
                   pl.BlockSpec((tm, ROUTER_LANES), lambda i: (i, 0)),
                   pl.BlockSpec((1, ROUTER_LANES), lambda i: (0, 0)),
                   pl.BlockSpec((None, 8, ROUTER_LANES), lambda i: (i, 0, 0))],
        out_shape=[jax.ShapeDtypeStruct((t, d), F32), jax.ShapeDtypeStruct((t, d), BF16),
                   jax.ShapeDtypeStruct((t, ROUTER_LANES), F32), jax.ShapeDtypeStruct((1, ROUTER_LANES), F32),
                   jax.ShapeDtypeStruct((t // tm, 8, ROUTER_LANES), F32)],
        compiler_params=_params("arbitrary"),
        name="ln_ln_router",
    )(v, ln_g.reshape(1, d), ln_b.reshape(1, d), sc, sh, w_router, b_router)


def _on_parity(blk, fn):
    for par in range(2):
        pl.when(lax.rem(blk, 2) == par)(functools.partial(fn, blk, par))


def _dispatch_kernel(run_n_ref, run_src_ref, run_dst_ref, pad_row_ref, pad_n_ref, nact_ref,
                     h_ref, rt_ref, o_hbm, stage_ref, zero_ref, sems):
    tb = pl.program_id(0)
    n_tb = pl.num_programs(0)
    tm, d = h_ref.shape
    slab = d // 2 // 128
    n_rows_out = o_hbm.shape[0] // slab
    fill_sem = sems.at[2]

    def pieces(wait, src_ref, src_row, dst_row, n, max_size, sem):
        done = jnp.int32(0)
        size = max_size
        while size >= 1:
            take = (n & size) != 0
            src0 = 0 if src_row is None else pl.multiple_of((src_row + done) * slab, slab)
            cp = pltpu.make_async_copy(
                src_ref.at[pl.ds(src0, size * slab), :],
                o_hbm.at[pl.ds(pl.multiple_of((dst_row + done) * slab, slab), size * slab), :], sem)
            pl.when(take)(cp.wait if wait else cp.start)
            done = done + jnp.where(take, size, 0)
            size //= 2

    def runs(wait, blk, par):
        def body(e, carry):
            k = blk * N_EXPERTS + e
            pieces(wait, stage_ref.at[par], run_src_ref[k], run_dst_ref[k], run_n_ref[k], tm, sems.at[par])
            return carry
        lax.fori_loop(0, N_EXPERTS, body, 0)

    def fill(wait):
        def tail(e, carry):
            pieces(wait, zero_ref, None, pad_row_ref[e], pad_n_ref[e], tm // 2, fill_sem)
            return carry

        def spare(b, carry):
            pieces(wait, zero_ref, None, b * tm, jnp.int32(tm), tm, fill_sem)
            return carry

        lax.fori_loop(0, N_EXPERTS, tail, 0)
        lax.fori_loop(nact_ref[0], n_rows_out // tm, spare, 0)

    @pl.when(tb == 0)
    def _():
        zero_ref[...] = jnp.zeros_like(zero_ref)
        fill(False)

    @pl.when(tb >= 2)
    def _():
        _on_parity(tb, lambda blk, par: runs(True, blk - 2, par))

    rt = rt_ref[...]
    local1 = rt[:, 6:7].astype(jnp.int32)
    local2 = rt[:, 7:8].astype(jnp.int32)
    place = lax.broadcasted_iota(jnp.int32, (tm, TOP_K * tm), 1)
    onehot = ((place == local1) | (place == local2)).astype(F32)
    xp = jnp.dot(onehot.T.astype(BF16), h_ref[...], preferred_element_type=F32)
    packed = (lax.bitcast_convert_type(xp[:, d // 2:], jnp.uint32)
              | (lax.bitcast_convert_type(xp[:, :d // 2], jnp.uint32) >> 16))
    _store_as_slabs(stage_ref.at[lax.rem(tb, 2)], packed)
    _on_parity(tb, functools.partial(runs, False))

    @pl.when(tb == n_tb - 1)
    def _():
        _on_parity(tb - 1, functools.partial(runs, True))
        _on_parity(tb, functools.partial(runs, True))
        fill(True)


def _dispatch_rows(h2, route, tables, n_blocks):
    t, d = h2.shape
    tm = TM_ROW
    slab = d // 2 // 128
    assert tm == TM_MOE and t // tm >= 2
    return pl.pallas_call(
        _dispatch_kernel,
        grid_spec=pltpu.PrefetchScalarGridSpec(
            num_scalar_prefetch=len(tables),
            grid=(t // tm,),
            in_specs=[pl.BlockSpec((tm, d), lambda i, *_: (i, 0)),
                      pl.BlockSpec((tm, ROUTER_LANES), lambda i, *_: (i, 0))],
            out_specs=pl.BlockSpec(memory_space=pl.ANY),
            scratch_shapes=[pltpu.VMEM((2, TOP_K * tm * slab, 128), jnp.uint32),
                            pltpu.VMEM((tm * slab, 128), jnp.uint32), pltpu.SemaphoreType.DMA((3,))]),
        out_shape=jax.ShapeDtypeStruct((n_blocks * TM_MOE * slab, 128), jnp.uint32),
        compiler_params=_params("arbitrary"),
        name="moe_dispatch",
    )(*tables, h2, route)


def _expert_up_kernel(se_ref, sj_ref, sb_ref, oj_ref, ob_ref, fst_ref, gp_ref, nxt_ref, ne_ref, nj_ref, ns_ref,
                      x_ref, wg_hbm, wu_hbm, o_ref, wg_st, wu_st, sems):
    live = pl.program_id(0) < ns_ref[0]
    _stream_group_weights(se_ref, sj_ref, fst_ref, gp_ref, nxt_ref, ne_ref, nj_ref, live,
                          ((wg_hbm, wg_st), (wu_hbm, wu_st)), sems)

    @pl.when(live)
    def _():
        x = _unpack_bf16_pairs(_load_from_slabs(x_ref, TM_MOE))
        slot = gp_ref[pl.program_id(0)]
        a = jnp.dot(x, wg_st[slot], preferred_element_type=F32)
        u = jnp.dot(x, wu_st[slot], preferred_element_type=F32)
        o_ref[...] = ((a * _sigmoid(a)) * u).astype(o_ref.dtype)

    @pl.when(jnp.logical_not(live))
    def _():
        o_ref[...] = jnp.zeros_like(o_ref)


def _stream_group_weights(se_ref, sj_ref, fst_ref, gp_ref, nxt_ref, ne_ref, nj_ref, live, weights, sems):
    s = pl.program_id(0)

    def copies(e, j, slot):
        out = []
        for k, (w_hbm, w_st) in enumerate(weights):
            tn = w_st.shape[2]
            src = w_hbm.at[e, :, pl.ds(pl.multiple_of(j * tn, tn), tn)]
            out.append(pltpu.make_async_copy(src, w_st.at[slot], sems.at[k, slot]))
        return out

    def first_step(slot):
        cur = copies(se_ref[s], sj_ref[s], slot)

        @pl.when(s == 0)
        def _():
            for cp in cur:
                cp.start(priority=1)

        @pl.when(nxt_ref[s] == 1)
        def _():
            for cp in copies(ne_ref[s], nj_ref[s], 1 - slot):
                cp.start(priority=1)

        for cp in cur:
            cp.wait()

    for slot in range(2):
        pl.when(live & (fst_ref[s] == 1) & (gp_ref[s] == slot))(functools.partial(first_step, slot))


_N_SCHED = 11


def _sched_map(fn):
    return lambda s, *refs: fn(s, *refs[:_N_SCHED])


def _expert_up(xs, w_gate, w_up, sched, n_steps):
    d, de = w_gate.shape[1], w_gate.shape[2]
    tm, tf = TM_MOE, TF_MOE
    slab = d // 2 // 128
    r = xs.shape[0] // slab
    return pl.pallas_call(
        _expert_up_kernel,
        grid_spec=pltpu.PrefetchScalarGridSpec(
            num_scalar_prefetch=_N_SCHED,
            grid=(n_steps,),
            in_specs=[pl.BlockSpec((tm * slab, 128), _sched_map(lambda s, se, sj, sb, *_: (sb[s], 0))),
                      pl.BlockSpec(memory_space=pl.ANY), pl.BlockSpec(memory_space=pl.ANY)],
            out_specs=pl.BlockSpec((tm, tf), _sched_map(lambda s, se, sj, sb, oj, ob, *_: (ob[s], oj[s]))),
            scratch_shapes=[pltpu.VMEM((2, d, tf), F32), pltpu.VMEM((2, d, tf), F32),
                            pltpu.SemaphoreType.DMA((2, 2))]),
        out_shape=jax.ShapeDtypeStruct((r, de), BF16),
        compiler_params=_params("arbitrary"),
        name="moe_gate_up",
    )(*sched, xs, w_gate, w_up)


def _expert_down_kernel(se_ref, sj_ref, sb_ref, oj_ref, ob_ref, fst_ref, gp_ref, nxt_ref, ne_ref, nj_ref, ns_ref,
                        h_ref, wd_hbm, o_ref, wd_st, sems):
    live = pl.program_id(0) < ns_ref[0]
    _stream_group_weights(se_ref, sj_ref, fst_ref, gp_ref, nxt_ref, ne_ref, nj_ref, live, ((wd_hbm, wd_st),), sems)

    @pl.when(live)
    def _():
        y = jnp.dot(h_ref[...].astype(F32), wd_st[gp_ref[pl.program_id(0)]], preferred_element_type=F32)
        _store_as_slabs(o_ref, _pack_bf16_pairs(y))

    @pl.when(jnp.logical_not(live))
    def _():
        o_ref[...] = jnp.zeros_like(o_ref)


def _expert_down(hid, w_down, sched, n_steps):
    r, de = hid.shape
    d = w_down.shape[2]
    tm = TM_MOE
    slab = d // 2 // 128
    return pl.pallas_call(
        _expert_down_kernel,
        grid_spec=pltpu.PrefetchScalarGridSpec(
            num_scalar_prefetch=_N_SCHED,
            grid=(n_steps,),
            in_specs=[pl.BlockSpec((tm, de), _sched_map(lambda s, se, sj, sb, *_: (sb[s], 0))),
                      pl.BlockSpec(memory_space=pl.ANY)],
            out_specs=pl.BlockSpec((tm * slab, 128), _sched_map(lambda s, se, sj, sb, oj, ob, *_: (ob[s], 0))),
            scratch_shapes=[pltpu.VMEM((2, de, d), F32), pltpu.SemaphoreType.DMA((1, 2))]),
        out_shape=jax.ShapeDtypeStruct((r * slab, 128), jnp.uint32),
        compiler_params=_params("arbitrary"),
        name="moe_down",
    )(*sched, hid, w_down)


def _combine_kernel(npb, alpha, pos_ref, y_hbm, x1_ref, rt_ref, gt_ref, g_ref, b_ref,
                    op_ref, os_ref, buf_ref, sems):
    i = pl.program_id(0)
    n = pl.num_programs(0)
    tm = x1_ref.shape[0]
    slab = buf_ref.shape[2] // tm

    def rows(wait, blk, par):
        def body(g, carry):
            for u in range(ROW_UNROLL):
                r = g * ROW_UNROLL + u
                for k in range(TOP_K):
                    cp = _slab_copy(y_hbm, buf_ref.at[par, k], pos_ref[(blk * tm + r) * TOP_K + k], r, slab,
                                    sems.at[par])
                    cp.wait() if wait else cp.start(priority=k % 2)
            return carry
        lax.fori_loop(0, tm // ROW_UNROLL, body, 0)

    @pl.when(i == 0)
    def _():
        rows(False, 0, 0)

    @pl.when(i + 1 < n)
    def _():
        _on_parity(i + 1, functools.partial(rows, False))

    _on_parity(i, functools.partial(rows, True))
    rt = rt_ref[...]
    par = lax.rem(i, 2)
    expert_rows = lambda k: _unpack_bf16_pairs(_load_from_slabs(buf_ref.at[par, k], tm))
    f = rt[:, 2:3] * expert_rows(0) + rt[:, 3:4] * expert_rows(1)
    out = _ln_rows(alpha * x1_ref[...] + gt_ref[...] * f) * g_ref[...] + b_ref[...]

    @pl.when(i < npb)
    def _():
        op_ref[...] = out

    @pl.when(i >= npb)
    def _():
        os_ref[...] = out


def _combine(yb, pos, x1, route, gt, ln_g, ln_b, alpha, tp, dec_seq):
    t, d = x1.shape
    tm = TM_ROW
    npb = tp // tm
    cidx = lambda i: jnp.where(i < npb, 0, 1 + (jnp.maximum(i - npb, 0) * tm) // dec_seq)
    vec = pl.BlockSpec((1, d), lambda i, *_: (0, 0))
    return pl.pallas_call(
        functools.partial(_combine_kernel, npb, alpha),
        grid_spec=pltpu.PrefetchScalarGridSpec(
            num_scalar_prefetch=1,
            grid=(t // tm,),
            in_specs=[pl.BlockSpec(memory_space=pl.ANY),
                      pl.BlockSpec((tm, d), lambda i, *_: (i, 0)),
                      pl.BlockSpec((tm, ROUTER_LANES), lambda i, *_: (i, 0)),
                      pl.BlockSpec((None, 1, d), lambda i, *_: (cidx(i), 0, 0)),
                      vec, vec],
            out_specs=[pl.BlockSpec((tm, d), lambda i, *_: (jnp.minimum(i, npb - 1), 0)),
                       pl.BlockSpec((tm, d), lambda i, *_: (jnp.maximum(i - npb, 0), 0))],
            scratch_shapes=[pltpu.VMEM((2, TOP_K, tm * (d // 2 // 128), 128), jnp.uint32),
                            pltpu.SemaphoreType.DMA((2,))]),
        out_shape=[jax.ShapeDtypeStruct((tp, d), F32), jax.ShapeDtypeStruct((t - tp, d), F32)],
        compiler_params=_params("arbitrary"),
        name="moe_combine_ln",
    )(pos, yb, x1, route, gt, ln_g.reshape(1, d), ln_b.reshape(1, d))


def _dispatch_plan(route, counts_f, block_tab, n_tok):
    tm = TM_MOE
    n_blocks = n_tok * TOP_K // tm + N_EXPERTS
    i32 = jnp.int32
    eid = route[:, 0:TOP_K].astype(i32).reshape(-1)
    slot = route[:, 4:4 + TOP_K].astype(i32).reshape(-1)
    counts = counts_f[0, :N_EXPERTS].astype(i32)
    nb = (counts + tm - 1) // tm

    def cumsum(x):
        idx = jnp.arange(x.shape[0], dtype=i32)
        return jnp.sum(jnp.where(idx[None, :] <= idx[:, None], x[None, :], 0), axis=1)

    nb_end = cumsum(nb)
    bs = nb_end - nb
    n_act = nb_end[-1]
    experts = jnp.arange(N_EXPERTS, dtype=i32)
    look = lambda table, e: jnp.sum(jnp.where(e[:, None] == experts[None, :], table[None, :], 0), axis=1)
    tab = block_tab[:, :3, :N_EXPERTS].astype(i32)
    run_dst = (bs * tm)[None, :] + tab[:, 2]
    dispatch_tables = (tab[:, 0].reshape(-1), tab[:, 1].reshape(-1), run_dst.reshape(-1),
                       bs * tm + counts, nb * tm - counts, n_act.reshape(1))

    def schedule(n_inner):
        n_steps = n_inner * n_blocks
        n_live = n_inner * n_act

        def decode(step):
            s = jnp.minimum(step, n_live - 1)
            e = jnp.minimum(jnp.sum((s[:, None] >= n_inner * nb_end[None, :]).astype(i32), axis=1), N_EXPERTS - 1)
            nbe = jnp.maximum(look(nb, e), 1)
            loc = s - n_inner * look(bs, e)
            return e, loc // nbe, loc % nbe, nbe, look(bs, e)

        step = jnp.arange(n_steps, dtype=i32)
        e, sj, bi, nbe, bse = decode(step)
        live = step < n_live
        nxt_step = jnp.minimum(step, n_live - 1) + nbe - bi
        ne, nj, _, _, _ = decode(nxt_step)
        fst = (live & (bi == 0)).astype(i32)
        gp = (cumsum(fst) - 1) % 2
        nxt = (live & (bi == 0) & (nxt_step < n_live)).astype(i32)
        spare = jnp.maximum(step - n_live, 0)
        oj = jnp.where(live, sj, spare % n_inner)
        ob = jnp.where(live, bse + bi, n_act + spare // n_inner)
        sched = (e, sj, bse + bi, oj, ob, fst, gp, nxt, ne, nj, n_live.reshape(1))
        return tuple(a.astype(i32) for a in sched), n_steps

    pos = look(bs * tm, eid) + slot
    return pos, dispatch_tables, n_blocks, schedule


def kernel(x_prompt, x_sample, state_ret_fwd, state_ret_bwd, c, c_ctx, w_mod, b_mod, w_in, conv_w, conv_b, conv_ln_g, conv_ln_b, ret_decay_fwd, ret_decay_bwd, ret_gn_g, w_out, ln1_g, ln1_b, w_grp, b_grp, w_exp, b_exp, w_gate, w_up, w_down, ln2_g, ln2_b):
    depth = w_mod.shape[0]
    assert depth == 1, "single-layer step"
    bp, sp, d = x_prompt.shape
    bs_, ss, _ = x_sample.shape
    tp, ts = bp * sp, bs_ * ss
    t = tp + ts
    conv_width = conv_w.shape[2]
    ret_width = ret_gn_g.shape[1]
    assert ret_width == RET_HEADS * RET_D and sp % CHUNK == 0 and ss % CHUNK == 0
    alpha = (2.0 * depth) ** 0.25

    xp = x_prompt.reshape(tp, d)
    xs = x_sample.reshape(ts, d)
    cond8 = jnp.zeros((8, d), F32).at[0].set(c_ctx).at[1:1 + bs_].set(c)
    m = _modulation(cond8, w_mod[0], b_mod[0])
    sh1, sc1, gt1, sh2, sc2, gt2 = [m[:, k * d:(k + 1) * d].reshape(8, 1, d) for k in range(6)]

    h = _ln_modulate(xp, xs, sc1, sh1, ss)
    u_glu = _glu_proj(h, w_in[0], conv_width)
    cos_t, sin_t = _rope_tables(ss, RET_D)
    qkvg = _qkvg_proj(h, w_in[0], cos_t, sin_t, 2 * conv_width, 4 * ret_width, tp)

    u = _conv_module(u_glu, conv_w[0], conv_b[0], conv_ln_g[0], conv_ln_b[0], tp, ss)

    lg = jnp.stack([jax.nn.log_sigmoid(ret_decay_fwd[0].astype(F32)),
                    jax.nn.log_sigmoid(ret_decay_bwd[0].astype(F32))])
    gn = ret_gn_g[0].reshape(1, ret_width)
    r_p, new_f, new_b = _retention(qkvg, lg, gn, sp, 0, bp)
    r_s = _retention(qkvg, lg, gn, ss, tp // ss, bs_, state_ret_fwd, state_ret_bwd)
    r = jnp.concatenate([r_p, r_s], axis=0)

    v = _out_proj(u, r, w_out[0], xp, xs, gt1, alpha, ss)

    w_router = jnp.zeros((d, ROUTER_LANES), F32).at[:, :N_GROUPS].set(w_grp[0]).at[:, N_GROUPS:N_GROUPS + N_EXPERTS].set(w_exp[0]).astype(BF16)
    b_router = jnp.zeros((1, ROUTER_LANES), F32).at[0, :N_GROUPS].set(b_grp[0]).at[0, N_GROUPS:N_GROUPS + N_EXPERTS].set(b_exp[0])
    x1, h2, route, counts, block_tab = _post_mix(v, ln1_g[0], ln1_b[0], sc2, sh2, w_router, b_router, tp, ss)

    pos, dispatch_tables, n_blocks, schedule = _dispatch_plan(route, counts, block_tab, t)
    xg = _dispatch_rows(h2, route, dispatch_tables, n_blocks)
    sched_up, n_up = schedule(w_gate.shape[3] // TF_MOE)
    hid = _expert_up(xg, w_gate[0], w_up[0], sched_up, n_up)
    sched_dn, n_dn = schedule(1)
    yb = _expert_down(hid, w_down[0], sched_dn, n_dn)

    out_p, out_s = _combine(yb, pos, x1, route, gt2, ln2_g[0], ln2_b[0], alpha, tp, ss)
    return (out_p.reshape(bp, sp, d), out_s.reshape(bs_, ss, d), new_f, new_b)
```

```python
import functools

import numpy as np
import jax
import jax.numpy as jnp
from jax import lax
from jax.experimental import pallas as pl
from jax.experimental.pallas import tpu as pltpu

F32 = jnp.float32
BF16 = jnp.bfloat16

LN_EPS = 1e-5
CONV_K = 31
CONV_HALO = 16
RET_HEADS = 8
RET_D = 256
HEADS_PER_STEP = 4
CHUNK = 128
GRID_W = 64
ROPE_BASE = 10000.0
N_GROUPS = 4
EXPERTS_PER_GROUP = 8
N_EXPERTS = N_GROUPS * EXPERTS_PER_GROUP
TOP_K = 2
ROUTER_LANES = 128
VMEM_LIMIT = 56 * 1024 * 1024
TM_PROJ = 1024
TM_ROW = 256
TM_MOE = 256
TF_MOE = 512
ROW_UNROLL = 8


def _params(*sem):
    return pltpu.CompilerParams(dimension_semantics=tuple(sem), vmem_limit_bytes=VMEM_LIMIT)


def _sigmoid(x):
    return 1.0 / (1.0 + jnp.exp(-x))


def _ln_rows(x):
    mu = jnp.mean(x, axis=-1, keepdims=True)
    xc = x - mu
    var = jnp.mean(xc * xc, axis=-1, keepdims=True)
    return xc * lax.rsqrt(var + LN_EPS)


def _mod_kernel(c_ref, w_ref, b_ref, o_ref):
    c = c_ref[...]
    s = c * _sigmoid(c)
    o_ref[...] = jnp.dot(s.astype(BF16), w_ref[...].astype(BF16), preferred_element_type=F32) + b_ref[...]


def _modulation(cond8, w_mod, b_mod):
    d, n = w_mod.shape
    tn = 512
    return pl.pallas_call(
        _mod_kernel,
        grid=(n // tn,),
        in_specs=[pl.BlockSpec((8, d), lambda j: (0, 0)),
                  pl.BlockSpec((d, tn), lambda j: (0, j)),
                  pl.BlockSpec((1, tn), lambda j: (0, j))],
        out_specs=pl.BlockSpec((8, tn), lambda j: (0, j)),
        out_shape=jax.ShapeDtypeStruct((8, n), F32),
        compiler_params=_params("arbitrary"),
        name="modulation",
    )(cond8, w_mod, b_mod.reshape(1, n))


def _ln_mod_kernel(npb, xp_ref, xs_ref, sc_ref, sh_ref, o_ref):
    def body(x_ref):
        y = _ln_rows(x_ref[...])
        o_ref[...] = (y * (1.0 + sc_ref[...]) + sh_ref[...]).astype(o_ref.dtype)

    i = pl.program_id(0)
    pl.when(i < npb)(lambda: body(xp_ref))
    pl.when(i >= npb)(lambda: body(xs_ref))


def _ln_modulate(xp, xs, sc, sh, dec_seq):
    (tp, d), ts = xp.shape, xs.shape[0]
    tm = 512
    npb, nsb = tp // tm, ts // tm
    cidx = lambda i: jnp.where(i < npb, 0, 1 + (jnp.maximum(i - npb, 0) * tm) // dec_seq)
    return pl.pallas_call(
        functools.partial(_ln_mod_kernel, npb),
        grid=(npb + nsb,),
        in_specs=[pl.BlockSpec((tm, d), lambda i: (jnp.minimum(i, npb - 1), 0)),
                  pl.BlockSpec((tm, d), lambda i: (jnp.maximum(i - npb, 0), 0)),
                  pl.BlockSpec((None, 1, d), lambda i: (cidx(i), 0, 0)),
                  pl.BlockSpec((None, 1, d), lambda i: (cidx(i), 0, 0))],
        out_specs=pl.BlockSpec((tm, d), lambda i: (i, 0)),
        out_shape=jax.ShapeDtypeStruct((tp + ts, d), BF16),
        compiler_params=_params("arbitrary"),
        name="ln_modulate",
    )(xp, xs, sc, sh)


def _glu_proj_kernel(h_ref, wv_ref, wg_ref, o_ref):
    h = h_ref[...]
    a = jnp.dot(h, wv_ref[...].astype(BF16), preferred_element_type=F32)
    g = jnp.dot(h, wg_ref[...].astype(BF16), preferred_element_type=F32)
    o_ref[...] = a * _sigmoid(g)


def _glu_proj(h, w_in, conv_width):
    t, d = h.shape
    tm, tn = TM_PROJ, 256
    goff = conv_width // tn
    return pl.pallas_call(
        _glu_proj_kernel,
        grid=(t // tm, conv_width // tn),
        in_specs=[pl.BlockSpec((tm, d), lambda i, j: (i, 0)),
                  pl.BlockSpec((d, tn), lambda i, j: (0, j)),
                  pl.BlockSpec((d, tn), lambda i, j: (0, j + goff))],
        out_specs=pl.BlockSpec((tm, tn), lambda i, j: (i, j)),
        out_shape=jax.ShapeDtypeStruct((t, conv_width), F32),
        compiler_params=_params("arbitrary", "arbitrary"),
        name="glu_proj",
    )(h, w_in, w_in)


def _qkvg_proj_kernel(npb, tiles_per_part, k_scale, h_ref, w_ref, cos_ref, sin_ref, o_ref):
    i, j = pl.program_id(0), pl.program_id(1)
    z = jnp.dot(h_ref[...], w_ref[...].astype(BF16), preferred_element_type=F32)
    tpp = tiles_per_part

    @pl.when(j >= 3 * tpp)
    def _():
        o_ref[...] = (z * _sigmoid(z)).astype(o_ref.dtype)

    @pl.when((j >= 2 * tpp) & (j < 3 * tpp))
    def _():
        o_ref[...] = z.astype(o_ref.dtype)

    @pl.when(j < 2 * tpp)
    def _():
        zz = z * jnp.where(j >= tpp, k_scale, 1.0).astype(F32)

        @pl.when(i < npb)
        def _():
            o_ref[...] = zz.astype(o_ref.dtype)

        @pl.when(i >= npb)
        def _():
            for s in range(zz.shape[1] // 128):
                cs = slice(s * 128, (s + 1) * 128)
                ts_ = slice((s * 128) % RET_D, (s * 128) % RET_D + 128)
                zs = zz[:, cs]
                o_ref[:, cs] = (zs * cos_ref[:, ts_] + pltpu.roll(zs, 64, axis=1) * sin_ref[:, ts_]).astype(o_ref.dtype)


def _qkvg_proj(h, w_in, cos_t, sin_t, col0, n_cols, tp):
    t, d = h.shape
    tm, tn = TM_PROJ, 512
    assert cos_t.shape == (tm, RET_D), "a projection row tile is one latent sequence"
    npb = tp // tm
    tpp = (n_cols // 4) // tn
    return pl.pallas_call(
        functools.partial(_qkvg_proj_kernel, npb, tpp, RET_D ** -0.5),
        grid=(t // tm, n_cols // tn),
        in_specs=[pl.BlockSpec((tm, d), lambda i, j: (i, 0)),
                  pl.BlockSpec((d, tn), lambda i, j: (0, j + col0 // tn)),
                  pl.BlockSpec((tm, RET_D), lambda i, j: (0, 0)),
                  pl.BlockSpec((tm, RET_D), lambda i, j: (0, 0))],
        out_specs=pl.BlockSpec((tm, tn), lambda i, j: (i, j)),
        out_shape=jax.ShapeDtypeStruct((t, n_cols), BF16),
        compiler_params=_params("arbitrary", "arbitrary"),
        name="qkvg_proj",
    )(h, w_in, cos_t, sin_t)


def _rope_tables(n_tok, width):
    rows = n_tok // GRID_W
    r_idx = np.repeat(np.arange(rows), GRID_W).astype(np.float64)
    c_idx = np.tile(np.arange(GRID_W), rows).astype(np.float64)
    dq = RET_D // 2
    inv = ROPE_BASE ** (-np.arange(dq // 2, dtype=np.float64) / (dq // 2))
    sign = np.concatenate([-np.ones(dq // 2), np.ones(dq // 2)])
    cos_h, sin_h = [], []
    for idx in (r_idx, c_idx):
        ang = idx[:, None] * inv
        cos_h.append(np.concatenate([np.cos(ang), np.cos(ang)], axis=1))
        sin_h.append(np.concatenate([np.sin(ang), np.sin(ang)], axis=1) * sign)
    cos_h, sin_h = np.concatenate(cos_h, axis=1), np.concatenate(sin_h, axis=1)
    reps = width // RET_D
    return (jnp.asarray(np.tile(cos_h, (1, reps)), F32), jnp.asarray(np.tile(sin_h, (1, reps)), F32))


def _conv_kernel(npb, tiles_per_seq, uc_ref, up_ref, un_ref, w_ref, b_ref, g_ref, bt_ref, o_ref, buf_ref, y_ref):
    r = pl.program_id(0)
    tm = uc_ref.shape[0]
    ncg = buf_ref.shape[0]
    t = lax.rem(jnp.maximum(r - npb, 0), tiles_per_seq)
    is_s = r >= npb
    has_prev = is_s & (t != 0)
    has_next = is_s & (t != tiles_per_seq - 1)
    for cg in range(ncg):
        cs = slice(cg * 128, (cg + 1) * 128)
        buf_ref[cg, 0:CONV_HALO, :] = jnp.where(has_prev, up_ref[:, cs], 0.0)
        buf_ref[cg, CONV_HALO:CONV_HALO + tm, :] = uc_ref[:, cs]
        buf_ref[cg, CONV_HALO + tm:, :] = jnp.where(has_next, un_ref[:, cs], 0.0)

    off = CONV_HALO - CONV_K // 2

    def body(cg, carry):
        acc = jnp.zeros((tm, 128), F32)
        for tap in range(CONV_K):
            acc = acc + buf_ref[cg, off + tap:off + tap + tm, :] * w_ref[cg, tap:tap + 1, :]
        y_ref[cg] = acc + b_ref[cg]
        return carry

    lax.fori_loop(0, ncg, body, 0)

    n_ch = ncg * 128
    tot = y_ref[0]
    for cg in range(1, ncg):
        tot = tot + y_ref[cg]
    mu = jnp.sum(tot, axis=1, keepdims=True) * (1.0 / n_ch)
    sq = jnp.zeros((tm, 128), F32)
    for cg in range(ncg):
        dv = y_ref[cg] - mu
        sq = sq + dv * dv
    var = jnp.sum(sq, axis=1, keepdims=True) * (1.0 / n_ch)
    rstd = lax.rsqrt(var + LN_EPS)
    for cg in range(ncg):
        cs = slice(cg * 128, (cg + 1) * 128)
        v = (y_ref[cg] - mu) * rstd * g_ref[cg] + bt_ref[cg]
        o_ref[:, cs] = (v * _sigmoid(v)).astype(o_ref.dtype)


def _conv_module(u, conv_w, conv_b, ln_g, ln_b, tp, dec_seq):
    t, c = u.shape
    tm = TM_ROW
    ncg = c // 128
    npb = tp // tm
    hb = tm // CONV_HALO
    n_halo_blocks = t // CONV_HALO
    w3 = jnp.zeros((32, c), F32).at[:CONV_K].set(conv_w).reshape(32, ncg, 128).transpose(1, 0, 2)
    vec = lambda a: a.reshape(ncg, 1, 128)
    return pl.pallas_call(
        functools.partial(_conv_kernel, npb, dec_seq // tm),
        grid=(t // tm,),
        in_specs=[pl.BlockSpec((tm, c), lambda r: (r, 0)),
                  pl.BlockSpec((CONV_HALO, c), lambda r: (jnp.maximum(r * hb - 1, 0), 0)),
                  pl.BlockSpec((CONV_HALO, c), lambda r: (jnp.minimum((r + 1) * hb, n_halo_blocks - 1), 0)),
                  pl.BlockSpec((ncg, 32, 128), lambda r: (0, 0, 0)),
                  pl.BlockSpec((ncg, 1, 128), lambda r: (0, 0, 0)),
                  pl.BlockSpec((ncg, 1, 128), lambda r: (0, 0, 0)),
                  pl.BlockSpec((ncg, 1, 128), lambda r: (0, 0, 0))],
        out_specs=pl.BlockSpec((tm, c), lambda r: (r, 0)),
        out_shape=jax.ShapeDtypeStruct((t, c), BF16),
        scratch_shapes=[pltpu.VMEM((ncg, tm + 2 * CONV_HALO, 128), F32),
                        pltpu.VMEM((ncg, tm, 128), F32)],
        compiler_params=_params("arbitrary"),
        name="conv_ln_swish",
    )(u, u, u, w3, vec(conv_b), vec(ln_g), vec(ln_b))


def _retention_kernel(nc, has_init, lg_ref, q_ref, k_ref, v_ref, g_ref, gn_ref, *rest):
    if has_init:
        s0f_ref, s0b_ref, r_ref, o_ref, sf_ref, sb_ref = rest
    else:
        s0f_ref = s0b_ref = None
        r_ref, sf_ref, sb_ref, o_ref = rest
    hps = r_ref.shape[1] // RET_D
    for hh in range(hps):
        cols = slice(hh * RET_D, (hh + 1) * RET_D)
        _retention_head(nc, lg_ref, pl.program_id(1) * hps + hh,
                        q_ref.at[:, cols], k_ref.at[:, cols], v_ref.at[:, cols], g_ref.at[:, cols],
                        gn_ref.at[:, cols], None if s0f_ref is None else s0f_ref.at[hh],
                        None if s0b_ref is None else s0b_ref.at[hh],
                        r_ref.at[:, cols], sf_ref.at[hh], sb_ref.at[hh], o_ref.at[hh])


def _retention_head(nc, lg_ref, hd, q_ref, k_ref, v_ref, g_ref, gn_ref, s0f_ref, s0b_ref, r_ref, sf_ref, sb_ref, o_ref):
    has_init = s0f_ref is not None
    lgf, lgb = lg_ref[0, hd], lg_ref[1, hd]
    c = CHUNK
    row = lax.broadcasted_iota(jnp.int32, (c, c), 0)
    col = lax.broadcasted_iota(jnp.int32, (c, c), 1)
    diff = (row - col).astype(F32)
    dec = (jnp.where(diff >= 0, jnp.exp(jnp.maximum(diff, 0.0) * lgf), 0.0)
           + jnp.where(diff <= 0, jnp.exp(jnp.maximum(-diff, 0.0) * lgb), 0.0))
    pos = lax.broadcasted_iota(jnp.int32, (c, 1), 0).astype(F32)
    xi_f = jnp.exp((pos + 1.0) * lgf)
    zeta_f = jnp.exp((c - 1.0 - pos) * lgf)
    xi_b = jnp.exp((c - pos) * lgb)
    zeta_b = jnp.exp(pos * lgb)
    gch_f = jnp.exp(jnp.full((1, RET_D), c, F32) * lgf)
    gch_b = jnp.exp(jnp.full((1, RET_D), c, F32) * lgb)

    if has_init:
        sf_ref[...] = s0f_ref[...]
        sb_ref[...] = s0b_ref[...]
    else:
        sf_ref[...] = jnp.zeros_like(sf_ref)
        sb_ref[...] = jnp.zeros_like(sb_ref)

    def chunk(ci):
        sl = slice(ci * c, (ci + 1) * c)
        return q_ref[sl, :], k_ref[sl, :], v_ref[sl, :], sl

    def state_update(s_ref, kc, vc, zeta, gch):
        kz = (kc.astype(F32) * zeta).T.astype(BF16)
        s_ref[...] = gch * s_ref[...] + jnp.dot(kz, vc, preferred_element_type=F32)

    for ci in range(nc):
        qc, kc, vc, sl = chunk(ci)
        s = lax.dot_general(qc, kc, (((1,), (1,)), ((), ())), preferred_element_type=F32)
        p = (s * dec).astype(BF16)
        o = jnp.dot(p, vc, preferred_element_type=F32)
        o = o + jnp.dot(qc, sf_ref[...].astype(BF16), preferred_element_type=F32) * xi_f
        o_ref[sl, :] = o
        state_update(sf_ref, kc, vc, zeta_f, gch_f)

    for ci in reversed(range(nc)):
        qc, kc, vc, sl = chunk(ci)
        o_ref[sl, :] = o_ref[sl, :] + jnp.dot(qc, sb_ref[...].astype(BF16), preferred_element_type=F32) * xi_b
        state_update(sb_ref, kc, vc, zeta_b, gch_b)

    y = _ln_rows(o_ref[...])
    r_ref[...] = (g_ref[...].astype(F32) * (y * gn_ref[...])).astype(r_ref.dtype)


def _retention(qkvg, lg, gn, n, row_blk0, n_seq, s0f=None, s0b=None):
    has_init = s0f is not None
    nh, dd = RET_HEADS, RET_D
    hps = HEADS_PER_STEP if has_init else RET_HEADS
    wd = hps * dd
    col = lambda part: (lambda b, h: (b + row_blk0, part * (nh // hps) + h))
    st_spec = pl.BlockSpec((None, None, hps, dd, dd), lambda b, h: (b, 0, h, 0, 0))
    in_specs = [pl.BlockSpec(memory_space=pltpu.SMEM),
                pl.BlockSpec((n, wd), col(0)), pl.BlockSpec((n, wd), col(1)),
                pl.BlockSpec((n, wd), col(2)), pl.BlockSpec((n, wd), col(3)),
                pl.BlockSpec((1, wd), lambda b, h: (0, h))]
    args = [lg, qkvg, qkvg, qkvg, qkvg, gn]
    r_shape = jax.ShapeDtypeStruct((n_seq * n, nh * dd), BF16)
    r_spec = pl.BlockSpec((n, wd), lambda b, h: (b, h))
    st_scratch = pltpu.VMEM((hps, dd, dd), F32)
    if has_init:
        in_specs += [st_spec, st_spec]
        args += [s0f, s0b]
        out_specs, out_shape = r_spec, r_shape
        scratch = [pltpu.VMEM((hps, n, dd), F32), st_scratch, st_scratch]
    else:
        st_shape = jax.ShapeDtypeStruct((n_seq, 1, nh, dd, dd), F32)
        out_specs, out_shape = [r_spec, st_spec, st_spec], [r_shape, st_shape, st_shape]
        scratch = [pltpu.VMEM((hps, n, dd), F32)]
    return pl.pallas_call(
        functools.partial(_retention_kernel, n // CHUNK, has_init),
        grid=(n_seq, nh // hps),
        in_specs=in_specs, out_specs=out_specs, out_shape=out_shape,
        scratch_shapes=scratch,
        compiler_params=_params("arbitrary", "arbitrary"),
        name="retention_latent" if has_init else "retention_context",
    )(*args)


def _out_proj_kernel(npb, alpha, u_ref, r_ref, w1_ref, w2_ref, xp_ref, xs_ref, gt_ref, o_ref):
    i = pl.program_id(0)
    y = (jnp.dot(u_ref[...], w1_ref[...].astype(BF16), preferred_element_type=F32)
         + jnp.dot(r_ref[...], w2_ref[...].astype(BF16), preferred_element_type=F32))

    @pl.when(i < npb)
    def _():
        o_ref[...] = alpha * xp_ref[...] + gt_ref[...] * y

    @pl.when(i >= npb)
    def _():
        o_ref[...] = alpha * xs_ref[...] + gt_ref[...] * y


def _out_proj(u, r, w_out, xp, xs, gt, alpha, dec_seq):
    t, kh = u.shape
    d = w_out.shape[1]
    tm, tn = TM_PROJ, 512
    npb = xp.shape[0] // tm
    cidx = lambda i: jnp.where(i < npb, 0, 1 + (jnp.maximum(i - npb, 0) * tm) // dec_seq)
    return pl.pallas_call(
        functools.partial(_out_proj_kernel, npb, alpha),
        grid=(t // tm, d // tn),
        in_specs=[pl.BlockSpec((tm, kh), lambda i, j: (i, 0)),
                  pl.BlockSpec((tm, kh), lambda i, j: (i, 0)),
                  pl.BlockSpec((kh, tn), lambda i, j: (0, j)),
                  pl.BlockSpec((kh, tn), lambda i, j: (1, j)),
                  pl.BlockSpec((tm, tn), lambda i, j: (jnp.minimum(i, npb - 1), j)),
                  pl.BlockSpec((tm, tn), lambda i, j: (jnp.maximum(i - npb, 0), j)),
                  pl.BlockSpec((None, 1, tn), lambda i, j: (cidx(i), 0, j))],
        out_specs=pl.BlockSpec((tm, tn), lambda i, j: (i, j)),
        out_shape=jax.ShapeDtypeStruct((t, d), F32),
        compiler_params=_params("arbitrary", "arbitrary"),
        name="out_proj_residual",
    )(u, r, w_out, w_out, xp, xs, gt)


def _pack_bf16_pairs(x):
    n = x.shape[1] // 2
    lo = lax.bitcast_convert_type(x[:, :n].astype(BF16).astype(F32), jnp.uint32)
    hi = lax.bitcast_convert_type(x[:, n:].astype(BF16).astype(F32), jnp.uint32)
    return hi | (lo >> 16)


def _unpack_bf16_pairs(p):
    lo = lax.bitcast_convert_type(p << 16, F32)
    hi = lax.bitcast_convert_type(p & jnp.uint32(0xFFFF0000), F32)
    return jnp.concatenate([lo, hi], axis=1)


def _store_as_slabs(ref, rows):
    m, n = rows.shape
    slab = n // 128
    for s in range(slab):
        ref[pl.ds(s, m, stride=slab), :] = rows[:, s * 128:(s + 1) * 128]


def _load_from_slabs(ref, m):
    slab = ref.shape[0] // m
    return jnp.concatenate([ref[pl.ds(s, m, stride=slab), :] for s in range(slab)], axis=1)


def _slab_copy(src, dst, src_row, dst_row, slab, sem):
    return pltpu.make_async_copy(src.at[pl.ds(pl.multiple_of(src_row * slab, slab), slab), :],
                                 dst.at[pl.ds(pl.multiple_of(dst_row * slab, slab), slab), :], sem)


def _post_mix_kernel(v_ref, g1_ref, b1_ref, sc_ref, sh_ref, wr_ref, br_ref, x1_ref, h2_ref, rt_ref, cnt_ref, tab_ref):
    @pl.when(pl.program_id(0) == 0)
    def _():
        cnt_ref[...] = jnp.zeros_like(cnt_ref)

    x1 = _ln_rows(v_ref[...]) * g1_ref[...] + b1_ref[...]
    x1_ref[...] = x1
    h2 = (_ln_rows(x1) * (1.0 + sc_ref[...]) + sh_ref[...]).astype(BF16)
    h2_ref[...] = h2
    logits = jnp.dot(h2, wr_ref[...], preferred_element_type=F32) + br_ref[...]
    lane = lax.broadcasted_iota(jnp.int32, logits.shape, 1)
    big = jnp.int32(ROUTER_LANES)
    neg = jnp.float32(-jnp.inf)

    def first_lane_of_max(vals):
        m = jnp.max(vals, axis=1, keepdims=True)
        return m, jnp.min(jnp.where(vals == m, lane, big), axis=1, keepdims=True)

    lgt = jnp.where(lane < N_GROUPS, logits, neg)
    eg = jnp.exp(lgt - jnp.max(lgt, axis=1, keepdims=True))
    pg = eg / jnp.sum(eg, axis=1, keepdims=True)
    grp_prob, grp = first_lane_of_max(jnp.where(lane < N_GROUPS, pg, -1.0))
    lo = N_GROUPS + grp * EXPERTS_PER_GROUP
    in_grp = (lane >= lo) & (lane < lo + EXPERTS_PER_GROUP)
    let = jnp.where(in_grp, logits, neg)
    ee = jnp.exp(let - jnp.max(let, axis=1, keepdims=True))
    pe = jnp.where(in_grp, ee / jnp.sum(ee, axis=1, keepdims=True), -1.0)
    p1, l1 = first_lane_of_max(pe)
    p2, l2 = first_lane_of_max(jnp.where(lane == l1, -1.0, pe))
    den = p1 + p2
    gate1, gate2 = grp_prob * p1 / den, grp_prob * p2 / den
    e1, e2 = l1 - N_GROUPS, l2 - N_GROUPS
    tm = logits.shape[0]
    hot1 = (lane == e1).astype(F32)
    hot2 = (lane == e2).astype(F32)
    earlier = (lax.broadcasted_iota(jnp.int32, (tm, tm), 0) > lax.broadcasted_iota(jnp.int32, (tm, tm), 1)).astype(BF16)
    before1 = jnp.dot(earlier, hot1.astype(BF16), preferred_element_type=F32)
    n1 = jnp.sum(hot1, axis=0, keepdims=True)
    before2 = jnp.dot(earlier, hot2.astype(BF16), preferred_element_type=F32) + n1
    cnt = cnt_ref[...]
    slot1 = jnp.sum((before1 + cnt) * hot1, axis=1, keepdims=True)
    slot2 = jnp.sum((before2 + cnt) * hot2, axis=1, keepdims=True)
    n_blk = n1 + jnp.sum(hot2, axis=0, keepdims=True)
    lower = (lax.broadcasted_iota(jnp.int32, (ROUTER_LANES, ROUTER_LANES), 0)
             < lax.broadcasted_iota(jnp.int32, (ROUTER_LANES, ROUTER_LANES), 1)).astype(BF16)
    start = jnp.dot(jnp.broadcast_to(n_blk, (8, ROUTER_LANES)).astype(BF16), lower, preferred_element_type=F32)[0:1]
    local1 = jnp.sum((before1 + start) * hot1, axis=1, keepdims=True)
    local2 = jnp.sum((before2 + start) * hot2, axis=1, keepdims=True)
    cnt_ref[...] = cnt + n_blk
    vals = (e1.astype(F32), e2.astype(F32), gate1, gate2, slot1, slot2, local1, local2)
    out = jnp.zeros_like(logits)
    for k, val in enumerate(vals):
        out = jnp.where(lane == k, val, out)
    rt_ref[...] = out
    sub = lax.broadcasted_iota(jnp.int32, (8, ROUTER_LANES), 0)
    tab_ref[...] = jnp.where(sub == 0, n_blk, jnp.where(sub == 1, start, jnp.where(sub == 2, cnt, 0.0)))


def _post_mix(v, ln_g, ln_b, sc, sh, w_router, b_router, tp, dec_seq):
    t, d = v.shape
    tm = TM_ROW
    npb = tp // tm
    cidx = lambda i: jnp.where(i < npb, 0, 1 + (jnp.maximum(i - npb, 0) * tm) // dec_seq)
    row = pl.BlockSpec((tm, d), lambda i: (i, 0))
    vec = pl.BlockSpec((1, d), lambda i: (0, 0))
    cvec = pl.BlockSpec((None, 1, d), lambda i: (cidx(i), 0, 0))
    return pl.pallas_call(
        _post_mix_kernel,
        grid=(t // tm,),
        in_specs=[row, vec, vec, cvec, cvec,
                  pl.BlockSpec((d, ROUTER_LANES), lambda i: (0, 0)),
                  pl.BlockSpec((1, ROUTER_LANES), lambda i: (0, 0))],
        out_specs=[row, row,
                   pl.BlockSpec((tm, ROUTER_LANES), lambda i: (i, 0)),
                   pl.BlockSpec((1, ROUTER_LANES), lambda i: (0, 0)),
                   pl.BlockSpec((None, 8, ROUTER_LANES), lambda i: (i, 0, 0))],
        out_shape=[jax.ShapeDtypeStruct((t, d), F32), jax.ShapeDtypeStruct((t, d), BF16),
                   jax.ShapeDtypeStruct((t, ROUTER_LANES), F32), jax.ShapeDtypeStruct((1, ROUTER_LANES), F32),
                   jax.ShapeDtypeStruct((t // tm, 8, ROUTER_LANES), F32)],
        compiler_params=_params("arbitrary"),
        name="ln_ln_router",
    )(v, ln_g.reshape(1, d), ln_b.reshape(1, d), sc, sh, w_router, b_router)


def _on_parity(blk, fn):
    for par in range(2):
        pl.when(lax.rem(blk, 2) == par)(functools.partial(fn, blk, par))


def _dispatch_kernel(run_n_ref, run_src_ref, run_dst_ref, pad_row_ref, pad_n_ref, nact_ref,
                     h_ref, rt_ref, o_hbm, stage_ref, zero_ref, sems):
    tb = pl.program_id(0)
    n_tb = pl.num_programs(0)
    tm, d = h_ref.shape
    slab = d // 2 // 128
    n_rows_out = o_hbm.shape[0] // slab
    fill_sem = sems.at[2]

    def pieces(wait, src_ref, src_row, dst_row, n, max_size, sem):
        done = jnp.int32(0)
        size = max_size
        while size >= 1:
            take = (n & size) != 0
            src0 = 0 if src_row is None else pl.multiple_of((src_row + done) * slab, slab)
            cp = pltpu.make_async_copy(
                src_ref.at[pl.ds(src0, size * slab), :],
                o_hbm.at[pl.ds(pl.multiple_of((dst_row + done) * slab, slab), size * slab), :], sem)
            pl.when(take)(cp.wait if wait else cp.start)
            done = done + jnp.where(take, size, 0)
            size //= 2

    def runs(wait, blk, par):
        def body(e, carry):
            k = blk * N_EXPERTS + e
            pieces(wait, stage_ref.at[par], run_src_ref[k], run_dst_ref[k], run_n_ref[k], tm, sems.at[par])
            return carry
        lax.fori_loop(0, N_EXPERTS, body, 0)

    def fill(wait):
        def tail(e, carry):
            pieces(wait, zero_ref, None, pad_row_ref[e], pad_n_ref[e], tm // 2, fill_sem)
            return carry

        def spare(b, carry):
            pieces(wait, zero_ref, None, b * tm, jnp.int32(tm), tm, fill_sem)
            return carry

        lax.fori_loop(0, N_EXPERTS, tail, 0)
        lax.fori_loop(nact_ref[0], n_rows_out // tm, spare, 0)

    @pl.when(tb == 0)
    def _():
        zero_ref[...] = jnp.zeros_like(zero_ref)
        fill(False)

    @pl.when(tb >= 2)
    def _():
        _on_parity(tb, lambda blk, par: runs(True, blk - 2, par))

    rt = rt_ref[...]
    local1 = rt[:, 6:7].astype(jnp.int32)
    local2 = rt[:, 7:8].astype(jnp.int32)
    place = lax.broadcasted_iota(jnp.int32, (tm, TOP_K * tm), 1)
    onehot = ((place == local1) | (place == local2)).astype(F32)
    xp = jnp.dot(onehot.T.astype(BF16), h_ref[...], preferred_element_type=F32)
    packed = (lax.bitcast_convert_type(xp[:, d // 2:], jnp.uint32)
              | (lax.bitcast_convert_type(xp[:, :d // 2], jnp.uint32) >> 16))
    _store_as_slabs(stage_ref.at[lax.rem(tb, 2)], packed)
    _on_parity(tb, functools.partial(runs, False))

    @pl.when(tb == n_tb - 1)
    def _():
        _on_parity(tb - 1, functools.partial(runs, True))
        _on_parity(tb, functools.partial(runs, True))
        fill(True)


def _dispatch_rows(h2, route, tables, n_blocks):
    t, d = h2.shape
    tm = TM_ROW
    slab = d // 2 // 128
    assert tm == TM_MOE and t // tm >= 2
    return pl.pallas_call(
        _dispatch_kernel,
        grid_spec=pltpu.PrefetchScalarGridSpec(
            num_scalar_prefetch=len(tables),
            grid=(t // tm,),
            in_specs=[pl.BlockSpec((tm, d), lambda i, *_: (i, 0)),
                      pl.BlockSpec((tm, ROUTER_LANES), lambda i, *_: (i, 0))],
            out_specs=pl.BlockSpec(memory_space=pl.ANY),
            scratch_shapes=[pltpu.VMEM((2, TOP_K * tm * slab, 128), jnp.uint32),
                            pltpu.VMEM((tm * slab, 128), jnp.uint32), pltpu.SemaphoreType.DMA((3,))]),
        out_shape=jax.ShapeDtypeStruct((n_blocks * TM_MOE * slab, 128), jnp.uint32),
        compiler_params=_params("arbitrary"),
        name="moe_dispatch",
    )(*tables, h2, route)


def _expert_up_kernel(se_ref, sj_ref, sb_ref, oj_ref, ob_ref, fst_ref, gp_ref, nxt_ref, ne_ref, nj_ref, ns_ref,
                      x_ref, wg_hbm, wu_hbm, o_ref, wg_st, wu_st, sems):
    live = pl.program_id(0) < ns_ref[0]
    _stream_group_weights(se_ref, sj_ref, fst_ref, gp_ref, nxt_ref, ne_ref, nj_ref, live,
                          ((wg_hbm, wg_st), (wu_hbm, wu_st)), sems)

    @pl.when(live)
    def _():
        x = _unpack_bf16_pairs(_load_from_slabs(x_ref, TM_MOE))
        slot = gp_ref[pl.program_id(0)]
        a = jnp.dot(x, wg_st[slot], preferred_element_type=F32)
        u = jnp.dot(x, wu_st[slot], preferred_element_type=F32)
        o_ref[...] = ((a * _sigmoid(a)) * u).astype(o_ref.dtype)

    @pl.when(jnp.logical_not(live))
    def _():
        o_ref[...] = jnp.zeros_like(o_ref)


def _stream_group_weights(se_ref, sj_ref, fst_ref, gp_ref, nxt_ref, ne_ref, nj_ref, live, weights, sems):
    s = pl.program_id(0)

    def copies(e, j, slot):
        out = []
        for k, (w_hbm, w_st) in enumerate(weights):
            tn = w_st.shape[2]
            src = w_hbm.at[e, :, pl.ds(pl.multiple_of(j * tn, tn), tn)]
            out.append(pltpu.make_async_copy(src, w_st.at[slot], sems.at[k, slot]))
        return out

    def first_step(slot):
        cur = copies(se_ref[s], sj_ref[s], slot)

        @pl.when(s == 0)
        def _():
            for cp in cur:
                cp.start(priority=1)

        @pl.when(nxt_ref[s] == 1)
        def _():
            for cp in copies(ne_ref[s], nj_ref[s], 1 - slot):
                cp.start(priority=1)

        for cp in cur:
            cp.wait()

    for slot in range(2):
        pl.when(live & (fst_ref[s] == 1) & (gp_ref[s] == slot))(functools.partial(first_step, slot))


_N_SCHED = 11


def _sched_map(fn):
    return lambda s, *refs: fn(s, *refs[:_N_SCHED])


def _expert_up(xs, w_gate, w_up, sched, n_steps):
    d, de = w_gate.shape[1], w_gate.shape[2]
    tm, tf = TM_MOE, TF_MOE
    slab = d // 2 // 128
    r = xs.shape[0] // slab
    return pl.pallas_call(
        _expert_up_kernel,
        grid_spec=pltpu.PrefetchScalarGridSpec(
            num_scalar_prefetch=_N_SCHED,
            grid=(n_steps,),
            in_specs=[pl.BlockSpec((tm * slab, 128), _sched_map(lambda s, se, sj, sb, *_: (sb[s], 0))),
                      pl.BlockSpec(memory_space=pl.ANY), pl.BlockSpec(memory_space=pl.ANY)],
            out_specs=pl.BlockSpec((tm, tf), _sched_map(lambda s, se, sj, sb, oj, ob, *_: (ob[s], oj[s]))),
            scratch_shapes=[pltpu.VMEM((2, d, tf), F32), pltpu.VMEM((2, d, tf), F32),
                            pltpu.SemaphoreType.DMA((2, 2))]),
        out_shape=jax.ShapeDtypeStruct((r, de), BF16),
        compiler_params=_params("arbitrary"),
        name="moe_gate_up",
    )(*sched, xs, w_gate, w_up)


def _expert_down_kernel(se_ref, sj_ref, sb_ref, oj_ref, ob_ref, fst_ref, gp_ref, nxt_ref, ne_ref, nj_ref, ns_ref,
                        h_ref, wd_hbm, o_ref, wd_st, sems):
    live = pl.program_id(0) < ns_ref[0]
    _stream_group_weights(se_ref, sj_ref, fst_ref, gp_ref, nxt_ref, ne_ref, nj_ref, live, ((wd_hbm, wd_st),), sems)

    @pl.when(live)
    def _():
        y = jnp.dot(h_ref[...].astype(F32), wd_st[gp_ref[pl.program_id(0)]], preferred_element_type=F32)
        _store_as_slabs(o_ref, _pack_bf16_pairs(y))

    @pl.when(jnp.logical_not(live))
    def _():
        o_ref[...] = jnp.zeros_like(o_ref)


def _expert_down(hid, w_down, sched, n_steps):
    r, de = hid.shape
    d = w_down.shape[2]
    tm = TM_MOE
    slab = d // 2 // 128
    return pl.pallas_call(
        _expert_down_kernel,
        grid_spec=pltpu.PrefetchScalarGridSpec(
            num_scalar_prefetch=_N_SCHED,
            grid=(n_steps,),
            in_specs=[pl.BlockSpec((tm, de), _sched_map(lambda s, se, sj, sb, *_: (sb[s], 0))),
                      pl.BlockSpec(memory_space=pl.ANY)],
            out_specs=pl.BlockSpec((tm * slab, 128), _sched_map(lambda s, se, sj, sb, oj, ob, *_: (ob[s], 0))),
            scratch_shapes=[pltpu.VMEM((2, de, d), F32), pltpu.SemaphoreType.DMA((1, 2))]),
        out_shape=jax.ShapeDtypeStruct((r * slab, 128), jnp.uint32),
        compiler_params=_params("arbitrary"),
        name="moe_down",
    )(*sched, hid, w_down)


def _combine_kernel(npb, alpha, pos_ref, y_hbm, x1_ref, rt_ref, gt_ref, g_ref, b_ref,
                    op_ref, os_ref, buf_ref, sems):
    i = pl.program_id(0)
    n = pl.num_programs(0)
    tm = x1_ref.shape[0]
    slab = buf_ref.shape[2] // tm

    def rows(wait, blk, par):
        def body(g, carry):
            for u in range(ROW_UNROLL):
                r = g * ROW_UNROLL + u
                for k in range(TOP_K):
                    cp = _slab_copy(y_hbm, buf_ref.at[par, k], pos_ref[(blk * tm + r) * TOP_K + k], r, slab,
                                    sems.at[par])
                    cp.wait() if wait else cp.start(priority=k % 2)
            return carry
        lax.fori_loop(0, tm // ROW_UNROLL, body, 0)

    @pl.when(i == 0)
    def _():
        rows(False, 0, 0)

    @pl.when(i + 1 < n)
    def _():
        _on_parity(i + 1, functools.partial(rows, False))

    _on_parity(i, functools.partial(rows, True))
    rt = rt_ref[...]
    par = lax.rem(i, 2)
    expert_rows = lambda k: _unpack_bf16_pairs(_load_from_slabs(buf_ref.at[par, k], tm))
    f = rt[:, 2:3] * expert_rows(0) + rt[:, 3:4] * expert_rows(1)
    out = _ln_rows(alpha * x1_ref[...] + gt_ref[...] * f) * g_ref[...] + b_ref[...]

    @pl.when(i < npb)
    def _():
        op_ref[...] = out

    @pl.when(i >= npb)
    def _():
        os_ref[...] = out


def _combine(yb, pos, x1, route, gt, ln_g, ln_b, alpha, tp, dec_seq):
    t, d = x1.shape
    tm = TM_ROW
    npb = tp // tm
    cidx = lambda i: jnp.where(i < npb, 0, 1 + (jnp.maximum(i - npb, 0) * tm) // dec_seq)
    vec = pl.BlockSpec((1, d), lambda i, *_: (0, 0))
    return pl.pallas_call(
        functools.partial(_combine_kernel, npb, alpha),
        grid_spec=pltpu.PrefetchScalarGridSpec(
            num_scalar_prefetch=1,
            grid=(t // tm,),
            in_specs=[pl.BlockSpec(memory_space=pl.ANY),
                      pl.BlockSpec((tm, d), lambda i, *_: (i, 0)),
                      pl.BlockSpec((tm, ROUTER_LANES), lambda i, *_: (i, 0)),
                      pl.BlockSpec((None, 1, d), lambda i, *_: (cidx(i), 0, 0)),
                      vec, vec],
            out_specs=[pl.BlockSpec((tm, d), lambda i, *_: (jnp.minimum(i, npb - 1), 0)),
                       pl.BlockSpec((tm, d), lambda i, *_: (jnp.maximum(i - npb, 0), 0))],
            scratch_shapes=[pltpu.VMEM((2, TOP_K, tm * (d // 2 // 128), 128), jnp.uint32),
                            pltpu.SemaphoreType.DMA((2,))]),
        out_shape=[jax.ShapeDtypeStruct((tp, d), F32), jax.ShapeDtypeStruct((t - tp, d), F32)],
        compiler_params=_params("arbitrary"),
        name="moe_combine_ln",
    )(pos, yb, x1, route, gt, ln_g.reshape(1, d), ln_b.reshape(1, d))


def _dispatch_plan(route, counts_f, block_tab, n_tok):
    tm = TM_MOE
    n_blocks = n_tok * TOP_K // tm + N_EXPERTS
    i32 = jnp.int32
    eid = route[:, 0:TOP_K].astype(i32).reshape(-1)
    slot = route[:, 4:4 + TOP_K].astype(i32).reshape(-1)
    counts = counts_f[0, :N_EXPERTS].astype(i32)
    nb = (counts + tm - 1) // tm

    def cumsum(x):
        idx = jnp.arange(x.shape[0], dtype=i32)
        return jnp.sum(jnp.where(idx[None, :] <= idx[:, None], x[None, :], 0), axis=1)

    nb_end = cumsum(nb)
    bs = nb_end - nb
    n_act = nb_end[-1]
    experts = jnp.arange(N_EXPERTS, dtype=i32)
    look = lambda table, e: jnp.sum(jnp.where(e[:, None] == experts[None, :], table[None, :], 0), axis=1)
    tab = block_tab[:, :3, :N_EXPERTS].astype(i32)
    run_dst = (bs * tm)[None, :] + tab[:, 2]
    dispatch_tables = (tab[:, 0].reshape(-1), tab[:, 1].reshape(-1), run_dst.reshape(-1),
                       bs * tm + counts, nb * tm - counts, n_act.reshape(1))

    def schedule(n_inner):
        n_steps = n_inner * n_blocks
        n_live = n_inner * n_act

        def decode(step):
            s = jnp.minimum(step, n_live - 1)
            e = jnp.minimum(jnp.sum((s[:, None] >= n_inner * nb_end[None, :]).astype(i32), axis=1), N_EXPERTS - 1)
            nbe = jnp.maximum(look(nb, e), 1)
            loc = s - n_inner * look(bs, e)
            return e, loc // nbe, loc % nbe, nbe, look(bs, e)

        step = jnp.arange(n_steps, dtype=i32)
        e, sj, bi, nbe, bse = decode(step)
        live = step < n_live
        nxt_step = jnp.minimum(step, n_live - 1) + nbe - bi
        ne, nj, _, _, _ = decode(nxt_step)
        fst = (live & (bi == 0)).astype(i32)
        gp = (cumsum(fst) - 1) % 2
        nxt = (live & (bi == 0) & (nxt_step < n_live)).astype(i32)
        spare = jnp.maximum(step - n_live, 0)
        oj = jnp.where(live, sj, spare % n_inner)
        ob = jnp.where(live, bse + bi, n_act + spare // n_inner)
        sched = (e, sj, bse + bi, oj, ob, fst, gp, nxt, ne, nj, n_live.reshape(1))
        return tuple(a.astype(i32) for a in sched), n_steps

    pos = look(bs * tm, eid) + slot
    return pos, dispatch_tables, n_blocks, schedule


def kernel(x_prompt, x_sample, state_ret_fwd, state_ret_bwd, c, c_ctx, w_mod, b_mod, w_in, conv_w, conv_b, conv_ln_g, conv_ln_b, ret_decay_fwd, ret_decay_bwd, ret_gn_g, w_out, ln1_g, ln1_b, w_grp, b_grp, w_exp, b_exp, w_gate, w_up, w_down, ln2_g, ln2_b):
    depth = w_mod.shape[0]
    assert depth == 1, "single-layer step"
    bp, sp, d = x_prompt.shape
    bs_, ss, _ = x_sample.shape
    tp, ts = bp * sp, bs_ * ss
    t = tp + ts
    conv_width = conv_w.shape[2]
    ret_width = ret_gn_g.shape[1]
    assert ret_width == RET_HEADS * RET_D and sp % CHUNK == 0 and ss % CHUNK == 0
    alpha = (2.0 * depth) ** 0.25

    xp = x_prompt.reshape(tp, d)
    xs = x_sample.reshape(ts, d)
    cond8 = jnp.zeros((8, d), F32).at[0].set(c_ctx).at[1:1 + bs_].set(c)
    m = _modulation(cond8, w_mod[0], b_mod[0])
    sh1, sc1, gt1, sh2, sc2, gt2 = [m[:, k * d:(k + 1) * d].reshape(8, 1, d) for k in range(6)]

    h = _ln_modulate(xp, xs, sc1, sh1, ss)
    u_glu = _glu_proj(h, w_in[0], conv_width)
    cos_t, sin_t = _rope_tables(ss, RET_D)
    qkvg = _qkvg_proj(h, w_in[0], cos_t, sin_t, 2 * conv_width, 4 * ret_width, tp)

    u = _conv_module(u_glu, conv_w[0], conv_b[0], conv_ln_g[0], conv_ln_b[0], tp, ss)

    lg = jnp.stack([jax.nn.log_sigmoid(ret_decay_fwd[0].astype(F32)),
                    jax.nn.log_sigmoid(ret_decay_bwd[0].astype(F32))])
    gn = ret_gn_g[0].reshape(1, ret_width)
    r_p, new_f, new_b = _retention(qkvg, lg, gn, sp, 0, bp)
    r_s = _retention(qkvg, lg, gn, ss, tp // ss, bs_, state_ret_fwd, state_ret_bwd)
    r = jnp.concatenate([r_p, r_s], axis=0)

    v = _out_proj(u, r, w_out[0], xp, xs, gt1, alpha, ss)

    w_router = jnp.zeros((d, ROUTER_LANES), F32).at[:, :N_GROUPS].set(w_grp[0]).at[:, N_GROUPS:N_GROUPS + N_EXPERTS].set(w_exp[0]).astype(BF16)
    b_router = jnp.zeros((1, ROUTER_LANES), F32).at[0, :N_GROUPS].set(b_grp[0]).at[0, N_GROUPS:N_GROUPS + N_EXPERTS].set(b_exp[0])
    x1, h2, route, counts, block_tab = _post_mix(v, ln1_g[0], ln1_b[0], sc2, sh2, w_router, b_router, tp, ss)

    pos, dispatch_tables, n_blocks, schedule = _dispatch_plan(route, counts, block_tab, t)
    xg = _dispatch_rows(h2, route, dispatch_tables, n_blocks)
    sched_up, n_up = schedule(w_gate.shape[3] // TF_MOE)
    hid = _expert_up(xg, w_gate[0], w_up[0], sched_up, n_up)
    sched_dn, n_dn = schedule(1)
    yb = _expert_down(hid, w_down[0], sched_dn, n_dn)

    out_p, out_s = _combine(yb, pos, x1, route, gt2, ln2_g[0], ln2_b[0], alpha, tp, ss)
    return (out_p.reshape(bp, sp, d), out_s.reshape(bs_, ss, d), new_f, new_b)
```
